```python
import math
import jax, jax.numpy as jnp
from jax import lax
import numpy as np

D_MODEL = 2048
BATCH = 4
SEQ = 2048
DEPTH = 2
DEC_BATCH = 128
DEC_SEQ = 1
PAST_LEN = 16384
PAGE_SIZE = 128

HG_HEADS = 8
HG_DK = 128
HG_DV = 128
HG_F = HG_HEADS * HG_DK
HG_I = HG_HEADS * HG_DV
HG_CHUNK = 64
RG_WIDTH = 1024
RG_BLOCKS = 8
RG_BW = RG_WIDTH // RG_BLOCKS
CONV_W = 4
RG_C = 8.0
IN_COLS = 2 * HG_F + 2 * HG_I + 2 * RG_WIDTH + 2 * D_MODEL
D_FF = 5504
N_EXPERTS = 8
TOP_K = 2
D_FF_EXPERT = 7168
N_DENSE = (DEPTH + 1) // 2
N_MOE = DEPTH // 2
EPS = 1e-6

kernel_name = "hybrid_hgrn2_rglru_gated_merge_step"


def _split_points():
    sizes = [HG_F, HG_F, HG_I, HG_I, RG_WIDTH, RG_WIDTH, D_MODEL, D_MODEL]
    pts, acc = [], 0
    for s in sizes[:-1]:
        acc += s
        pts.append(acc)
    return pts


def rms_norm(x, g):
    xf = x.astype(jnp.float32)
    y = xf * lax.rsqrt(jnp.mean(xf * xf, axis=-1, keepdims=True) + EPS)
    return (y * g.astype(jnp.float32)).astype(x.dtype)


def hgrn_lower_bounds(lb_logits):
    s = jax.nn.softmax(lb_logits.astype(jnp.float32), axis=0)
    return jnp.maximum(jnp.cumsum(s, axis=0) - s[0:1], 0.0)


def hgrn2_chunked(q, logf, i, s0):
    B, H, L, K = q.shape
    V = i.shape[-1]
    C = math.gcd(L, HG_CHUNK)
    n = L // C
    k = -jnp.expm1(logf)

    def to_chunks(a):
        return jnp.moveaxis(a.reshape(B, H, n, C, a.shape[-1]), 2, 0)

    mask = jnp.tril(jnp.ones((C, C), dtype=bool))[None, None, :, :, None]

    def step(S, inp):
        qc, gc, kc, ic = inp
        b = jnp.cumsum(gc, axis=2)
        diff = b[:, :, :, None, :] - b[:, :, None, :, :]
        decay = jnp.exp(jnp.where(mask, diff, -jnp.inf))
        A = jnp.einsum('bhtk,bhsk,bhtsk->bhts', qc, kc, decay)
        o = jnp.einsum('bhts,bhsv->bhtv', A, ic) + jnp.einsum('bhtk,bhkv->bhtv', qc * jnp.exp(b), S)
        bC = b[:, :, -1:, :]
        S_new = jnp.exp(bC[:, :, 0, :])[..., None] * S + jnp.einsum('bhsk,bhsv->bhkv', kc * jnp.exp(bC - b), ic)
        return S_new, o

    S_fin, o = lax.scan(step, s0, (to_chunks(q), to_chunks(logf), to_chunks(k), to_chunks(i)))
    o = jnp.moveaxis(o, 0, 2).reshape(B, H, L, V)
    return o, S_fin


def rg_branch(xr, gate, conv_buf, h0, conv_w, conv_b, wa, ba, wx, bx, a_param):
    B, L, R = xr.shape
    xp = jnp.concatenate([conv_buf.astype(xr.dtype), xr], axis=1)
    conv = conv_b
    for j in range(CONV_W):
        conv = conv + xp[:, j:j + L] * conv_w[j]
    new_buf = xp[:, L:]
    xb = conv.reshape(B, L, RG_BLOCKS, RG_BW)
    r = jax.nn.sigmoid((jnp.einsum('blnc,ncd->blnd', xb, wa) + ba).astype(jnp.float32)).reshape(B, L, R)
    ig = jax.nn.sigmoid((jnp.einsum('blnc,ncd->blnd', xb, wx) + bx).astype(jnp.float32)).reshape(B, L, R)
    log_a = RG_C * r * jax.nn.log_sigmoid(a_param.astype(jnp.float32))
    a = jnp.exp(log_a)
    b = jnp.sqrt(-jnp.expm1(2.0 * log_a)) * ig * conv.astype(jnp.float32)
    b = b.at[:, 0].add(a[:, 0] * h0.astype(jnp.float32))

    def comb(left, right):
        a1, b1 = left
        a2, b2 = right
        return a1 * a2, a2 * b1 + b2

    _, h = lax.associative_scan(comb, (a, b), axis=1)
    y = (h * jax.nn.gelu(gate.astype(jnp.float32), approximate=True)).astype(xr.dtype)
    return y, h[:, -1], new_buf


def mixer(h, lb, hg_s0, rg_h0, conv_buf, w_in, hg_gnorm, conv_w, conv_b, wa, ba, wx, bx, a_param,
          w_br_a, w_br_b, w_out):
    B, L, _ = h.shape
    proj = h @ w_in
    q, fz, iv, g, xr, gate, ma, mb = jnp.split(proj, _split_points(), axis=-1)
    logf = jnp.logaddexp(jnp.log(lb), jnp.log1p(-lb) + jax.nn.log_sigmoid(fz.astype(jnp.float32)))

    def heads(a, d):
        return a.reshape(B, L, HG_HEADS, d).transpose(0, 2, 1, 3).astype(jnp.float32)

    o, s_new = hgrn2_chunked(heads(q, HG_DK), heads(logf, HG_DK), heads(iv, HG_DV), hg_s0.astype(jnp.float32))
    o = o.transpose(0, 2, 1, 3)
    o = o * lax.rsqrt(jnp.mean(o * o, axis=-1, keepdims=True) + EPS) * hg_gnorm.reshape(HG_HEADS, HG_DV).astype(jnp.float32)
    o = (o.reshape(B, L, HG_I) * jax.nn.silu(g.astype(jnp.float32))).astype(h.dtype)
    y_rg, h_last, new_buf = rg_branch(xr, gate, conv_buf, rg_h0, conv_w, conv_b, wa, ba, wx, bx, a_param)
    merged = jax.nn.sigmoid(ma) * (o @ w_br_a) + jax.nn.sigmoid(mb) * (y_rg @ w_br_b)
    return merged @ w_out, s_new, h_last, new_buf


def swiglu(h, w_gate, w_up, w_down):
    return (jax.nn.silu(h @ w_gate) * (h @ w_up)) @ w_down


def moe(h, w_router, w_gate, w_up, w_down):
    logits = (h @ w_router).astype(jnp.float32)
    probs = jax.nn.softmax(logits, axis=-1)
    top_vals, top_idx = lax.top_k(probs, TOP_K)
    top_vals = top_vals / jnp.sum(top_vals, axis=-1, keepdims=True)
    gates = jnp.sum(jax.nn.one_hot(top_idx, N_EXPERTS, dtype=jnp.float32) * top_vals[..., None], axis=-2)
    gates = gates.astype(h.dtype)
    out = jnp.zeros_like(h)
    for e in range(N_EXPERTS):
        out = out + gates[..., e:e + 1] * swiglu(h, w_gate[e], w_up[e], w_down[e])
    return out


def setup_inputs(seed: int = 0) -> dict:
    key = jax.random.key(seed)
    ks = iter(jax.random.split(key, 40))

    def nrm(shape, scale):
        return jax.random.normal(next(ks), shape, jnp.float32) * scale

    u = jax.random.uniform(next(ks), (DEPTH, RG_WIDTH), jnp.float32, 0.9, 0.999)
    base = u ** (1.0 / RG_C)
    a_param = jnp.log(base) - jnp.log1p(-base)
    return {
        "x_prompt": nrm((BATCH, SEQ, D_MODEL), 1.0),
        "x_sample": nrm((DEC_BATCH, DEC_SEQ, D_MODEL), 1.0),
        "state_hgrn": nrm((DEPTH, DEC_BATCH, HG_HEADS, HG_DK, HG_DV), 0.5),
        "state_rglru": nrm((DEPTH, DEC_BATCH, RG_WIDTH), 0.5),
        "state_conv": nrm((DEPTH, DEC_BATCH, CONV_W - 1, RG_WIDTH), 1.0),
        "hgrn_lb_logits": nrm((DEPTH, HG_F), 0.1),
        "norm_mix": 1.0 + nrm((DEPTH, D_MODEL), 0.02),
        "w_in": nrm((DEPTH, D_MODEL, IN_COLS), D_MODEL ** -0.5),
        "hgrn_gnorm": 1.0 + nrm((DEPTH, HG_I), 0.02),
        "rg_conv_w": nrm((DEPTH, CONV_W, RG_WIDTH), CONV_W ** -0.5),
        "rg_conv_b": nrm((DEPTH, RG_WIDTH), 0.01),
        "rg_wa": nrm((DEPTH, RG_BLOCKS, RG_BW, RG_BW), RG_BW ** -0.5),
        "rg_ba": nrm((DEPTH, RG_BLOCKS, RG_BW), 0.01),
        "rg_wx": nrm((DEPTH, RG_BLOCKS, RG_BW, RG_BW), RG_BW ** -0.5),
        "rg_bx": nrm((DEPTH, RG_BLOCKS, RG_BW), 0.01),
        "rg_a_param": a_param,
        "w_br_a": nrm((DEPTH, HG_I, D_MODEL), HG_I ** -0.5),
        "w_br_b": nrm((DEPTH, RG_WIDTH, D_MODEL), RG_WIDTH ** -0.5),
        "w_out": nrm((DEPTH, D_MODEL, D_MODEL), D_MODEL ** -0.5),
        "norm_ffn": 1.0 + nrm((DEPTH, D_MODEL), 0.02),
        "ffn_w_gate": nrm((N_DENSE, D_MODEL, D_FF), D_MODEL ** -0.5),
        "ffn_w_up": nrm((N_DENSE, D_MODEL, D_FF), D_MODEL ** -0.5),
        "ffn_w_down": nrm((N_DENSE, D_FF, D_MODEL), D_FF ** -0.5),
        "moe_router": nrm((N_MOE, D_MODEL, N_EXPERTS), D_MODEL ** -0.5),
        "moe_w_gate": nrm((N_MOE, N_EXPERTS, D_MODEL, D_FF_EXPERT), D_MODEL ** -0.5),
        "moe_w_up": nrm((N_MOE, N_EXPERTS, D_MODEL, D_FF_EXPERT), D_MODEL ** -0.5),
        "moe_w_down": nrm((N_MOE, N_EXPERTS, D_FF_EXPERT, D_MODEL), D_FF_EXPERT ** -0.5),
        "norm_final": 1.0 + nrm((D_MODEL,), 0.02),
    }


def reference(x_prompt, x_sample, state_hgrn, state_rglru, state_conv, hgrn_lb_logits, norm_mix, w_in,
              hgrn_gnorm, rg_conv_w, rg_conv_b, rg_wa, rg_ba, rg_wx, rg_bx, rg_a_param, w_br_a, w_br_b,
              w_out, norm_ffn, ffn_w_gate, ffn_w_up, ffn_w_down, moe_router, moe_w_gate, moe_w_up,
              moe_w_down, norm_final):
    lbs = hgrn_lower_bounds(hgrn_lb_logits)
    Bp = x_prompt.shape[0]
    xp, xs = x_prompt, x_sample
    hg_p, rg_p, cv_p, hg_s, rg_s, cv_s = [], [], [], [], [], []
    for l in range(DEPTH):
        zero_hg = jnp.zeros((Bp, HG_HEADS, HG_DK, HG_DV), jnp.float32)
        zero_rg = jnp.zeros((Bp, RG_WIDTH), jnp.float32)
        zero_cv = jnp.zeros((Bp, CONV_W - 1, RG_WIDTH), xp.dtype)
        groups = ((xp, zero_hg, zero_rg, zero_cv), (xs, state_hgrn[l], state_rglru[l], state_conv[l]))
        outs = []
        for x, s0, h0, buf in groups:
            h = rms_norm(x, norm_mix[l])
            mix, s_new, h_new, buf_new = mixer(h, lbs[l], s0, h0, buf, w_in[l], hgrn_gnorm[l], rg_conv_w[l],
                                               rg_conv_b[l], rg_wa[l], rg_ba[l], rg_wx[l], rg_bx[l],
                                               rg_a_param[l], w_br_a[l], w_br_b[l], w_out[l])
            x = x + mix
            h2 = rms_norm(x, norm_ffn[l])
            if l % 2 == 0:
                j = l // 2
                x = x + swiglu(h2, ffn_w_gate[j], ffn_w_up[j], ffn_w_down[j])
            else:
                j = l // 2
                x = x + moe(h2, moe_router[j], moe_w_gate[j], moe_w_up[j], moe_w_down[j])
            outs.append((x, s_new, h_new, buf_new))
        (xp, a1, a2, a3), (xs, b1, b2, b3) = outs
        hg_p.append(a1); rg_p.append(a2); cv_p.append(a3)
        hg_s.append(b1); rg_s.append(b2); cv_s.append(b3)
    y_prompt = rms_norm(xp, norm_final)
    y_sample = rms_norm(xs, norm_final)
    return (y_prompt, y_sample, jnp.stack(hg_p), jnp.stack(rg_p), jnp.stack(cv_p),
            jnp.stack(hg_s), jnp.stack(rg_s), jnp.stack(cv_s))
```

```python
import functools

import jax
import jax.numpy as jnp
from jax import lax
from jax.experimental import pallas as pl
from jax.experimental.pallas import tpu as pltpu

D_MODEL = 2048
HG_HEADS = 8
HG_DK = 128
HG_DV = 128
HG_F = HG_HEADS * HG_DK
HG_I = HG_HEADS * HG_DV
RG_WIDTH = 1024
RG_BLOCKS = 8
RG_BW = RG_WIDTH // RG_BLOCKS
CONV_W = 4
RG_C = 8.0
IN_COLS = 2 * HG_F + 2 * HG_I + 2 * RG_WIDTH + 2 * D_MODEL
N_EXPERTS = 8
EPS = 1e-6

LANES = 128
BF16_SUBLANES = 16
VMEM_LIMIT = 56 * 1024 * 1024

BF16 = jnp.bfloat16
F32 = jnp.float32


def _params(sem, vmem=VMEM_LIMIT):
    return pltpu.CompilerParams(dimension_semantics=sem, vmem_limit_bytes=vmem)


def _row_tile(m, target):
    best = None
    for t in range(BF16_SUBLANES, min(m, target) + 1, BF16_SUBLANES):
        if m % t == 0:
            best = t
    assert best is not None, (m, target)
    return best


def _dot(a, b):
    return jnp.dot(a, b, preferred_element_type=F32)


def _dot_nt(a, b):
    return lax.dot_general(a, b, (((1,), (1,)), ((), ())), preferred_element_type=F32)


def _dot_tn(a, b):
    return lax.dot_general(a, b, (((0,), (0,)), ((), ())), preferred_element_type=F32)


def _sigmoid(x):
    return 1.0 / (1.0 + jnp.exp(-x))


def _log_sigmoid(x):
    return jnp.minimum(x, 0.0) - jnp.log1p(jnp.exp(-jnp.abs(x)))


def _rms(x, g):
    ms = jnp.mean(x * x, axis=-1, keepdims=True)
    return x * lax.rsqrt(ms + EPS) * g


def _rmsnorm_kernel(x_ref, g_ref, o_ref):
    o_ref[...] = _rms(x_ref[...], g_ref[...]).astype(o_ref.dtype)


def _rmsnorm(x, g, out_dtype):
    m, d = x.shape
    tm = _row_tile(m, 640)
    return pl.pallas_call(
        _rmsnorm_kernel,
        grid=(m // tm,),
        in_specs=[pl.BlockSpec((tm, d), lambda i: (i, 0)),
                  pl.BlockSpec((1, d), lambda i: (0, 0))],
        out_specs=pl.BlockSpec((tm, d), lambda i: (i, 0)),
        out_shape=jax.ShapeDtypeStruct((m, d), out_dtype),
        compiler_params=_params(("parallel",)),
        name="rmsnorm",
    )(x, g.reshape(1, d))


def _in_proj_kernel(x_ref, w_ref, o_ref, wbf_ref):
    @pl.when(pl.program_id(1) == 0)
    def _():
        wbf_ref[...] = w_ref[...].astype(BF16)

    o_ref[...] = _dot(x_ref[...], wbf_ref[...])


def _in_proj(h, w_in, layer):
    m, d = h.shape
    n = w_in.shape[-1]
    tm = _row_tile(m, 1664)
    tn = 512
    return pl.pallas_call(
        _in_proj_kernel,
        grid=(n // tn, m // tm),
        in_specs=[pl.BlockSpec((tm, d), lambda j, i: (i, 0)),
                  pl.BlockSpec((None, d, tn), lambda j, i: (layer, 0, j))],
        out_specs=pl.BlockSpec((tm, tn), lambda j, i: (i, j)),
        out_shape=jax.ShapeDtypeStruct((m, n), F32),
        scratch_shapes=[pltpu.VMEM((d, tn), BF16)],
        compiler_params=_params(("arbitrary", "arbitrary")),
        name="in_proj",
    )(h, w_in)


def _hgrn_gates(fz, log_lb, log1m_lb):
    t = jnp.log1p(jnp.exp(-jnp.abs(fz)))
    c = log1m_lb + (jnp.minimum(fz, 0.0) - t)
    log_f = jnp.maximum(log_lb, c) + jnp.log1p(jnp.exp(-jnp.abs(log_lb - c)))
    one_minus_f = jnp.exp(log1m_lb + (jnp.minimum(-fz, 0.0) - t))
    return log_f, one_minus_f


def _hgrn_out(o, gate, gnorm):
    return _rms(o, gnorm) * (gate * _sigmoid(gate))


def _hgrn_prompt_kernel(q_ref, f_ref, i_ref, g_ref, llb_ref, l1lb_ref, gn_ref,
                        o_ref, s_out_ref, s_ref, *, chunk):
    C = chunk
    c = pl.program_id(2)

    @pl.when(c == 0)
    def _():
        s_ref[...] = jnp.zeros_like(s_ref)

    q = q_ref[...]
    v = i_ref[...].astype(BF16)
    g, k = _hgrn_gates(f_ref[...], llb_ref[...], l1lb_ref[...])

    row = lax.broadcasted_iota(jnp.int32, (C, HG_DK), 0)
    rc_xor = (lax.broadcasted_iota(jnp.int32, (C, C), 0)
              ^ lax.broadcasted_iota(jnp.int32, (C, C), 1))

    a_mat = jnp.where(rc_xor == 0, _dot_nt(q.astype(BF16), k.astype(BF16)), 0.0)

    pre, suf, tot = g, jnp.zeros_like(g), g
    w = 1
    while w < C:
        upper = (row & w) != 0
        e = jnp.exp(jnp.where(upper, pre, suf))
        qw = jnp.where(upper, q * e, 0.0).astype(BF16)
        kw = jnp.where(upper, 0.0, k * e).astype(BF16)
        a_mat = a_mat + jnp.where(rc_xor < 2 * w, _dot_nt(qw, kw), 0.0)
        up = pltpu.roll(tot, w, 0)
        dn = pltpu.roll(tot, C - w, 0)
        pre = pre + jnp.where(upper, up, 0.0)
        suf = suf + jnp.where(upper, 0.0, dn)
        tot = tot + jnp.where(upper, up, dn)
        w *= 2

    s_old = s_ref[...]
    o = _dot(a_mat.astype(BF16), v) + _dot((q * jnp.exp(pre)).astype(BF16), s_old.astype(BF16))
    decay_rows = jnp.exp(jnp.broadcast_to(tot[0:1, :], (HG_DK, HG_DK))).T
    s_new = decay_rows * s_old + _dot_tn((k * jnp.exp(suf)).astype(BF16), v)
    s_ref[...] = s_new

    o_ref[...] = _hgrn_out(o, g_ref[...], gn_ref[...]).astype(o_ref.dtype)

    @pl.when(c == pl.num_programs(2) - 1)
    def _():
        s_out_ref[...] = s_new


def _hgrn_prompt(proj, log_lb, log1m_lb, gnorm, batch, seq, m_total, chunk):
    n_chunks = seq // chunk
    tok = lambda off: pl.BlockSpec((chunk, LANES), lambda b, h, c: (b * n_chunks + c, off + h))
    vec = pl.BlockSpec((1, LANES), lambda b, h, c: (0, h))
    return pl.pallas_call(
        functools.partial(_hgrn_prompt_kernel, chunk=chunk),
        grid=(batch, HG_HEADS, n_chunks),
        in_specs=[tok(0), tok(HG_HEADS), tok(2 * HG_HEADS), tok(3 * HG_HEADS), vec, vec, vec],
        out_specs=[pl.BlockSpec((chunk, LANES), lambda b, h, c: (b * n_chunks + c, h)),
                   pl.BlockSpec((None, None, HG_DK, HG_DV), lambda b, h, c: (b, h, 0, 0))],
        out_shape=[jax.ShapeDtypeStruct((m_total, HG_I), BF16),
                   jax.ShapeDtypeStruct((batch, HG_HEADS, HG_DK, HG_DV), F32)],
        scratch_shapes=[pltpu.VMEM((HG_DK, HG_DV), F32)],
        compiler_params=_params(("parallel", "parallel", "arbitrary")),
        name="hgrn_prompt",
    )(proj, proj, proj, proj, log_lb, log1m_lb, gnorm)


HGRN_SAMPLE_ROWS = 16


def _hgrn_sample_kernel(q_ref, f_ref, i_ref, g_ref, llb_ref, l1lb_ref, gn_ref, s_ref, o_prev_ref,
                        o_ref, s_out_ref):
    del o_prev_ref
    n = HGRN_SAMPLE_ROWS
    q = q_ref[...]
    vi = i_ref[...]
    g, k = _hgrn_gates(f_ref[...], llb_ref[...], l1lb_ref[...])
    f = jnp.exp(g)

    def columns(x):
        pad = jnp.zeros((LANES - n, HG_DK), F32)
        return jnp.concatenate([x, pad], axis=0).T

    f_t, k_t, q_t = columns(f), columns(k), columns(q)
    seq_id = lax.broadcasted_iota(jnp.int32, (n, HG_DV), 0)
    o = jnp.zeros((n, HG_DV), F32)
    for j in range(n):
        bc = lambda xt: jnp.broadcast_to(xt[:, j:j + 1], (HG_DK, HG_DV))
        s_new = bc(f_t) * s_ref[j] + bc(k_t) * vi[j:j + 1, :]
        s_out_ref[j] = s_new
        o_j = jnp.sum(bc(q_t) * s_new, axis=0, keepdims=True)
        o = jnp.where(seq_id == j, o_j, o)
    o_ref[...] = _hgrn_out(o, g_ref[...], gn_ref[...]).astype(o_ref.dtype)


def _hgrn_sample(proj, log_lb, log1m_lb, gnorm, state, o_all, row0):
    n_seq = state.shape[0]
    n = HGRN_SAMPLE_ROWS
    assert n_seq % n == 0 and row0 % n == 0
    blk0 = row0 // n
    tok = lambda off: pl.BlockSpec((n, LANES), lambda j, h: (blk0 + j, off + h))
    vec = pl.BlockSpec((1, LANES), lambda j, h: (0, h))
    st = pl.BlockSpec((n, None, HG_DK, HG_DV), lambda j, h: (j, h, 0, 0))
    return pl.pallas_call(
        _hgrn_sample_kernel,
        grid=(n_seq // n, HG_HEADS),
        in_specs=[tok(0), tok(HG_HEADS), tok(2 * HG_HEADS), tok(3 * HG_HEADS), vec, vec, vec, st,
                  pl.BlockSpec(memory_space=pl.ANY)],
        out_specs=[pl.BlockSpec((n, LANES), lambda j, h: (blk0 + j, h)), st],
        out_shape=[jax.ShapeDtypeStruct(o_all.shape, o_all.dtype),
                   jax.ShapeDtypeStruct(state.shape, F32)],
        input_output_aliases={8: 0},
        compiler_params=_params(("parallel", "parallel")),
        name="hgrn_sample",
    )(proj, proj, proj, proj, log_lb, log1m_lb, gnorm, state, o_all)


def _gelu_tanh(x):
    return 0.5 * x * (1.0 + jnp.tanh(0.7978845608028654 * (x + 0.044715 * (x * x * x))))


def _rg_gates(conv, wa_ref, ba_ref, wx_ref, bx_ref, ap_ref):
    cb = conv.astype(BF16)
    r = _sigmoid(_dot(cb, wa_ref[...].astype(BF16)) + ba_ref[...])
    ig = _sigmoid(_dot(cb, wx_ref[...].astype(BF16)) + bx_ref[...])
    log_a = RG_C * r * _log_sigmoid(ap_ref[...])
    a = jnp.exp(log_a)
    t = jnp.tanh(log_a)
    b = jnp.sqrt(-2.0 * t / (1.0 - t)) * ig * conv
    return a, b


def _rg_prompt_kernel(x_ref, gate_ref, cw_ref, cb_ref, wa_ref, ba_ref, wx_ref, bx_ref, ap_ref,
                      y_ref, h_out_ref, xp_ref, h_ref, *, rows):
    T = rows
    t = pl.program_id(2)

    @pl.when(t == 0)
    def _():
        xp_ref[0:8, :] = jnp.zeros((8, RG_BW), F32)
        h_ref[...] = jnp.zeros_like(h_ref)

    xp_ref[8:8 + T, :] = x_ref[...]
    conv = cb_ref[...]
    for j in range(CONV_W):
        conv = conv + xp_ref[5 + j:5 + j + T, :] * cw_ref[j:j + 1, :]
    xp_ref[0:8, :] = xp_ref[T:T + 8, :]

    a, b = _rg_gates(conv, wa_ref, ba_ref, wx_ref, bx_ref, ap_ref)

    row = lax.broadcasted_iota(jnp.int32, (T, RG_BW), 0)
    d = 1
    while d < T:
        ok = row >= d
        b = jnp.where(ok, a * pltpu.roll(b, d, 0) + b, b)
        a = jnp.where(ok, a * pltpu.roll(a, d, 0), a)
        d *= 2
    h = a * h_ref[0:1, :] + b
    h_last = h[T - 1:T, :]
    h_ref[...] = jnp.broadcast_to(h_last, h_ref.shape)

    y_ref[...] = (h * _gelu_tanh(gate_ref[...])).astype(y_ref.dtype)

    @pl.when(t == pl.num_programs(2) - 1)
    def _():
        h_out_ref[...] = h_last


def _rg_prompt(proj, cw, cb, wa, ba, wx, bx, ap, batch, seq, m_total, rows):
    n_t = seq // rows
    xoff = (2 * HG_F + 2 * HG_I) // LANES
    goff = xoff + RG_BLOCKS
    tok = lambda off: pl.BlockSpec((rows, LANES), lambda b, n, t: (b * n_t + t, off + n))
    vec = pl.BlockSpec((1, LANES), lambda b, n, t: (0, n))
    blkw = pl.BlockSpec((None, RG_BW, RG_BW), lambda b, n, t: (n, 0, 0))
    blkb = pl.BlockSpec((None, 1, RG_BW), lambda b, n, t: (n, 0, 0))
    return pl.pallas_call(
        functools.partial(_rg_prompt_kernel, rows=rows),
        grid=(batch, RG_BLOCKS, n_t),
        in_specs=[tok(xoff), tok(goff),
                  pl.BlockSpec((CONV_W, LANES), lambda b, n, t: (0, n)), vec,
                  blkw, blkb, blkw, blkb, vec],
        out_specs=[pl.BlockSpec((rows, LANES), lambda b, n, t: (b * n_t + t, n)),
                   pl.BlockSpec((None, 1, LANES), lambda b, n, t: (b, 0, n))],
        out_shape=[jax.ShapeDtypeStruct((m_total, RG_WIDTH), BF16),
                   jax.ShapeDtypeStruct((batch, 1, RG_WIDTH), F32)],
        scratch_shapes=[pltpu.VMEM((rows + 8, RG_BW), F32), pltpu.VMEM((8, RG_BW), F32)],
        compiler_params=_params(("parallel", "parallel", "arbitrary")),
        name="rg_prompt",
    )(proj, proj, cw, cb, wa, ba, wx, bx, ap)


def _rg_sample_kernel(x_ref, gate_ref, b0_ref, b1_ref, b2_ref, h0_ref, cw_ref, cb_ref,
                      wa_ref, ba_ref, wx_ref, bx_ref, ap_ref, y_prev_ref, y_ref, h_out_ref):
    del y_prev_ref
    conv = cb_ref[...]
    for j, r in enumerate((b0_ref, b1_ref, b2_ref, x_ref)):
        conv = conv + r[...] * cw_ref[j:j + 1, :]
    a, b = _rg_gates(conv, wa_ref, ba_ref, wx_ref, bx_ref, ap_ref)
    h = a * h0_ref[...] + b
    h_out_ref[...] = h
    y_ref[...] = (h * _gelu_tanh(gate_ref[...])).astype(y_ref.dtype)


def _rg_sample(proj, conv_state, h0, cw, cb, wa, ba, wx, bx, ap, y_all, row0):
    n_seq = h0.shape[0]
    assert row0 % n_seq == 0 and n_seq % BF16_SUBLANES == 0
    blk0 = row0 // n_seq
    xoff = (2 * HG_F + 2 * HG_I) // LANES
    goff = xoff + RG_BLOCKS
    tok = lambda off: pl.BlockSpec((n_seq, LANES), lambda n: (blk0, off + n))
    buf = lambda j: pl.BlockSpec((n_seq, LANES), lambda n: (0, j * RG_BLOCKS + n))
    vec = pl.BlockSpec((1, LANES), lambda n: (0, n))
    blkw = pl.BlockSpec((None, RG_BW, RG_BW), lambda n: (n, 0, 0))
    blkb = pl.BlockSpec((None, 1, RG_BW), lambda n: (n, 0, 0))
    flat_state = conv_state.reshape(n_seq, (CONV_W - 1) * RG_WIDTH)
    return pl.pallas_call(
        _rg_sample_kernel,
        grid=(RG_BLOCKS,),
        in_specs=[tok(xoff), tok(goff), buf(0), buf(1), buf(2),
                  pl.BlockSpec((n_seq, LANES), lambda n: (0, n)),
                  pl.BlockSpec((CONV_W, LANES), lambda n: (0, n)), vec,
                  blkw, blkb, blkw, blkb, vec,
                  pl.BlockSpec(memory_space=pl.ANY)],
        out_specs=[pl.BlockSpec((n_seq, LANES), lambda n: (blk0, n)),
                   pl.BlockSpec((n_seq, LANES), lambda n: (0, n))],
        out_shape=[jax.ShapeDtypeStruct(y_all.shape, y_all.dtype),
                   jax.ShapeDtypeStruct((n_seq, RG_WIDTH), F32)],
        input_output_aliases={13: 0},
        compiler_params=_params(("parallel",)),
        name="rg_sample",
    )(proj, proj, flat_state, flat_state, flat_state, h0, cw, cb, wa, ba, wx, bx, ap, y_all)


def _merge_out_kernel(o_ref, y_ref, ma_ref, mb_ref, x_ref, wa_ref, wb_ref, wo_ref, g_ref,
                      xo_ref, h_ref):
    merged = (_sigmoid(ma_ref[...]) * _dot(o_ref[...], wa_ref[...])
              + _sigmoid(mb_ref[...]) * _dot(y_ref[...], wb_ref[...]))
    x_new = x_ref[...] + _dot(merged.astype(BF16), wo_ref[...])
    xo_ref[...] = x_new
    h_ref[...] = _rms(x_new, g_ref[...]).astype(h_ref.dtype)


def _merge_out(o, y, proj, x, w_a, w_b, w_o, g):
    m, d = x.shape
    tm = _row_tile(m, 320)
    moff = (2 * HG_F + 2 * HG_I + 2 * RG_WIDTH) // d
    row = lambda width, col: pl.BlockSpec((tm, width), lambda i: (i, col))
    res = lambda shape: pl.BlockSpec(shape, lambda i: (0, 0), pipeline_mode=pl.Buffered(1))
    return pl.pallas_call(
        _merge_out_kernel,
        grid=(m // tm,),
        in_specs=[row(HG_I, 0), row(RG_WIDTH, 0), row(d, moff), row(d, moff + 1), row(d, 0),
                  res(w_a.shape), res(w_b.shape), res(w_o.shape), res((1, d))],
        out_specs=[row(d, 0), row(d, 0)],
        out_shape=[jax.ShapeDtypeStruct((m, d), F32), jax.ShapeDtypeStruct((m, d), BF16)],
        compiler_params=_params(("parallel",)),
        name="merge_out",
    )(o, y, proj, proj, x, w_a, w_b, w_o, g.reshape(1, d))


def _router_kernel(h_ref, w_ref, gates_ref):
    logits = _dot(h_ref[...], w_ref[...])
    lane = lax.broadcasted_iota(jnp.int32, logits.shape, 1)
    neg = jnp.float32(-jnp.inf)
    logits = jnp.where(lane < N_EXPERTS, logits, neg)
    m1 = jnp.max(logits, axis=-1, keepdims=True)
    i1 = jnp.min(jnp.where(logits == m1, lane, LANES), axis=-1, keepdims=True)
    rest = jnp.where(lane == i1, neg, logits)
    m2 = jnp.max(rest, axis=-1, keepdims=True)
    i2 = jnp.min(jnp.where(rest == m2, lane, LANES), axis=-1, keepdims=True)
    e = jnp.exp(m2 - m1)
    g1 = 1.0 / (1.0 + e)
    g2 = e / (1.0 + e)
    gates_ref[...] = jnp.where(lane == i1, g1, 0.0) + jnp.where(lane == i2, g2, 0.0)


def _router(h, w_router):
    m, d = h.shape
    tm = _row_tile(m, 640)
    w = jnp.pad(w_router, ((0, 0), (0, LANES - N_EXPERTS))).astype(BF16)
    return pl.pallas_call(
        _router_kernel,
        grid=(m // tm,),
        in_specs=[pl.BlockSpec((tm, d), lambda i: (i, 0)),
                  pl.BlockSpec((d, LANES), lambda i: (0, 0))],
        out_specs=pl.BlockSpec((tm, LANES), lambda i: (i, 0)),
        out_shape=jax.ShapeDtypeStruct((m, LANES), F32),
        compiler_params=_params(("parallel",)),
        name="router",
    )(h, w)


def _ffn_kernel(h_ref, x_ref, wg_ref, wu_ref, wd_ref, gates_ref, g_ref, o_ref, acc_ref, exp_ref,
                *, gated):
    e = pl.program_id(1)
    f = pl.program_id(2)
    last_f = f == pl.num_programs(2) - 1

    @pl.when((e == 0) & (f == 0))
    def _():
        acc_ref[...] = x_ref[...]

    @pl.when(f == 0)
    def _():
        exp_ref[...] = jnp.zeros_like(exp_ref)

    h = h_ref[...]
    gate = _dot(h, wg_ref[...])
    act = (gate * _sigmoid(gate)) * _dot(h, wu_ref[...])
    exp_ref[...] += _dot(act.astype(BF16), wd_ref[...])

    @pl.when(last_f)
    def _():
        if gated:
            lane = lax.broadcasted_iota(jnp.int32, gates_ref.shape, 1)
            ge = jnp.sum(jnp.where(lane == e, gates_ref[...], 0.0), axis=-1, keepdims=True)
            acc_ref[...] += ge * exp_ref[...]
        else:
            acc_ref[...] += exp_ref[...]

    @pl.when(last_f & (e == pl.num_programs(1) - 1))
    def _():
        o_ref[...] = _rms(acc_ref[...], g_ref[...]).astype(o_ref.dtype)


def _ffn(h, x, wg, wu, wd, gates, g, out_dtype, tf):
    m, d = x.shape
    n_exp, _, ff = wg.shape
    tm = _row_tile(m, 416)
    gated = gates is not None
    if gates is None:
        gates = jnp.zeros((m, LANES), F32)
    return pl.pallas_call(
        functools.partial(_ffn_kernel_xout, gated=gated),
        grid=(m // tm, n_exp, ff // tf),
        in_specs=[pl.BlockSpec((tm, d), lambda i, e, f: (i, 0)),
                  pl.BlockSpec((tm, d), lambda i, e, f: (i, 0)),
                  pl.BlockSpec((None, d, tf), lambda i, e, f: (e, 0, f)),
                  pl.BlockSpec((None, d, tf), lambda i, e, f: (e, 0, f)),
                  pl.BlockSpec((None, tf, d), lambda i, e, f: (e, f, 0)),
                  pl.BlockSpec((tm, LANES), lambda i, e, f: (i, 0)),
                  pl.BlockSpec((1, d), lambda i, e, f: (0, 0))],
        out_specs=[pl.BlockSpec((tm, d), lambda i, e, f: (i, 0)),
                   pl.BlockSpec((tm, d), lambda i, e, f: (i, 0))],
        out_shape=[jax.ShapeDtypeStruct((m, d), out_dtype), jax.ShapeDtypeStruct((m, d), F32)],
        scratch_shapes=[pltpu.VMEM((tm, d), F32), pltpu.VMEM((tm, d), F32)],
        compiler_params=_params(("parallel", "arbitrary", "arbitrary")),
        name="ffn_moe" if gated else "ffn_dense",
    )(h, x, wg, wu, wd, gates, g.reshape(1, d))


def _ffn_kernel_xout(h_ref, x_ref, wg_ref, wu_ref, wd_ref, gates_ref, g_ref, o_ref, xo_ref,
                     acc_ref, exp_ref, *, gated):
    _ffn_kernel(h_ref, x_ref, wg_ref, wu_ref, wd_ref, gates_ref, g_ref, o_ref, acc_ref, exp_ref,
                gated=gated)

    @pl.when((pl.program_id(2) == pl.num_programs(2) - 1)
             & (pl.program_id(1) == pl.num_programs(1) - 1))
    def _():
        xo_ref[...] = acc_ref[...]


def _pad_ff(w, axis, mult):
    ff = w.shape[axis]
    pad = (-ff) % mult
    if pad:
        widths = [(0, 0)] * w.ndim
        widths[axis] = (0, pad)
        w = jnp.pad(w, widths)
    return w


def kernel(x_prompt, x_sample, state_hgrn, state_rglru, state_conv, hgrn_lb_logits, norm_mix, w_in,
           hgrn_gnorm, rg_conv_w, rg_conv_b, rg_wa, rg_ba, rg_wx, rg_bx, rg_a_param, w_br_a, w_br_b,
           w_out, norm_ffn, ffn_w_gate, ffn_w_up, ffn_w_down, moe_router, moe_w_gate, moe_w_up,
           moe_w_down, norm_final):
    depth = w_in.shape[0]
    batch, seq, d = x_prompt.shape
    n_seq = x_sample.shape[0]
    m_prompt = batch * seq
    m = m_prompt + n_seq

    sm = jax.nn.softmax(hgrn_lb_logits.astype(F32), axis=0)
    lbs = jnp.maximum(jnp.cumsum(sm, axis=0) - sm[0:1], 0.0)
    log_lb = jnp.log(lbs)
    log1m_lb = jnp.log1p(-lbs)

    x = jnp.concatenate([x_prompt.reshape(m_prompt, d), x_sample.reshape(n_seq, d)], axis=0)
    h = _rmsnorm(x, norm_mix[0], BF16)

    chunk = 256 if seq % 256 == 0 else seq
    hg_p, rg_p, cv_p, hg_s, rg_s, cv_s = [], [], [], [], [], []
    y_out = None
    for l in range(depth):
        proj = _in_proj(h, w_in, l)
        vrow = lambda a: a[l].reshape(1, -1)
        o_all, s_prompt = _hgrn_prompt(proj, vrow(log_lb), vrow(log1m_lb), vrow(hgrn_gnorm),
                                       batch, seq, m, chunk)
        o_all, s_sample = _hgrn_sample(proj, vrow(log_lb), vrow(log1m_lb), vrow(hgrn_gnorm),
                                       state_hgrn[l], o_all, m_prompt)
        rg_args = (rg_conv_w[l], vrow(rg_conv_b), rg_wa[l], rg_ba[l].reshape(RG_BLOCKS, 1, RG_BW),
                   rg_wx[l], rg_bx[l].reshape(RG_BLOCKS, 1, RG_BW), vrow(rg_a_param))
        y_all, h_prompt = _rg_prompt(proj, *rg_args, batch, seq, m, chunk)
        y_all, h_sample = _rg_sample(proj, state_conv[l], state_rglru[l], *rg_args, y_all, m_prompt)

        x, h2 = _merge_out(o_all, y_all, proj, x, w_br_a[l].astype(BF16), w_br_b[l].astype(BF16),
                           w_out[l].astype(BF16), norm_ffn[l])

        last = l == depth - 1
        g_next = norm_final if last else norm_mix[l + 1]
        out_dtype = F32 if last else BF16
        j = l // 2
        if l % 2 == 0:
            wg = _pad_ff(ffn_w_gate[j], 1, 512).astype(BF16)[None]
            wu = _pad_ff(ffn_w_up[j], 1, 512).astype(BF16)[None]
            wd = _pad_ff(ffn_w_down[j], 0, 512).astype(BF16)[None]
            nxt, x = _ffn(h2, x, wg, wu, wd, None, g_next, out_dtype, 512)
        else:
            gates = _router(h2, moe_router[j])
            nxt, x = _ffn(h2, x, moe_w_gate[j].astype(BF16), moe_w_up[j].astype(BF16),
                          moe_w_down[j].astype(BF16), gates, g_next, out_dtype, 512)
        if last:
            y_out = nxt
        else:
            h = nxt

        xr = proj[:, 2 * HG_F + 2 * HG_I:2 * HG_F + 2 * HG_I + RG_WIDTH]
        hg_p.append(s_prompt)
        rg_p.append(h_prompt.reshape(batch, RG_WIDTH))
        cv_p.append(xr[:m_prompt].reshape(batch, seq, RG_WIDTH)[:, seq - (CONV_W - 1):])
        hg_s.append(s_sample)
        rg_s.append(h_sample)
        cv_s.append(jnp.concatenate([state_conv[l][:, 1:], xr[m_prompt:, None, :]], axis=1))

    y_prompt = y_out[:m_prompt].reshape(batch, seq, d)
    y_sample = y_out[m_prompt:].reshape(n_seq, 1, d)
    return (y_prompt, y_sample, jnp.stack(hg_p), jnp.stack(rg_p), jnp.stack(cv_p),
            jnp.stack(hg_s), jnp.stack(rg_s), jnp.stack(cv_s))
```

```python
import functools

import jax
import jax.numpy as jnp
from jax import lax
from jax.experimental import pallas as pl
from jax.experimental.pallas import tpu as pltpu

D_MODEL = 2048
HG_HEADS = 8
HG_DK = 128
HG_DV = 128
HG_F = HG_HEADS * HG_DK
HG_I = HG_HEADS * HG_DV
RG_WIDTH = 1024
RG_BLOCKS = 8
RG_BW = RG_WIDTH // RG_BLOCKS
CONV_W = 4
RG_C = 8.0
IN_COLS = 2 * HG_F + 2 * HG_I + 2 * RG_WIDTH + 2 * D_MODEL
N_EXPERTS = 8
TOP_K = 2
EPS = 1e-6

LANES = 128
SUBLANES = 8
BF16_SUBLANES = 16
VMEM_LIMIT = 56 * 1024 * 1024
EXPERT_ROWS = 512

BF16 = jnp.bfloat16
F32 = jnp.float32
I32 = jnp.int32


def _params(sem, vmem=VMEM_LIMIT):
    return pltpu.CompilerParams(dimension_semantics=sem, vmem_limit_bytes=vmem)


def _row_tile(m, target):
    best = None
    for t in range(BF16_SUBLANES, min(m, target) + 1, BF16_SUBLANES):
        if m % t == 0:
            best = t
    assert best is not None, (m, target)
    return best


def _dot(a, b):
    return jnp.dot(a, b, preferred_element_type=F32)


def _dot_nt(a, b):
    return lax.dot_general(a, b, (((1,), (1,)), ((), ())), preferred_element_type=F32)


def _dot_tn(a, b):
    return lax.dot_general(a, b, (((0,), (0,)), ((), ())), preferred_element_type=F32)


def _sigmoid(x):
    return 1.0 / (1.0 + jnp.exp(-x))


def _log_sigmoid(x):
    return jnp.minimum(x, 0.0) - jnp.log1p(jnp.exp(-jnp.abs(x)))


def _rms(x, g):
    ms = jnp.mean(x * x, axis=-1, keepdims=True)
    return x * lax.rsqrt(ms + EPS) * g


def _rmsnorm_kernel(x_ref, g_ref, o_ref):
    o_ref[...] = _rms(x_ref[...], g_ref[...]).astype(o_ref.dtype)


def _rmsnorm(x, g, out_dtype):
    m, d = x.shape
    tm = _row_tile(m, 640)
    return pl.pallas_call(
        _rmsnorm_kernel,
        grid=(m // tm,),
        in_specs=[pl.BlockSpec((tm, d), lambda i: (i, 0)),
                  pl.BlockSpec((1, d), lambda i: (0, 0))],
        out_specs=pl.BlockSpec((tm, d), lambda i: (i, 0)),
        out_shape=jax.ShapeDtypeStruct((m, d), out_dtype),
        compiler_params=_params(("parallel",)),
        name="rmsnorm",
    )(x, g.reshape(1, d))


def _in_proj_kernel(x_ref, w_ref, o_ref, wbf_ref):
    @pl.when(pl.program_id(1) == 0)
    def _():
        wbf_ref[...] = w_ref[...].astype(BF16)

    o_ref[...] = _dot(x_ref[...], wbf_ref[...])


def _in_proj(h, w_in, layer):
    m, d = h.shape
    n = w_in.shape[-1]
    tm = _row_tile(m, 1664)
    tn = 512
    return pl.pallas_call(
        _in_proj_kernel,
        grid=(n // tn, m // tm),
        in_specs=[pl.BlockSpec((tm, d), lambda j, i: (i, 0)),
                  pl.BlockSpec((None, d, tn), lambda j, i: (layer, 0, j))],
        out_specs=pl.BlockSpec((tm, tn), lambda j, i: (i, j)),
        out_shape=jax.ShapeDtypeStruct((m, n), F32),
        scratch_shapes=[pltpu.VMEM((d, tn), BF16)],
        compiler_params=_params(("arbitrary", "arbitrary")),
        name="in_proj",
    )(h, w_in)


def _hgrn_gates(fz, log_lb, log1m_lb):
    t = jnp.log1p(jnp.exp(-jnp.abs(fz)))
    c = log1m_lb + (jnp.minimum(fz, 0.0) - t)
    log_f = jnp.maximum(log_lb, c) + jnp.log1p(jnp.exp(-jnp.abs(log_lb - c)))
    one_minus_f = jnp.exp(log1m_lb + (jnp.minimum(-fz, 0.0) - t))
    return log_f, one_minus_f


def _hgrn_out(o, gate, gnorm):
    return _rms(o, gnorm) * (gate * _sigmoid(gate))


def _hgrn_prompt_kernel(q_ref, f_ref, i_ref, g_ref, llb_ref, l1lb_ref, gn_ref, o_init_ref,
                        o_ref, s_out_ref, s_ref, *, chunk):
    del o_init_ref
    C = chunk
    c = pl.program_id(2)

    @pl.when(c == 0)
    def _():
        s_ref[...] = jnp.zeros_like(s_ref)

    q = q_ref[...]
    v = i_ref[...].astype(BF16)
    g, k = _hgrn_gates(f_ref[...], llb_ref[...], l1lb_ref[...])

    row = lax.broadcasted_iota(I32, (C, HG_DK), 0)
    rc_xor = lax.broadcasted_iota(I32, (C, C), 0) ^ lax.broadcasted_iota(I32, (C, C), 1)

    a_mat = jnp.where(rc_xor == 0, _dot_nt(q.astype(BF16), k.astype(BF16)), 0.0)

    pre, suf, tot = g, jnp.zeros_like(g), g
    w = 1
    while w < C:
        upper = (row & w) != 0
        e = jnp.exp(jnp.where(upper, pre, suf))
        qw = jnp.where(upper, q * e, 0.0).astype(BF16)
        kw = jnp.where(upper, 0.0, k * e).astype(BF16)
        a_mat = a_mat + jnp.where(rc_xor < 2 * w, _dot_nt(qw, kw), 0.0)
        up = pltpu.roll(tot, w, 0)
        dn = pltpu.roll(tot, C - w, 0)
        pre = pre + jnp.where(upper, up, 0.0)
        suf = suf + jnp.where(upper, 0.0, dn)
        tot = tot + jnp.where(upper, up, dn)
        w *= 2

    s_old = s_ref[...]
    o = _dot(a_mat.astype(BF16), v) + _dot((q * jnp.exp(pre)).astype(BF16), s_old.astype(BF16))
    decay_rows = jnp.exp(jnp.broadcast_to(tot[0:1, :], (HG_DK, HG_DK))).T
    s_new = decay_rows * s_old + _dot_tn((k * jnp.exp(suf)).astype(BF16), v)
    s_ref[...] = s_new

    o_ref[...] = _hgrn_out(o, g_ref[...], gn_ref[...]).astype(o_ref.dtype)

    @pl.when(c == pl.num_programs(2) - 1)
    def _():
        s_out_ref[...] = s_new


def _hgrn_prompt(proj, log_lb, log1m_lb, gnorm, batch, seq, m_total, chunk):
    n_chunks = seq // chunk
    tok = lambda off: pl.BlockSpec((chunk, LANES), lambda b, h, c: (b * n_chunks + c, off + h))
    vec = pl.BlockSpec((1, LANES), lambda b, h, c: (0, h))
    return pl.pallas_call(
        functools.partial(_hgrn_prompt_kernel, chunk=chunk),
        grid=(batch, HG_HEADS, n_chunks),
        in_specs=[tok(0), tok(HG_HEADS), tok(2 * HG_HEADS), tok(3 * HG_HEADS), vec, vec, vec,
                  pl.BlockSpec(memory_space=pl.ANY)],
        input_output_aliases={7: 0},
        out_specs=[pl.BlockSpec((chunk, LANES), lambda b, h, c: (b * n_chunks + c, h)),
                   pl.BlockSpec((None, None, HG_DK, HG_DV), lambda b, h, c: (b, h, 0, 0))],
        out_shape=[jax.ShapeDtypeStruct((m_total, HG_I), BF16),
                   jax.ShapeDtypeStruct((batch, HG_HEADS, HG_DK, HG_DV), F32)],
        scratch_shapes=[pltpu.VMEM((HG_DK, HG_DV), F32)],
        compiler_params=_params(("parallel", "parallel", "arbitrary")),
        name="hgrn_prompt",
    )(proj, proj, proj, proj, log_lb, log1m_lb, gnorm, jnp.zeros((m_total, HG_I), BF16))


HGRN_SAMPLE_ROWS = 16


def _hgrn_sample_kernel(q_ref, f_ref, i_ref, g_ref, llb_ref, l1lb_ref, gn_ref, s_ref, o_prev_ref,
                        o_ref, s_out_ref):
    del o_prev_ref
    n = HGRN_SAMPLE_ROWS
    q = q_ref[...]
    vi = i_ref[...]
    g, k = _hgrn_gates(f_ref[...], llb_ref[...], l1lb_ref[...])
    f = jnp.exp(g)

    def columns(x):
        pad = jnp.zeros((LANES - n, HG_DK), F32)
        return jnp.concatenate([x, pad], axis=0).T

    f_t, k_t, q_t = columns(f), columns(k), columns(q)
    seq_id = lax.broadcasted_iota(I32, (n, HG_DV), 0)
    o = jnp.zeros((n, HG_DV), F32)
    for j in range(n):
        bc = lambda xt: jnp.broadcast_to(xt[:, j:j + 1], (HG_DK, HG_DV))
        s_new = bc(f_t) * s_ref[j] + bc(k_t) * vi[j:j + 1, :]
        s_out_ref[j] = s_new
        o_j = jnp.sum(bc(q_t) * s_new, axis=0, keepdims=True)
        o = jnp.where(seq_id == j, o_j, o)
    o_ref[...] = _hgrn_out(o, g_ref[...], gn_ref[...]).astype(o_ref.dtype)


def _hgrn_sample(proj, log_lb, log1m_lb, gnorm, state, o_all, row0):
    n_seq = state.shape[0]
    n = HGRN_SAMPLE_ROWS
    assert n_seq % n == 0 and row0 % n == 0
    blk0 = row0 // n
    tok = lambda off: pl.BlockSpec((n, LANES), lambda j, h: (blk0 + j, off + h))
    vec = pl.BlockSpec((1, LANES), lambda j, h: (0, h))
    st = pl.BlockSpec((n, None, HG_DK, HG_DV), lambda j, h: (j, h, 0, 0))
    return pl.pallas_call(
        _hgrn_sample_kernel,
        grid=(n_seq // n, HG_HEADS),
        in_specs=[tok(0), tok(HG_HEADS), tok(2 * HG_HEADS), tok(3 * HG_HEADS), vec, vec, vec, st,
                  pl.BlockSpec(memory_space=pl.ANY)],
        out_specs=[pl.BlockSpec((n, LANES), lambda j, h: (blk0 + j, h)), st],
        out_shape=[jax.ShapeDtypeStruct(o_all.shape, o_all.dtype),
                   jax.ShapeDtypeStruct(state.shape, F32)],
        input_output_aliases={8: 0},
        compiler_params=_params(("parallel", "parallel")),
        name="hgrn_sample",
    )(proj, proj, proj, proj, log_lb, log1m_lb, gnorm, state, o_all)


def _gelu_tanh(x):
    return 0.5 * x * (1.0 + jnp.tanh(0.7978845608028654 * (x + 0.044715 * (x * x * x))))


def _rg_gates(conv, wa_ref, ba_ref, wx_ref, bx_ref, ap_ref):
    cb = conv.astype(BF16)
    r = _sigmoid(_dot(cb, wa_ref[...].astype(BF16)) + ba_ref[...])
    ig = _sigmoid(_dot(cb, wx_ref[...].astype(BF16)) + bx_ref[...])
    log_a = RG_C * r * _log_sigmoid(ap_ref[...])
    a = jnp.exp(log_a)
    t = jnp.tanh(log_a)
    b = jnp.sqrt(-2.0 * t / (1.0 - t)) * ig * conv
    return a, b


def _rg_prompt_kernel(x_ref, gate_ref, cw_ref, cb_ref, wa_ref, ba_ref, wx_ref, bx_ref, ap_ref,
                      y_init_ref, y_ref, h_out_ref, xp_ref, h_ref, *, rows):
    del y_init_ref
    T = rows
    t = pl.program_id(2)

    @pl.when(t == 0)
    def _():
        xp_ref[0:8, :] = jnp.zeros((8, RG_BW), F32)
        h_ref[...] = jnp.zeros_like(h_ref)

    xp_ref[8:8 + T, :] = x_ref[...]
    conv = cb_ref[...]
    for j in range(CONV_W):
        conv = conv + xp_ref[5 + j:5 + j + T, :] * cw_ref[j:j + 1, :]
    xp_ref[0:8, :] = xp_ref[T:T + 8, :]

    a, b = _rg_gates(conv, wa_ref, ba_ref, wx_ref, bx_ref, ap_ref)

    row = lax.broadcasted_iota(I32, (T, RG_BW), 0)
    d = 1
    while d < T:
        ok = row >= d
        b = jnp.where(ok, a * pltpu.roll(b, d, 0) + b, b)
        a = jnp.where(ok, a * pltpu.roll(a, d, 0), a)
        d *= 2
    h = a * h_ref[0:1, :] + b
    h_last = h[T - 1:T, :]
    h_ref[...] = jnp.broadcast_to(h_last, h_ref.shape)

    y_ref[...] = (h * _gelu_tanh(gate_ref[...])).astype(y_ref.dtype)

    @pl.when(t == pl.num_programs(2) - 1)
    def _():
        h_out_ref[...] = h_last


def _rg_prompt(proj, cw, cb, wa, ba, wx, bx, ap, batch, seq, m_total, rows):
    n_t = seq // rows
    xoff = (2 * HG_F + 2 * HG_I) // LANES
    goff = xoff + RG_BLOCKS
    tok = lambda off: pl.BlockSpec((rows, LANES), lambda b, n, t: (b * n_t + t, off + n))
    vec = pl.BlockSpec((1, LANES), lambda b, n, t: (0, n))
    blkw = pl.BlockSpec((None, RG_BW, RG_BW), lambda b, n, t: (n, 0, 0))
    blkb = pl.BlockSpec((None, 1, RG_BW), lambda b, n, t: (n, 0, 0))
    return pl.pallas_call(
        functools.partial(_rg_prompt_kernel, rows=rows),
        grid=(batch, RG_BLOCKS, n_t),
        in_specs=[tok(xoff), tok(goff),
                  pl.BlockSpec((CONV_W, LANES), lambda b, n, t: (0, n)), vec,
                  blkw, blkb, blkw, blkb, vec, pl.BlockSpec(memory_space=pl.ANY)],
        input_output_aliases={9: 0},
        out_specs=[pl.BlockSpec((rows, LANES), lambda b, n, t: (b * n_t + t, n)),
                   pl.BlockSpec((None, 1, LANES), lambda b, n, t: (b, 0, n))],
        out_shape=[jax.ShapeDtypeStruct((m_total, RG_WIDTH), BF16),
                   jax.ShapeDtypeStruct((batch, 1, RG_WIDTH), F32)],
        scratch_shapes=[pltpu.VMEM((rows + 8, RG_BW), F32), pltpu.VMEM((8, RG_BW), F32)],
        compiler_params=_params(("parallel", "parallel", "arbitrary")),
        name="rg_prompt",
    )(proj, proj, cw, cb, wa, ba, wx, bx, ap, jnp.zeros((m_total, RG_WIDTH), BF16))


def _rg_sample_kernel(x_ref, gate_ref, b0_ref, b1_ref, b2_ref, h0_ref, cw_ref, cb_ref,
                      wa_ref, ba_ref, wx_ref, bx_ref, ap_ref, y_prev_ref, y_ref, h_out_ref):
    del y_prev_ref
    conv = cb_ref[...]
    for j, r in enumerate((b0_ref, b1_ref, b2_ref, x_ref)):
        conv = conv + r[...] * cw_ref[j:j + 1, :]
    a, b = _rg_gates(conv, wa_ref, ba_ref, wx_ref, bx_ref, ap_ref)
    h = a * h0_ref[...] + b
    h_out_ref[...] = h
    y_ref[...] = (h * _gelu_tanh(gate_ref[...])).astype(y_ref.dtype)


def _rg_sample(proj, conv_state, h0, cw, cb, wa, ba, wx, bx, ap, y_all, row0):
    n_seq = h0.shape[0]
    assert row0 % n_seq == 0 and n_seq % BF16_SUBLANES == 0
    blk0 = row0 // n_seq
    xoff = (2 * HG_F + 2 * HG_I) // LANES
    goff = xoff + RG_BLOCKS
    tok = lambda off: pl.BlockSpec((n_seq, LANES), lambda n: (blk0, off + n))
    buf = lambda j: pl.BlockSpec((n_seq, LANES), lambda n: (0, j * RG_BLOCKS + n))
    vec = pl.BlockSpec((1, LANES), lambda n: (0, n))
    blkw = pl.BlockSpec((None, RG_BW, RG_BW), lambda n: (n, 0, 0))
    blkb = pl.BlockSpec((None, 1, RG_BW), lambda n: (n, 0, 0))
    flat_state = conv_state.reshape(n_seq, (CONV_W - 1) * RG_WIDTH)
    return pl.pallas_call(
        _rg_sample_kernel,
        grid=(RG_BLOCKS,),
        in_specs=[tok(xoff), tok(goff), buf(0), buf(1), buf(2),
                  pl.BlockSpec((n_seq, LANES), lambda n: (0, n)),
                  pl.BlockSpec((CONV_W, LANES), lambda n: (0, n)), vec,
                  blkw, blkb, blkw, blkb, vec,
                  pl.BlockSpec(memory_space=pl.ANY)],
        out_specs=[pl.BlockSpec((n_seq, LANES), lambda n: (blk0, n)),
                   pl.BlockSpec((n_seq, LANES), lambda n: (0, n))],
        out_shape=[jax.ShapeDtypeStruct(y_all.shape, y_all.dtype),
                   jax.ShapeDtypeStruct((n_seq, RG_WIDTH), F32)],
        input_output_aliases={13: 0},
        compiler_params=_params(("parallel",)),
        name="rg_sample",
    )(proj, proj, flat_state, flat_state, flat_state, h0, cw, cb, wa, ba, wx, bx, ap, y_all)


def _merge_out_kernel(o_ref, y_ref, ma_ref, mb_ref, x_ref, wa_ref, wb_ref, wo_ref, g_ref,
                      xo_ref, h_ref, *maybe_h32_ref):
    merged = (_sigmoid(ma_ref[...]) * _dot(o_ref[...], wa_ref[...])
              + _sigmoid(mb_ref[...]) * _dot(y_ref[...], wb_ref[...]))
    x_new = x_ref[...] + _dot(merged.astype(BF16), wo_ref[...])
    xo_ref[...] = x_new
    h = _rms(x_new, g_ref[...])
    h_ref[...] = h.astype(h_ref.dtype)
    for r in maybe_h32_ref:
        r[...] = h


def _merge_out(o, y, proj, x, w_a, w_b, w_o, g, with_h32):
    m, d = x.shape
    tm = _row_tile(m, 320)
    moff = (2 * HG_F + 2 * HG_I + 2 * RG_WIDTH) // d
    row = lambda width, col: pl.BlockSpec((tm, width), lambda i: (i, col))
    res = lambda shape: pl.BlockSpec(shape, lambda i: (0, 0), pipeline_mode=pl.Buffered(1))
    n_out = 3 if with_h32 else 2
    return pl.pallas_call(
        _merge_out_kernel,
        grid=(m // tm,),
        in_specs=[row(HG_I, 0), row(RG_WIDTH, 0), row(d, moff), row(d, moff + 1), row(d, 0),
                  res(w_a.shape), res(w_b.shape), res(w_o.shape), res((1, d))],
        out_specs=[row(d, 0)] * n_out,
        out_shape=[jax.ShapeDtypeStruct((m, d), F32), jax.ShapeDtypeStruct((m, d), BF16),
                   jax.ShapeDtypeStruct((m, d), F32)][:n_out],
        compiler_params=_params(("parallel",)),
        name="merge_out",
    )(o, y, proj, proj, x, w_a, w_b, w_o, g.reshape(1, d))


def _ffn_kernel(h_ref, x_ref, wg_ref, wu_ref, wd_ref, g_ref, o_ref, xo_ref, acc_ref):
    f = pl.program_id(1)

    @pl.when(f == 0)
    def _():
        acc_ref[...] = jnp.zeros_like(acc_ref)

    h = h_ref[...]
    gate = _dot(h, wg_ref[...])
    act = (gate * _sigmoid(gate)) * _dot(h, wu_ref[...])
    acc_ref[...] += _dot(act.astype(BF16), wd_ref[...])

    @pl.when(f == pl.num_programs(1) - 1)
    def _():
        x_new = x_ref[...] + acc_ref[...]
        xo_ref[...] = x_new
        o_ref[...] = _rms(x_new, g_ref[...]).astype(o_ref.dtype)


def _ffn(h, x, wg, wu, wd, g, out_dtype, tf):
    m, d = x.shape
    ff = wg.shape[1]
    tm = _row_tile(m, 416)
    return pl.pallas_call(
        _ffn_kernel,
        grid=(m // tm, ff // tf),
        in_specs=[pl.BlockSpec((tm, d), lambda i, f: (i, 0)),
                  pl.BlockSpec((tm, d), lambda i, f: (i, 0)),
                  pl.BlockSpec((d, tf), lambda i, f: (0, f)),
                  pl.BlockSpec((d, tf), lambda i, f: (0, f)),
                  pl.BlockSpec((tf, d), lambda i, f: (f, 0)),
                  pl.BlockSpec((1, d), lambda i, f: (0, 0))],
        out_specs=[pl.BlockSpec((tm, d), lambda i, f: (i, 0)),
                   pl.BlockSpec((tm, d), lambda i, f: (i, 0))],
        out_shape=[jax.ShapeDtypeStruct((m, d), out_dtype), jax.ShapeDtypeStruct((m, d), F32)],
        scratch_shapes=[pltpu.VMEM((tm, d), F32)],
        compiler_params=_params(("parallel", "arbitrary")),
        name="ffn_dense",
    )(h, x, wg, wu, wd, g.reshape(1, d))


def _router_kernel(h_ref, w_ref, info_ref, gate_ref, cnt_ref, carry_ref):
    @pl.when(pl.program_id(0) == 0)
    def _():
        carry_ref[...] = jnp.zeros_like(carry_ref)

    logits = _dot(h_ref[...], w_ref[...])
    tm = logits.shape[0]
    lane = lax.broadcasted_iota(I32, logits.shape, 1)
    neg = jnp.float32(-jnp.inf)
    logits = jnp.where(lane < N_EXPERTS, logits, neg)
    m1 = jnp.max(logits, axis=-1, keepdims=True)
    i1 = jnp.min(jnp.where(logits == m1, lane, LANES), axis=-1, keepdims=True)
    rest = jnp.where(lane == i1, neg, logits)
    m2 = jnp.max(rest, axis=-1, keepdims=True)
    i2 = jnp.min(jnp.where(rest == m2, lane, LANES), axis=-1, keepdims=True)
    e = jnp.exp(m2 - m1)
    g1 = 1.0 / (1.0 + e)
    g2 = e / (1.0 + e)
    gate_ref[...] = jnp.where(lane == 0, g1, jnp.where(lane == 1, g2, 0.0))

    chosen = jnp.where((lane == i1) | (lane == i2), 1.0, 0.0)
    lower = jnp.where(lax.broadcasted_iota(I32, (tm, tm), 0) > lax.broadcasted_iota(I32, (tm, tm), 1),
                      1.0, 0.0).astype(BF16)
    before = _dot(lower, chosen.astype(BF16)) + carry_ref[0:1, :]
    r1 = jnp.sum(jnp.where(lane == i1, before, 0.0), axis=-1, keepdims=True).astype(I32)
    r2 = jnp.sum(jnp.where(lane == i2, before, 0.0), axis=-1, keepdims=True).astype(I32)
    info_ref[...] = jnp.where(lane == 0, i1, jnp.where(lane == 1, i2,
                              jnp.where(lane == 2, r1, jnp.where(lane == 3, r2, 0))))
    total = carry_ref[0:1, :] + jnp.sum(chosen, axis=0, keepdims=True)
    carry_ref[...] = jnp.broadcast_to(total, carry_ref.shape)
    cnt_ref[...] = jnp.broadcast_to(total, cnt_ref.shape)


def _router(h, w_router):
    m, d = h.shape
    tm = _row_tile(m, 640)
    w = jnp.pad(w_router, ((0, 0), (0, LANES - N_EXPERTS))).astype(BF16)
    return pl.pallas_call(
        _router_kernel,
        grid=(m // tm,),
        in_specs=[pl.BlockSpec((tm, d), lambda i: (i, 0)),
                  pl.BlockSpec((d, LANES), lambda i: (0, 0))],
        out_specs=[pl.BlockSpec((tm, LANES), lambda i: (i, 0)),
                   pl.BlockSpec((tm, LANES), lambda i: (i, 0)),
                   pl.BlockSpec((SUBLANES, LANES), lambda i: (0, 0))],
        out_shape=[jax.ShapeDtypeStruct((m, LANES), I32),
                   jax.ShapeDtypeStruct((m, LANES), F32),
                   jax.ShapeDtypeStruct((SUBLANES, LANES), F32)],
        scratch_shapes=[pltpu.VMEM((SUBLANES, LANES), F32)],
        compiler_params=_params(("arbitrary",)),
        name="router",
    )(h, w)


def _row_copy(src_ref, src_row, dst_ref, dst_row, sem):
    return pltpu.make_async_copy(src_ref.at[pl.ds(src_row, 1), :], dst_ref.at[pl.ds(dst_row, 1), :], sem)


def _rows_wait(src_ref, dst_ref, n_rows, sem):
    pltpu.make_async_copy(src_ref.at[pl.ds(0, n_rows), :], dst_ref.at[pl.ds(0, n_rows), :], sem).wait()


def _dispatch_kernel(off_ref, tok_ref, h_ref, xs_in_ref, xs_ref, pos_ref, sem):
    del xs_in_ref
    tb = pos_ref.shape[-1] // TOP_K
    base = pl.program_id(0) * tb

    def body(j, carry):
        for k in range(TOP_K):
            p = off_ref[tok_ref[0, 4 * j + k]] + tok_ref[0, 4 * j + 2 + k]
            pos_ref[0, TOP_K * j + k] = p
            _row_copy(h_ref, base + j, xs_ref, p, sem).start()
        return carry

    lax.fori_loop(0, tb, body, 0)
    for _ in range(TOP_K):
        _rows_wait(h_ref, xs_ref, tb, sem)


def _dispatch(h32, tok_info, offsets, n_rows):
    m, d = h32.shape
    tb = _row_tile(m, 512)
    tok = tok_info[:, :4].reshape(m // tb, 1, 4 * tb)
    return pl.pallas_call(
        _dispatch_kernel,
        grid=(m // tb,),
        in_specs=[pl.BlockSpec(memory_space=pltpu.SMEM),
                  pl.BlockSpec((None, 1, 4 * tb), lambda i: (i, 0, 0), memory_space=pltpu.SMEM),
                  pl.BlockSpec(memory_space=pl.ANY), pl.BlockSpec(memory_space=pl.ANY)],
        out_specs=[pl.BlockSpec(memory_space=pl.ANY),
                   pl.BlockSpec((None, 1, TOP_K * tb), lambda i: (i, 0, 0), memory_space=pltpu.SMEM)],
        out_shape=[jax.ShapeDtypeStruct((n_rows, d), F32),
                   jax.ShapeDtypeStruct((m // tb, 1, TOP_K * tb), I32)],
        scratch_shapes=[pltpu.SemaphoreType.DMA(())],
        input_output_aliases={3: 0},
        compiler_params=_params(("arbitrary",)),
        name="moe_dispatch",
    )(offsets, tok, h32, jnp.zeros((n_rows, d), F32))


def _experts_kernel(be_ref, nv_ref, x_ref, wg_ref, wu_ref, wd_ref, y_ref, xbf_ref, acc_ref):
    del be_ref
    f = pl.program_id(1)
    used = pl.program_id(0) < nv_ref[0]

    @pl.when(jnp.logical_not(used) & (f == 0))
    def _():
        y_ref[...] = jnp.zeros_like(y_ref)

    @pl.when(used)
    def _():
        @pl.when(f == 0)
        def _():
            xbf_ref[...] = x_ref[...].astype(BF16)

        h = xbf_ref[...]
        gate = _dot(h, wg_ref[...])
        act = (gate * _sigmoid(gate)) * _dot(h, wu_ref[...])
        part = _dot(act.astype(BF16), wd_ref[...])

        @pl.when(f == 0)
        def _():
            acc_ref[...] = part

        @pl.when(f > 0)
        def _():
            acc_ref[...] += part

        @pl.when(f == pl.num_programs(1) - 1)
        def _():
            y_ref[...] = acc_ref[...]


def _experts(xs, wg, wu, wd, blk_expert, n_valid, tf):
    n_rows, d = xs.shape
    ff = wg.shape[-1]
    n_f = ff // tf
    tmb = EXPERT_ROWS
    xmap = lambda b, f, be, nv: (jnp.minimum(b, nv[0] - 1), 0)
    fidx = lambda b, f, nv: jnp.where(b < nv[0], f, n_f - 1)
    grid_spec = pltpu.PrefetchScalarGridSpec(
        num_scalar_prefetch=2,
        grid=(n_rows // tmb, n_f),
        in_specs=[pl.BlockSpec((tmb, d), xmap),
                  pl.BlockSpec((None, d, tf), lambda b, f, be, nv: (be[b], 0, fidx(b, f, nv))),
                  pl.BlockSpec((None, d, tf), lambda b, f, be, nv: (be[b], 0, fidx(b, f, nv))),
                  pl.BlockSpec((None, tf, d), lambda b, f, be, nv: (be[b], fidx(b, f, nv), 0))],
        out_specs=pl.BlockSpec((tmb, d), lambda b, f, be, nv: (b, 0)),
        scratch_shapes=[pltpu.VMEM((tmb, d), BF16), pltpu.VMEM((tmb, d), F32)],
    )
    return pl.pallas_call(
        _experts_kernel,
        grid_spec=grid_spec,
        out_shape=jax.ShapeDtypeStruct((n_rows, d), F32),
        compiler_params=_params(("arbitrary", "arbitrary")),
        name="moe_experts",
    )(blk_expert, n_valid, xs, wg, wu, wd)


def _combine_kernel(pos_ref, x_ref, gate_ref, g_ref, y_ref, o_ref, ya_ref, yb_ref, sem):
    tb = x_ref.shape[0]

    def body(j, carry):
        _row_copy(y_ref, pos_ref[0, TOP_K * j], ya_ref, j, sem).start()
        _row_copy(y_ref, pos_ref[0, TOP_K * j + 1], yb_ref, j, sem).start()
        return carry

    lax.fori_loop(0, tb, body, 0)
    _rows_wait(y_ref, ya_ref, tb, sem)
    _rows_wait(y_ref, yb_ref, tb, sem)
    gate = gate_ref[...]
    x_new = x_ref[...] + (gate[:, 0:1] * ya_ref[...] + gate[:, 1:2] * yb_ref[...])
    o_ref[...] = _rms(x_new, g_ref[...]).astype(o_ref.dtype)


def _combine(x, y, pos, gates, g, out_dtype):
    m, d = x.shape
    tb = pos.shape[-1] // TOP_K
    return pl.pallas_call(
        _combine_kernel,
        grid=(m // tb,),
        in_specs=[pl.BlockSpec((None, 1, TOP_K * tb), lambda i: (i, 0, 0), memory_space=pltpu.SMEM),
                  pl.BlockSpec((tb, d), lambda i: (i, 0)),
                  pl.BlockSpec((tb, LANES), lambda i: (i, 0)),
                  pl.BlockSpec((1, d), lambda i: (0, 0)),
                  pl.BlockSpec(memory_space=pl.ANY)],
        out_specs=pl.BlockSpec((tb, d), lambda i: (i, 0)),
        out_shape=jax.ShapeDtypeStruct((m, d), out_dtype),
        scratch_shapes=[pltpu.VMEM((tb, d), F32), pltpu.VMEM((tb, d), F32),
                        pltpu.SemaphoreType.DMA(())],
        compiler_params=_params(("arbitrary",)),
        name="moe_combine",
    )(pos, x, gates, g.reshape(1, d), y)


def _moe(h, h32, x, w_router, wg, wu, wd, g, out_dtype):
    m, d = x.shape
    tmb = EXPERT_ROWS
    info, gates, cnt = _router(h, w_router)

    counts = cnt[0, :N_EXPERTS].astype(I32)
    padded = (counts + (tmb - 1)) // tmb * tmb
    ends = jnp.cumsum(padded)
    offsets = ends - padded
    n_blocks = -(-(TOP_K * m + N_EXPERTS * (tmb - 1)) // tmb)
    n_valid = ends[-1] // tmb
    starts = jnp.arange(n_blocks, dtype=I32) * tmb
    owner = jnp.sum((starts[:, None] >= ends[None, :]).astype(I32), axis=1)
    blk_expert = jnp.minimum(jnp.where(starts < ends[-1], owner, owner[n_valid - 1]), N_EXPERTS - 1)

    xs, pos = _dispatch(h32, info, offsets, n_blocks * tmb)
    y = _experts(xs, wg, wu, wd, blk_expert, n_valid.reshape(1), 512)
    return _combine(x, y, pos, gates, g, out_dtype)


def _pad_ff(w, axis, mult):
    ff = w.shape[axis]
    pad = (-ff) % mult
    if pad:
        widths = [(0, 0)] * w.ndim
        widths[axis] = (0, pad)
        w = jnp.pad(w, widths)
    return w


def kernel(x_prompt, x_sample, state_hgrn, state_rglru, state_conv, hgrn_lb_logits, norm_mix, w_in,
           hgrn_gnorm, rg_conv_w, rg_conv_b, rg_wa, rg_ba, rg_wx, rg_bx, rg_a_param, w_br_a, w_br_b,
           w_out, norm_ffn, ffn_w_gate, ffn_w_up, ffn_w_down, moe_router, moe_w_gate, moe_w_up,
           moe_w_down, norm_final):
    depth = w_in.shape[0]
    batch, seq, d = x_prompt.shape
    n_seq = x_sample.shape[0]
    m_prompt = batch * seq
    m = m_prompt + n_seq

    sm = jax.nn.softmax(hgrn_lb_logits.astype(F32), axis=0)
    lbs = jnp.maximum(jnp.cumsum(sm, axis=0) - sm[0:1], 0.0)
    log_lb = jnp.log(lbs)
    log1m_lb = jnp.log1p(-lbs)

    x = jnp.concatenate([x_prompt.reshape(m_prompt, d), x_sample.reshape(n_seq, d)], axis=0)
    h = _rmsnorm(x, norm_mix[0], BF16)

    chunk = 256 if seq % 256 == 0 else seq
    hg_p, rg_p, cv_p, hg_s, rg_s, cv_s = [], [], [], [], [], []
    for l in range(depth):
        proj = _in_proj(h, w_in, l)
        vrow = lambda a: a[l].reshape(1, -1)
        o_all, s_prompt = _hgrn_prompt(proj, vrow(log_lb), vrow(log1m_lb), vrow(hgrn_gnorm),
                                       batch, seq, m, chunk)
        o_all, s_sample = _hgrn_sample(proj, vrow(log_lb), vrow(log1m_lb), vrow(hgrn_gnorm),
                                       state_hgrn[l], o_all, m_prompt)
        rg_args = (rg_conv_w[l], vrow(rg_conv_b), rg_wa[l], rg_ba[l].reshape(RG_BLOCKS, 1, RG_BW),
                   rg_wx[l], rg_bx[l].reshape(RG_BLOCKS, 1, RG_BW), vrow(rg_a_param))
        y_all, h_prompt = _rg_prompt(proj, *rg_args, batch, seq, m, chunk)
        y_all, h_sample = _rg_sample(proj, state_conv[l], state_rglru[l], *rg_args, y_all, m_prompt)

        dense = l % 2 == 0
        merged = _merge_out(o_all, y_all, proj, x, w_br_a[l].astype(BF16), w_br_b[l].astype(BF16),
                            w_out[l].astype(BF16), norm_ffn[l], with_h32=not dense)
        x, h2 = merged[0], merged[1]

        last = l == depth - 1
        g_next = norm_final if last else norm_mix[l + 1]
        out_dtype = F32 if last else BF16
        j = l // 2
        if dense:
            wg = _pad_ff(ffn_w_gate[j], 1, 512).astype(BF16)
            wu = _pad_ff(ffn_w_up[j], 1, 512).astype(BF16)
            wd = _pad_ff(ffn_w_down[j], 0, 512).astype(BF16)
            h, x = _ffn(h2, x, wg, wu, wd, g_next, out_dtype, 512)
        else:
            assert last
            h = _moe(h2, merged[2], x, moe_router[j], moe_w_gate[j].astype(BF16),
                     moe_w_up[j].astype(BF16), moe_w_down[j].astype(BF16), g_next, out_dtype)

        xr = proj[:, 2 * HG_F + 2 * HG_I:2 * HG_F + 2 * HG_I + RG_WIDTH]
        hg_p.append(s_prompt)
        rg_p.append(h_prompt.reshape(batch, RG_WIDTH))
        cv_p.append(xr[:m_prompt].reshape(batch, seq, RG_WIDTH)[:, seq - (CONV_W - 1):])
        hg_s.append(s_sample)
        rg_s.append(h_sample)
        cv_s.append(jnp.concatenate([state_conv[l][:, 1:], xr[m_prompt:, None, :]], axis=1))

    y_prompt = h[:m_prompt].reshape(batch, seq, d)
    y_sample = h[m_prompt:].reshape(n_seq, 1, d)
    return (y_prompt, y_sample, jnp.stack(hg_p), jnp.stack(rg_p), jnp.stack(cv_p),
            jnp.stack(hg_s), jnp.stack(rg_s), jnp.stack(cv_s))
```

```python
import functools

import jax
import jax.numpy as jnp
from jax import lax
from jax.experimental import pallas as pl
from jax.experimental.pallas import tpu as pltpu

D_MODEL = 2048
HG_HEADS = 8
HG_DK = 128
HG_DV = 128
HG_F = HG_HEADS * HG_DK
HG_I = HG_HEADS * HG_DV
RG_WIDTH = 1024
RG_BLOCKS = 8
RG_BW = RG_WIDTH // RG_BLOCKS
CONV_W = 4
RG_C = 8.0
IN_COLS = 2 * HG_F + 2 * HG_I + 2 * RG_WIDTH + 2 * D_MODEL
N_EXPERTS = 8
TOP_K = 2
EPS = 1e-6

LANES = 128
SUBLANES = 8
BF16_SUBLANES = 16
VMEM_LIMIT = 56 * 1024 * 1024
EXPERT_ROWS = 512

BF16 = jnp.bfloat16
F32 = jnp.float32
I32 = jnp.int32


def _params(sem, vmem=VMEM_LIMIT):
    return pltpu.CompilerParams(dimension_semantics=sem, vmem_limit_bytes=vmem)


def _row_tile(m, target):
    best = None
    for t in range(BF16_SUBLANES, min(m, target) + 1, BF16_SUBLANES):
        if m % t == 0:
            best = t
    assert best is not None, (m, target)
    return best


def _dot(a, b):
    return jnp.dot(a, b, preferred_element_type=F32)


def _dot_nt(a, b):
    return lax.dot_general(a, b, (((1,), (1,)), ((), ())), preferred_element_type=F32)


def _dot_tn(a, b):
    return lax.dot_general(a, b, (((0,), (0,)), ((), ())), preferred_element_type=F32)


def _sigmoid(x):
    return 1.0 / (1.0 + jnp.exp(-x))


def _log_sigmoid(x):
    return jnp.minimum(x, 0.0) - jnp.log1p(jnp.exp(-jnp.abs(x)))


def _rms(x, g):
    ms = jnp.mean(x * x, axis=-1, keepdims=True)
    return x * lax.rsqrt(ms + EPS) * g


def _rmsnorm_kernel(x_ref, g_ref, o_ref):
    o_ref[...] = _rms(x_ref[...], g_ref[...]).astype(o_ref.dtype)


def _rmsnorm(x, g, out_dtype):
    m, d = x.shape
    tm = _row_tile(m, 640)
    return pl.pallas_call(
        _rmsnorm_kernel,
        grid=(m // tm,),
        in_specs=[pl.BlockSpec((tm, d), lambda i: (i, 0)),
                  pl.BlockSpec((1, d), lambda i: (0, 0))],
        out_specs=pl.BlockSpec((tm, d), lambda i: (i, 0)),
        out_shape=jax.ShapeDtypeStruct((m, d), out_dtype),
        compiler_params=_params(("parallel",)),
        name="rmsnorm",
    )(x, g.reshape(1, d))


def _in_proj_kernel(x_ref, w_ref, o_ref, wbf_ref):
    @pl.when(pl.program_id(1) == 0)
    def _():
        wbf_ref[...] = w_ref[...].astype(BF16)

    o_ref[...] = _dot(x_ref[...], wbf_ref[...])


def _in_proj(h, w_in, layer):
    m, d = h.shape
    n = w_in.shape[-1]
    tm = _row_tile(m, 1664)
    tn = 512
    return pl.pallas_call(
        _in_proj_kernel,
        grid=(n // tn, m // tm),
        in_specs=[pl.BlockSpec((tm, d), lambda j, i: (i, 0)),
                  pl.BlockSpec((None, d, tn), lambda j, i: (layer, 0, j))],
        out_specs=pl.BlockSpec((tm, tn), lambda j, i: (i, j)),
        out_shape=jax.ShapeDtypeStruct((m, n), F32),
        scratch_shapes=[pltpu.VMEM((d, tn), BF16)],
        compiler_params=_params(("arbitrary", "arbitrary")),
        name="in_proj",
    )(h, w_in)


def _hgrn_gates(fz, log_lb, log1m_lb):
    t = jnp.log1p(jnp.exp(-jnp.abs(fz)))
    c = log1m_lb + (jnp.minimum(fz, 0.0) - t)
    log_f = jnp.maximum(log_lb, c) + jnp.log1p(jnp.exp(-jnp.abs(log_lb - c)))
    one_minus_f = jnp.exp(log1m_lb + (jnp.minimum(-fz, 0.0) - t))
    return log_f, one_minus_f


def _hgrn_out(o, gate, gnorm):
    return _rms(o, gnorm) * (gate * _sigmoid(gate))


def _hgrn_prompt_kernel(q_ref, f_ref, i_ref, g_ref, llb_ref, l1lb_ref, gn_ref, o_init_ref,
                        o_ref, s_out_ref, s_ref, *, chunk):
    del o_init_ref
    C = chunk
    c = pl.program_id(2)

    @pl.when(c == 0)
    def _():
        s_ref[...] = jnp.zeros_like(s_ref)

    q = q_ref[...]
    v = i_ref[...].astype(BF16)
    g, k = _hgrn_gates(f_ref[...], llb_ref[...], l1lb_ref[...])

    row = lax.broadcasted_iota(I32, (C, HG_DK), 0)
    rc_xor = lax.broadcasted_iota(I32, (C, C), 0) ^ lax.broadcasted_iota(I32, (C, C), 1)

    a_mat = jnp.where(rc_xor == 0, _dot_nt(q.astype(BF16), k.astype(BF16)), 0.0)

    pre, suf, tot = g, jnp.zeros_like(g), g
    w = 1
    while w < C:
        upper = (row & w) != 0
        e = jnp.exp(jnp.where(upper, pre, suf))
        qw = jnp.where(upper, q * e, 0.0).astype(BF16)
        kw = jnp.where(upper, 0.0, k * e).astype(BF16)
        a_mat = a_mat + jnp.where(rc_xor < 2 * w, _dot_nt(qw, kw), 0.0)
        up = pltpu.roll(tot, w, 0)
        dn = pltpu.roll(tot, C - w, 0)
        pre = pre + jnp.where(upper, up, 0.0)
        suf = suf + jnp.where(upper, 0.0, dn)
        tot = tot + jnp.where(upper, up, dn)
        w *= 2

    s_old = s_ref[...]
    o = _dot(a_mat.astype(BF16), v) + _dot((q * jnp.exp(pre)).astype(BF16), s_old.astype(BF16))
    decay_rows = jnp.exp(jnp.broadcast_to(tot[0:1, :], (HG_DK, HG_DK))).T
    s_new = decay_rows * s_old + _dot_tn((k * jnp.exp(suf)).astype(BF16), v)
    s_ref[...] = s_new

    o_ref[...] = _hgrn_out(o, g_ref[...], gn_ref[...]).astype(o_ref.dtype)

    @pl.when(c == pl.num_programs(2) - 1)
    def _():
        s_out_ref[...] = s_new


def _hgrn_prompt(proj, log_lb, log1m_lb, gnorm, batch, seq, m_total, chunk):
    n_chunks = seq // chunk
    tok = lambda off: pl.BlockSpec((chunk, LANES), lambda b, h, c: (b * n_chunks + c, off + h))
    vec = pl.BlockSpec((1, LANES), lambda b, h, c: (0, h))
    return pl.pallas_call(
        functools.partial(_hgrn_prompt_kernel, chunk=chunk),
        grid=(batch, HG_HEADS, n_chunks),
        in_specs=[tok(0), tok(HG_HEADS), tok(2 * HG_HEADS), tok(3 * HG_HEADS), vec, vec, vec,
                  pl.BlockSpec(memory_space=pl.ANY)],
        input_output_aliases={7: 0},
        out_specs=[pl.BlockSpec((chunk, LANES), lambda b, h, c: (b * n_chunks + c, h)),
                   pl.BlockSpec((None, None, HG_DK, HG_DV), lambda b, h, c: (b, h, 0, 0))],
        out_shape=[jax.ShapeDtypeStruct((m_total, HG_I), BF16),
                   jax.ShapeDtypeStruct((batch, HG_HEADS, HG_DK, HG_DV), F32)],
        scratch_shapes=[pltpu.VMEM((HG_DK, HG_DV), F32)],
        compiler_params=_params(("parallel", "parallel", "arbitrary")),
        name="hgrn_prompt",
    )(proj, proj, proj, proj, log_lb, log1m_lb, gnorm, jnp.zeros((m_total, HG_I), BF16))


HGRN_SAMPLE_ROWS = 16


def _hgrn_sample_kernel(q_ref, f_ref, i_ref, g_ref, llb_ref, l1lb_ref, gn_ref, s_ref, o_prev_ref,
                        o_ref, s_out_ref):
    del o_prev_ref
    n = HGRN_SAMPLE_ROWS
    q = q_ref[...]
    vi = i_ref[...]
    g, k = _hgrn_gates(f_ref[...], llb_ref[...], l1lb_ref[...])
    f = jnp.exp(g)

    def columns(x):
        pad = jnp.zeros((LANES - n, HG_DK), F32)
        return jnp.concatenate([x, pad], axis=0).T

    f_t, k_t, q_t = columns(f), columns(k), columns(q)
    seq_id = lax.broadcasted_iota(I32, (n, HG_DV), 0)
    o = jnp.zeros((n, HG_DV), F32)
    for j in range(n):
        bc = lambda xt: jnp.broadcast_to(xt[:, j:j + 1], (HG_DK, HG_DV))
        s_new = bc(f_t) * s_ref[j] + bc(k_t) * vi[j:j + 1, :]
        s_out_ref[j] = s_new
        o_j = jnp.sum(bc(q_t) * s_new, axis=0, keepdims=True)
        o = jnp.where(seq_id == j, o_j, o)
    o_ref[...] = _hgrn_out(o, g_ref[...], gn_ref[...]).astype(o_ref.dtype)


def _hgrn_sample(proj, log_lb, log1m_lb, gnorm, state, o_all, row0):
    n_seq = state.shape[0]
    n = HGRN_SAMPLE_ROWS
    assert n_seq % n == 0 and row0 % n == 0
    blk0 = row0 // n
    tok = lambda off: pl.BlockSpec((n, LANES), lambda j, h: (blk0 + j, off + h))
    vec = pl.BlockSpec((1, LANES), lambda j, h: (0, h))
    st = pl.BlockSpec((n, None, HG_DK, HG_DV), lambda j, h: (j, h, 0, 0))
    return pl.pallas_call(
        _hgrn_sample_kernel,
        grid=(n_seq // n, HG_HEADS),
        in_specs=[tok(0), tok(HG_HEADS), tok(2 * HG_HEADS), tok(3 * HG_HEADS), vec, vec, vec, st,
                  pl.BlockSpec(memory_space=pl.ANY)],
        out_specs=[pl.BlockSpec((n, LANES), lambda j, h: (blk0 + j, h)), st],
        out_shape=[jax.ShapeDtypeStruct(o_all.shape, o_all.dtype),
                   jax.ShapeDtypeStruct(state.shape, F32)],
        input_output_aliases={8: 0},
        compiler_params=_params(("parallel", "parallel")),
        name="hgrn_sample",
    )(proj, proj, proj, proj, log_lb, log1m_lb, gnorm, state, o_all)


def _gelu_tanh(x):
    return 0.5 * x * (1.0 + jnp.tanh(0.7978845608028654 * (x + 0.044715 * (x * x * x))))


def _rg_gates(conv, wa_ref, ba_ref, wx_ref, bx_ref, ap_ref):
    cb = conv.astype(BF16)
    r = _sigmoid(_dot(cb, wa_ref[...].astype(BF16)) + ba_ref[...])
    ig = _sigmoid(_dot(cb, wx_ref[...].astype(BF16)) + bx_ref[...])
    log_a = RG_C * r * _log_sigmoid(ap_ref[...])
    a = jnp.exp(log_a)
    t = jnp.tanh(log_a)
    b = jnp.sqrt(-2.0 * t / (1.0 - t)) * ig * conv
    return a, b


def _rg_prompt_kernel(x_ref, gate_ref, cw_ref, cb_ref, wa_ref, ba_ref, wx_ref, bx_ref, ap_ref,
                      y_init_ref, y_ref, h_out_ref, xp_ref, h_ref, *, rows):
    del y_init_ref
    T = rows
    t = pl.program_id(2)

    @pl.when(t == 0)
    def _():
        xp_ref[0:8, :] = jnp.zeros((8, RG_BW), F32)
        h_ref[...] = jnp.zeros_like(h_ref)

    xp_ref[8:8 + T, :] = x_ref[...]
    conv = cb_ref[...]
    for j in range(CONV_W):
        conv = conv + xp_ref[5 + j:5 + j + T, :] * cw_ref[j:j + 1, :]
    xp_ref[0:8, :] = xp_ref[T:T + 8, :]

    a, b = _rg_gates(conv, wa_ref, ba_ref, wx_ref, bx_ref, ap_ref)

    row = lax.broadcasted_iota(I32, (T, RG_BW), 0)
    d = 1
    while d < T:
        ok = row >= d
        b = jnp.where(ok, a * pltpu.roll(b, d, 0) + b, b)
        a = jnp.where(ok, a * pltpu.roll(a, d, 0), a)
        d *= 2
    h = a * h_ref[0:1, :] + b
    h_last = h[T - 1:T, :]
    h_ref[...] = jnp.broadcast_to(h_last, h_ref.shape)

    y_ref[...] = (h * _gelu_tanh(gate_ref[...])).astype(y_ref.dtype)

    @pl.when(t == pl.num_programs(2) - 1)
    def _():
        h_out_ref[...] = h_last


def _rg_prompt(proj, cw, cb, wa, ba, wx, bx, ap, batch, seq, m_total, rows):
    n_t = seq // rows
    xoff = (2 * HG_F + 2 * HG_I) // LANES
    goff = xoff + RG_BLOCKS
    tok = lambda off: pl.BlockSpec((rows, LANES), lambda b, n, t: (b * n_t + t, off + n))
    vec = pl.BlockSpec((1, LANES), lambda b, n, t: (0, n))
    blkw = pl.BlockSpec((None, RG_BW, RG_BW), lambda b, n, t: (n, 0, 0))
    blkb = pl.BlockSpec((None, 1, RG_BW), lambda b, n, t: (n, 0, 0))
    return pl.pallas_call(
        functools.partial(_rg_prompt_kernel, rows=rows),
        grid=(batch, RG_BLOCKS, n_t),
        in_specs=[tok(xoff), tok(goff),
                  pl.BlockSpec((CONV_W, LANES), lambda b, n, t: (0, n)), vec,
                  blkw, blkb, blkw, blkb, vec, pl.BlockSpec(memory_space=pl.ANY)],
        input_output_aliases={9: 0},
        out_specs=[pl.BlockSpec((rows, LANES), lambda b, n, t: (b * n_t + t, n)),
                   pl.BlockSpec((None, 1, LANES), lambda b, n, t: (b, 0, n))],
        out_shape=[jax.ShapeDtypeStruct((m_total, RG_WIDTH), BF16),
                   jax.ShapeDtypeStruct((batch, 1, RG_WIDTH), F32)],
        scratch_shapes=[pltpu.VMEM((rows + 8, RG_BW), F32), pltpu.VMEM((8, RG_BW), F32)],
        compiler_params=_params(("parallel", "parallel", "arbitrary")),
        name="rg_prompt",
    )(proj, proj, cw, cb, wa, ba, wx, bx, ap, jnp.zeros((m_total, RG_WIDTH), BF16))


def _rg_sample_kernel(x_ref, gate_ref, b0_ref, b1_ref, b2_ref, h0_ref, cw_ref, cb_ref,
                      wa_ref, ba_ref, wx_ref, bx_ref, ap_ref, y_prev_ref, y_ref, h_out_ref):
    del y_prev_ref
    conv = cb_ref[...]
    for j, r in enumerate((b0_ref, b1_ref, b2_ref, x_ref)):
        conv = conv + r[...] * cw_ref[j:j + 1, :]
    a, b = _rg_gates(conv, wa_ref, ba_ref, wx_ref, bx_ref, ap_ref)
    h = a * h0_ref[...] + b
    h_out_ref[...] = h
    y_ref[...] = (h * _gelu_tanh(gate_ref[...])).astype(y_ref.dtype)


def _rg_sample(proj, conv_state, h0, cw, cb, wa, ba, wx, bx, ap, y_all, row0):
    n_seq = h0.shape[0]
    assert row0 % n_seq == 0 and n_seq % BF16_SUBLANES == 0
    blk0 = row0 // n_seq
    xoff = (2 * HG_F + 2 * HG_I) // LANES
    goff = xoff + RG_BLOCKS
    tok = lambda off: pl.BlockSpec((n_seq, LANES), lambda n: (blk0, off + n))
    buf = lambda j: pl.BlockSpec((n_seq, LANES), lambda n: (0, j * RG_BLOCKS + n))
    vec = pl.BlockSpec((1, LANES), lambda n: (0, n))
    blkw = pl.BlockSpec((None, RG_BW, RG_BW), lambda n: (n, 0, 0))
    blkb = pl.BlockSpec((None, 1, RG_BW), lambda n: (n, 0, 0))
    flat_state = conv_state.reshape(n_seq, (CONV_W - 1) * RG_WIDTH)
    return pl.pallas_call(
        _rg_sample_kernel,
        grid=(RG_BLOCKS,),
        in_specs=[tok(xoff), tok(goff), buf(0), buf(1), buf(2),
                  pl.BlockSpec((n_seq, LANES), lambda n: (0, n)),
                  pl.BlockSpec((CONV_W, LANES), lambda n: (0, n)), vec,
                  blkw, blkb, blkw, blkb, vec,
                  pl.BlockSpec(memory_space=pl.ANY)],
        out_specs=[pl.BlockSpec((n_seq, LANES), lambda n: (blk0, n)),
                   pl.BlockSpec((n_seq, LANES), lambda n: (0, n))],
        out_shape=[jax.ShapeDtypeStruct(y_all.shape, y_all.dtype),
                   jax.ShapeDtypeStruct((n_seq, RG_WIDTH), F32)],
        input_output_aliases={13: 0},
        compiler_params=_params(("parallel",)),
        name="rg_sample",
    )(proj, proj, flat_state, flat_state, flat_state, h0, cw, cb, wa, ba, wx, bx, ap, y_all)


def _merge_out_kernel(o_ref, y_ref, ma_ref, mb_ref, x_ref, wa_ref, wb_ref, wo_ref, g_ref,
                      xo_ref, h_ref, *maybe_h32_ref):
    merged = (_sigmoid(ma_ref[...]) * _dot(o_ref[...], wa_ref[...])
              + _sigmoid(mb_ref[...]) * _dot(y_ref[...], wb_ref[...]))
    x_new = x_ref[...] + _dot(merged.astype(BF16), wo_ref[...])
    xo_ref[...] = x_new
    h = _rms(x_new, g_ref[...])
    h_ref[...] = h.astype(h_ref.dtype)
    for r in maybe_h32_ref:
        r[...] = h


def _merge_out(o, y, proj, x, w_a, w_b, w_o, g, with_h32):
    m, d = x.shape
    tm = _row_tile(m, 320)
    moff = (2 * HG_F + 2 * HG_I + 2 * RG_WIDTH) // d
    row = lambda width, col: pl.BlockSpec((tm, width), lambda i: (i, col))
    res = lambda shape: pl.BlockSpec(shape, lambda i: (0, 0), pipeline_mode=pl.Buffered(1))
    n_out = 3 if with_h32 else 2
    return pl.pallas_call(
        _merge_out_kernel,
        grid=(m // tm,),
        in_specs=[row(HG_I, 0), row(RG_WIDTH, 0), row(d, moff), row(d, moff + 1), row(d, 0),
                  res(w_a.shape), res(w_b.shape), res(w_o.shape), res((1, d))],
        out_specs=[row(d, 0)] * n_out,
        out_shape=[jax.ShapeDtypeStruct((m, d), F32), jax.ShapeDtypeStruct((m, d), BF16),
                   jax.ShapeDtypeStruct((m, d), F32)][:n_out],
        compiler_params=_params(("parallel",)),
        name="merge_out",
    )(o, y, proj, proj, x, w_a, w_b, w_o, g.reshape(1, d))


def _ffn_kernel(h_ref, x_ref, wg_ref, wu_ref, wd_ref, g_ref, o_ref, xo_ref, acc_ref):
    f = pl.program_id(1)

    @pl.when(f == 0)
    def _():
        acc_ref[...] = jnp.zeros_like(acc_ref)

    h = h_ref[...]
    gate = _dot(h, wg_ref[...])
    act = (gate * _sigmoid(gate)) * _dot(h, wu_ref[...])
    acc_ref[...] += _dot(act.astype(BF16), wd_ref[...])

    @pl.when(f == pl.num_programs(1) - 1)
    def _():
        x_new = x_ref[...] + acc_ref[...]
        xo_ref[...] = x_new
        o_ref[...] = _rms(x_new, g_ref[...]).astype(o_ref.dtype)


def _ffn(h, x, wg, wu, wd, g, out_dtype, tf):
    m, d = x.shape
    ff = wg.shape[1]
    tm = _row_tile(m, 416)
    return pl.pallas_call(
        _ffn_kernel,
        grid=(m // tm, ff // tf),
        in_specs=[pl.BlockSpec((tm, d), lambda i, f: (i, 0)),
                  pl.BlockSpec((tm, d), lambda i, f: (i, 0)),
                  pl.BlockSpec((d, tf), lambda i, f: (0, f)),
                  pl.BlockSpec((d, tf), lambda i, f: (0, f)),
                  pl.BlockSpec((tf, d), lambda i, f: (f, 0)),
                  pl.BlockSpec((1, d), lambda i, f: (0, 0))],
        out_specs=[pl.BlockSpec((tm, d), lambda i, f: (i, 0)),
                   pl.BlockSpec((tm, d), lambda i, f: (i, 0))],
        out_shape=[jax.ShapeDtypeStruct((m, d), out_dtype), jax.ShapeDtypeStruct((m, d), F32)],
        scratch_shapes=[pltpu.VMEM((tm, d), F32)],
        compiler_params=_params(("parallel", "arbitrary")),
        name="ffn_dense",
    )(h, x, wg, wu, wd, g.reshape(1, d))


def _router_kernel(h_ref, w_ref, info_ref, gate_ref, cnt_ref, carry_ref):
    @pl.when(pl.program_id(0) == 0)
    def _():
        carry_ref[...] = jnp.zeros_like(carry_ref)

    logits = _dot(h_ref[...], w_ref[...])
    tm = logits.shape[0]
    lane = lax.broadcasted_iota(I32, logits.shape, 1)
    neg = jnp.float32(-jnp.inf)
    logits = jnp.where(lane < N_EXPERTS, logits, neg)
    m1 = jnp.max(logits, axis=-1, keepdims=True)
    i1 = jnp.min(jnp.where(logits == m1, lane, LANES), axis=-1, keepdims=True)
    rest = jnp.where(lane == i1, neg, logits)
    m2 = jnp.max(rest, axis=-1, keepdims=True)
    i2 = jnp.min(jnp.where(rest == m2, lane, LANES), axis=-1, keepdims=True)
    e = jnp.exp(m2 - m1)
    g1 = 1.0 / (1.0 + e)
    g2 = e / (1.0 + e)
    gate_ref[...] = jnp.where(lane == 0, g1, jnp.where(lane == 1, g2, 0.0))

    chosen = jnp.where((lane == i1) | (lane == i2), 1.0, 0.0)
    lower = jnp.where(lax.broadcasted_iota(I32, (tm, tm), 0) > lax.broadcasted_iota(I32, (tm, tm), 1),
                      1.0, 0.0).astype(BF16)
    before = _dot(lower, chosen.astype(BF16)) + carry_ref[0:1, :]
    r1 = jnp.sum(jnp.where(lane == i1, before, 0.0), axis=-1, keepdims=True).astype(I32)
    r2 = jnp.sum(jnp.where(lane == i2, before, 0.0), axis=-1, keepdims=True).astype(I32)
    info_ref[...] = jnp.where(lane == 0, i1, jnp.where(lane == 1, i2,
                              jnp.where(lane == 2, r1, jnp.where(lane == 3, r2, 0))))
    total = carry_ref[0:1, :] + jnp.sum(chosen, axis=0, keepdims=True)
    carry_ref[...] = jnp.broadcast_to(total, carry_ref.shape)
    cnt_ref[...] = jnp.broadcast_to(total, cnt_ref.shape)


def _router(h, w_router):
    m, d = h.shape
    tm = _row_tile(m, 640)
    w = jnp.pad(w_router, ((0, 0), (0, LANES - N_EXPERTS))).astype(BF16)
    return pl.pallas_call(
        _router_kernel,
        grid=(m // tm,),
        in_specs=[pl.BlockSpec((tm, d), lambda i: (i, 0)),
                  pl.BlockSpec((d, LANES), lambda i: (0, 0))],
        out_specs=[pl.BlockSpec((tm, LANES), lambda i: (i, 0)),
                   pl.BlockSpec((tm, LANES), lambda i: (i, 0)),
                   pl.BlockSpec((SUBLANES, LANES), lambda i: (0, 0))],
        out_shape=[jax.ShapeDtypeStruct((m, LANES), I32),
                   jax.ShapeDtypeStruct((m, LANES), F32),
                   jax.ShapeDtypeStruct((SUBLANES, LANES), F32)],
        scratch_shapes=[pltpu.VMEM((SUBLANES, LANES), F32)],
        compiler_params=_params(("arbitrary",)),
        name="router",
    )(h, w)


def _row_copy(src_ref, src_row, dst_ref, dst_row, sem):
    return pltpu.make_async_copy(src_ref.at[pl.ds(src_row, 1), :], dst_ref.at[pl.ds(dst_row, 1), :], sem)


def _rows_wait(src_ref, dst_ref, n_rows, sem):
    pltpu.make_async_copy(src_ref.at[pl.ds(0, n_rows), :], dst_ref.at[pl.ds(0, n_rows), :], sem).wait()


def _slots_kernel(off_ref, cnt_ref, end_ref, tok_ref, pos_ref, src_ref):
    tb = pos_ref.shape[-1] // TOP_K
    base = pl.program_id(0) * tb

    @pl.when(pl.program_id(0) == 0)
    def _():
        def clear(lo, hi):
            def body(p, carry):
                src_ref[p] = 0
                return carry
            lax.fori_loop(lo, hi, body, 0)

        for e in range(N_EXPERTS):
            clear(off_ref[e] + cnt_ref[e], end_ref[e])
        clear(end_ref[N_EXPERTS - 1], src_ref.shape[0])

    def body(j, carry):
        for k in range(TOP_K):
            p = off_ref[tok_ref[0, 4 * j + k]] + tok_ref[0, 4 * j + 2 + k]
            pos_ref[0, TOP_K * j + k] = p
            src_ref[p] = base + j
        return carry

    lax.fori_loop(0, tb, body, 0)


def _slots(tok_info, offsets, counts, ends, n_rows):
    m = tok_info.shape[0]
    tb = _row_tile(m, 512)
    tok = tok_info[:, :4].reshape(m // tb, 1, 4 * tb)
    smem = pl.BlockSpec(memory_space=pltpu.SMEM)
    return pl.pallas_call(
        _slots_kernel,
        grid=(m // tb,),
        in_specs=[smem, smem, smem,
                  pl.BlockSpec((None, 1, 4 * tb), lambda i: (i, 0, 0), memory_space=pltpu.SMEM)],
        out_specs=[pl.BlockSpec((None, 1, TOP_K * tb), lambda i: (i, 0, 0), memory_space=pltpu.SMEM),
                   smem],
        out_shape=[jax.ShapeDtypeStruct((m // tb, 1, TOP_K * tb), I32),
                   jax.ShapeDtypeStruct((n_rows,), I32)],
        compiler_params=_params(("arbitrary",)),
        name="moe_slots",
    )(offsets, counts, ends, tok)


def _gather_kernel(nv_ref, src_ref, h_ref, xs_ref, buf_ref, sem):
    tmb = buf_ref.shape[0]
    used = pl.program_id(0) < nv_ref[0]

    @pl.when(used)
    def _():
        def body(r, carry):
            _row_copy(h_ref, src_ref[0, r], buf_ref, r, sem).start()
            return carry

        lax.fori_loop(0, tmb, body, 0, unroll=8)
        _rows_wait(h_ref, buf_ref, tmb, sem)
        xs_ref[...] = buf_ref[...].astype(xs_ref.dtype)

    @pl.when(jnp.logical_not(used))
    def _():
        xs_ref[...] = jnp.zeros_like(xs_ref)


def _gather(h32, src, n_valid):
    m, d = h32.shape
    tmb = EXPERT_ROWS
    assert m >= tmb
    n_blocks = src.shape[0] // tmb
    return pl.pallas_call(
        _gather_kernel,
        grid=(n_blocks,),
        in_specs=[pl.BlockSpec(memory_space=pltpu.SMEM),
                  pl.BlockSpec((None, 1, tmb), lambda b: (b, 0, 0), memory_space=pltpu.SMEM),
                  pl.BlockSpec(memory_space=pl.ANY)],
        out_specs=pl.BlockSpec((tmb, d), lambda b: (b, 0)),
        out_shape=jax.ShapeDtypeStruct((n_blocks * tmb, d), BF16),
        scratch_shapes=[pltpu.VMEM((tmb, d), F32), pltpu.SemaphoreType.DMA(())],
        compiler_params=_params(("arbitrary",)),
        name="moe_gather",
    )(n_valid, src.reshape(n_blocks, 1, tmb), h32)


def _experts_kernel(be_ref, nv_ref, x_ref, wg_ref, wu_ref, wd_ref, y_ref, acc_ref):
    del be_ref
    f = pl.program_id(1)
    used = pl.program_id(0) < nv_ref[0]

    @pl.when(jnp.logical_not(used) & (f == 0))
    def _():
        y_ref[...] = jnp.zeros_like(y_ref)

    @pl.when(used)
    def _():
        h = x_ref[...]
        gate = _dot(h, wg_ref[...])
        act = (gate * _sigmoid(gate)) * _dot(h, wu_ref[...])
        part = _dot(act.astype(BF16), wd_ref[...])

        @pl.when(f == 0)
        def _():
            acc_ref[...] = part

        @pl.when(f > 0)
        def _():
            acc_ref[...] += part

        @pl.when(f == pl.num_programs(1) - 1)
        def _():
            y_ref[...] = acc_ref[...]


def _experts(xs, wg, wu, wd, blk_expert, n_valid, tf):
    n_rows, d = xs.shape
    ff = wg.shape[-1]
    n_f = ff // tf
    tmb = EXPERT_ROWS
    xmap = lambda b, f, be, nv: (jnp.minimum(b, nv[0] - 1), 0)
    fidx = lambda b, f, nv: jnp.where(b < nv[0], f, n_f - 1)
    grid_spec = pltpu.PrefetchScalarGridSpec(
        num_scalar_prefetch=2,
        grid=(n_rows // tmb, n_f),
        in_specs=[pl.BlockSpec((tmb, d), xmap),
                  pl.BlockSpec((None, d, tf), lambda b, f, be, nv: (be[b], 0, fidx(b, f, nv))),
                  pl.BlockSpec((None, d, tf), lambda b, f, be, nv: (be[b], 0, fidx(b, f, nv))),
                  pl.BlockSpec((None, tf, d), lambda b, f, be, nv: (be[b], fidx(b, f, nv), 0))],
        out_specs=pl.BlockSpec((tmb, d), lambda b, f, be, nv: (b, 0)),
        scratch_shapes=[pltpu.VMEM((tmb, d), F32)],
    )
    return pl.pallas_call(
        _experts_kernel,
        grid_spec=grid_spec,
        out_shape=jax.ShapeDtypeStruct((n_rows, d), F32),
        compiler_params=_params(("arbitrary", "arbitrary")),
        name="moe_experts",
    )(blk_expert, n_valid, xs, wg, wu, wd)


def _combine_kernel(pos_ref, x_ref, gate_ref, g_ref, y_ref, o_ref, ya_ref, yb_ref, sem):
    tb = x_ref.shape[0]

    def body(j, carry):
        _row_copy(y_ref, pos_ref[0, TOP_K * j], ya_ref, j, sem).start()
        _row_copy(y_ref, pos_ref[0, TOP_K * j + 1], yb_ref, j, sem).start()
        return carry

    lax.fori_loop(0, tb, body, 0)
    _rows_wait(y_ref, ya_ref, tb, sem)
    _rows_wait(y_ref, yb_ref, tb, sem)
    gate = gate_ref[...]
    x_new = x_ref[...] + (gate[:, 0:1] * ya_ref[...] + gate[:, 1:2] * yb_ref[...])
    o_ref[...] = _rms(x_new, g_ref[...]).astype(o_ref.dtype)


def _combine(x, y, pos, gates, g, out_dtype):
    m, d = x.shape
    tb = pos.shape[-1] // TOP_K
    return pl.pallas_call(
        _combine_kernel,
        grid=(m // tb,),
        in_specs=[pl.BlockSpec((None, 1, TOP_K * tb), lambda i: (i, 0, 0), memory_space=pltpu.SMEM),
                  pl.BlockSpec((tb, d), lambda i: (i, 0)),
                  pl.BlockSpec((tb, LANES), lambda i: (i, 0)),
                  pl.BlockSpec((1, d), lambda i: (0, 0)),
                  pl.BlockSpec(memory_space=pl.ANY)],
        out_specs=pl.BlockSpec((tb, d), lambda i: (i, 0)),
        out_shape=jax.ShapeDtypeStruct((m, d), out_dtype),
        scratch_shapes=[pltpu.VMEM((tb, d), F32), pltpu.VMEM((tb, d), F32),
                        pltpu.SemaphoreType.DMA(())],
        compiler_params=_params(("arbitrary",)),
        name="moe_combine",
    )(pos, x, gates, g.reshape(1, d), y)


def _moe(h, h32, x, w_router, wg, wu, wd, g, out_dtype):
    m, d = x.shape
    tmb = EXPERT_ROWS
    info, gates, cnt = _router(h, w_router)

    counts = cnt[0, :N_EXPERTS].astype(I32)
    padded = (counts + (tmb - 1)) // tmb * tmb
    ends = jnp.cumsum(padded)
    offsets = ends - padded
    n_blocks = -(-(TOP_K * m + N_EXPERTS * (tmb - 1)) // tmb)
    n_valid = ends[-1] // tmb
    starts = jnp.arange(n_blocks, dtype=I32) * tmb
    owner = jnp.sum((starts[:, None] >= ends[None, :]).astype(I32), axis=1)
    blk_expert = jnp.minimum(jnp.where(starts < ends[-1], owner, owner[n_valid - 1]), N_EXPERTS - 1)

    n_valid = n_valid.reshape(1)
    pos, src = _slots(info, offsets, counts, ends, n_blocks * tmb)
    xs = _gather(h32, src, n_valid)
    y = _experts(xs, wg, wu, wd, blk_expert, n_valid, 1024)
    return _combine(x, y, pos, gates, g, out_dtype)


def _pad_ff(w, axis, mult):
    ff = w.shape[axis]
    pad = (-ff) % mult
    if pad:
        widths = [(0, 0)] * w.ndim
        widths[axis] = (0, pad)
        w = jnp.pad(w, widths)
    return w


def kernel(x_prompt, x_sample, state_hgrn, state_rglru, state_conv, hgrn_lb_logits, norm_mix, w_in,
           hgrn_gnorm, rg_conv_w, rg_conv_b, rg_wa, rg_ba, rg_wx, rg_bx, rg_a_param, w_br_a, w_br_b,
           w_out, norm_ffn, ffn_w_gate, ffn_w_up, ffn_w_down, moe_router, moe_w_gate, moe_w_up,
           moe_w_down, norm_final):
    depth = w_in.shape[0]
    batch, seq, d = x_prompt.shape
    n_seq = x_sample.shape[0]
    m_prompt = batch * seq
    m = m_prompt + n_seq

    sm = jax.nn.softmax(hgrn_lb_logits.astype(F32), axis=0)
    lbs = jnp.maximum(jnp.cumsum(sm, axis=0) - sm[0:1], 0.0)
    log_lb = jnp.log(lbs)
    log1m_lb = jnp.log1p(-lbs)

    x = jnp.concatenate([x_prompt.reshape(m_prompt, d), x_sample.reshape(n_seq, d)], axis=0)
    h = _rmsnorm(x, norm_mix[0], BF16)

    chunk = 256 if seq % 256 == 0 else seq
    hg_p, rg_p, cv_p, hg_s, rg_s, cv_s = [], [], [], [], [], []
    for l in range(depth):
        proj = _in_proj(h, w_in, l)
        vrow = lambda a: a[l].reshape(1, -1)
        o_all, s_prompt = _hgrn_prompt(proj, vrow(log_lb), vrow(log1m_lb), vrow(hgrn_gnorm),
                                       batch, seq, m, chunk)
        o_all, s_sample = _hgrn_sample(proj, vrow(log_lb), vrow(log1m_lb), vrow(hgrn_gnorm),
                                       state_hgrn[l], o_all, m_prompt)
        rg_args = (rg_conv_w[l], vrow(rg_conv_b), rg_wa[l], rg_ba[l].reshape(RG_BLOCKS, 1, RG_BW),
                   rg_wx[l], rg_bx[l].reshape(RG_BLOCKS, 1, RG_BW), vrow(rg_a_param))
        y_all, h_prompt = _rg_prompt(proj, *rg_args, batch, seq, m, chunk)
        y_all, h_sample = _rg_sample(proj, state_conv[l], state_rglru[l], *rg_args, y_all, m_prompt)

        dense = l % 2 == 0
        merged = _merge_out(o_all, y_all, proj, x, w_br_a[l].astype(BF16), w_br_b[l].astype(BF16),
                            w_out[l].astype(BF16), norm_ffn[l], with_h32=not dense)
        x, h2 = merged[0], merged[1]

        last = l == depth - 1
        g_next = norm_final if last else norm_mix[l + 1]
        out_dtype = F32 if last else BF16
        j = l // 2
        if dense:
            wg = _pad_ff(ffn_w_gate[j], 1, 512).astype(BF16)
            wu = _pad_ff(ffn_w_up[j], 1, 512).astype(BF16)
            wd = _pad_ff(ffn_w_down[j], 0, 512).astype(BF16)
            h, x = _ffn(h2, x, wg, wu, wd, g_next, out_dtype, 512)
        else:
            assert last
            h = _moe(h2, merged[2], x, moe_router[j], moe_w_gate[j].astype(BF16),
                     moe_w_up[j].astype(BF16), moe_w_down[j].astype(BF16), g_next, out_dtype)

        xr = proj[:, 2 * HG_F + 2 * HG_I:2 * HG_F + 2 * HG_I + RG_WIDTH]
        hg_p.append(s_prompt)
        rg_p.append(h_prompt.reshape(batch, RG_WIDTH))
        cv_p.append(xr[:m_prompt].reshape(batch, seq, RG_WIDTH)[:, seq - (CONV_W - 1):])
        hg_s.append(s_sample)
        rg_s.append(h_sample)
        cv_s.append(jnp.concatenate([state_conv[l][:, 1:], xr[m_prompt:, None, :]], axis=1))

    y_prompt = h[:m_prompt].reshape(batch, seq, d)
    y_sample = h[m_prompt:].reshape(n_seq, 1, d)
    return (y_prompt, y_sample, jnp.stack(hg_p), jnp.stack(rg_p), jnp.stack(cv_p),
            jnp.stack(hg_s), jnp.stack(rg_s), jnp.stack(cv_s))
```

```python
import functools

import jax
import jax.numpy as jnp
from jax import lax
from jax.experimental import pallas as pl
from jax.experimental.pallas import tpu as pltpu

D_MODEL = 2048
HG_HEADS = 8
HG_DK = 128
HG_DV = 128
HG_F = HG_HEADS * HG_DK
HG_I = HG_HEADS * HG_DV
RG_WIDTH = 1024
RG_BLOCKS = 8
RG_BW = RG_WIDTH // RG_BLOCKS
CONV_W = 4
RG_C = 8.0
IN_COLS = 2 * HG_F + 2 * HG_I + 2 * RG_WIDTH + 2 * D_MODEL
N_EXPERTS = 8
TOP_K = 2
EPS = 1e-6

LANES = 128
SUBLANES = 8
BF16_SUBLANES = 16
VMEM_LIMIT = 56 * 1024 * 1024
EXPERT_ROWS = 512

BF16 = jnp.bfloat16
F32 = jnp.float32
I32 = jnp.int32


def _params(sem, vmem=VMEM_LIMIT):
    return pltpu.CompilerParams(dimension_semantics=sem, vmem_limit_bytes=vmem)


def _row_tile(m, target):
    best = None
    for t in range(BF16_SUBLANES, min(m, target) + 1, BF16_SUBLANES):
        if m % t == 0:
            best = t
    assert best is not None, (m, target)
    return best


def _dot(a, b):
    return jnp.dot(a, b, preferred_element_type=F32)


def _dot_nt(a, b):
    return lax.dot_general(a, b, (((1,), (1,)), ((), ())), preferred_element_type=F32)


def _dot_tn(a, b):
    return lax.dot_general(a, b, (((0,), (0,)), ((), ())), preferred_element_type=F32)


def _sigmoid(x):
    return 1.0 / (1.0 + jnp.exp(-x))


def _log_sigmoid(x):
    return jnp.minimum(x, 0.0) - jnp.log1p(jnp.exp(-jnp.abs(x)))


def _rms(x, g):
    ms = jnp.mean(x * x, axis=-1, keepdims=True)
    return x * lax.rsqrt(ms + EPS) * g


def _rmsnorm_kernel(x_ref, g_ref, o_ref):
    o_ref[...] = _rms(x_ref[...], g_ref[...]).astype(o_ref.dtype)


def _rmsnorm(x, g, out_dtype):
    m, d = x.shape
    tm = _row_tile(m, 640)
    return pl.pallas_call(
        _rmsnorm_kernel,
        grid=(m // tm,),
        in_specs=[pl.BlockSpec((tm, d), lambda i: (i, 0)),
                  pl.BlockSpec((1, d), lambda i: (0, 0))],
        out_specs=pl.BlockSpec((tm, d), lambda i: (i, 0)),
        out_shape=jax.ShapeDtypeStruct((m, d), out_dtype),
        compiler_params=_params(("parallel",)),
        name="rmsnorm",
    )(x, g.reshape(1, d))


def _in_proj_kernel(x_ref, w_ref, o_ref, wbf_ref):
    @pl.when(pl.program_id(1) == 0)
    def _():
        wbf_ref[...] = w_ref[...].astype(BF16)

    o_ref[...] = _dot(x_ref[...], wbf_ref[...])


def _in_proj(h, w_in, layer):
    m, d = h.shape
    n = w_in.shape[-1]
    tm = _row_tile(m, 1664)
    tn = 1024
    return pl.pallas_call(
        _in_proj_kernel,
        grid=(n // tn, m // tm),
        in_specs=[pl.BlockSpec((tm, d), lambda j, i: (i, 0)),
                  pl.BlockSpec((None, d, tn), lambda j, i: (layer, 0, j))],
        out_specs=pl.BlockSpec((tm, tn), lambda j, i: (i, j)),
        out_shape=jax.ShapeDtypeStruct((m, n), F32),
        scratch_shapes=[pltpu.VMEM((d, tn), BF16)],
        compiler_params=_params(("arbitrary", "arbitrary")),
        name="in_proj",
    )(h, w_in)


def _hgrn_gates(fz, lb, one_m_lb):
    u = jnp.exp(-jnp.abs(fz))
    r = 1.0 / (1.0 + u)
    nonneg = fz >= 0.0
    sig = jnp.where(nonneg, r, u * r)
    sig_neg = jnp.where(nonneg, u * r, r)
    f = lb + one_m_lb * sig
    return jnp.log(f), one_m_lb * sig_neg, f


def _hgrn_out(o, gate, gnorm):
    return _rms(o, gnorm) * (gate * _sigmoid(gate))


def _hgrn_prompt_kernel(q_ref, f_ref, i_ref, g_ref, lb_ref, omlb_ref, gn_ref, o_init_ref,
                        o_ref, s_out_ref, s_ref, *, chunk):
    del o_init_ref
    C = chunk
    c = pl.program_id(2)

    @pl.when(c == 0)
    def _():
        s_ref[...] = jnp.zeros_like(s_ref)

    q = q_ref[...]
    v = i_ref[...].astype(BF16)
    g, k, _ = _hgrn_gates(f_ref[...], lb_ref[...], omlb_ref[...])

    row = lax.broadcasted_iota(I32, (C, HG_DK), 0)
    rc_xor = lax.broadcasted_iota(I32, (C, C), 0) ^ lax.broadcasted_iota(I32, (C, C), 1)

    a_mat = jnp.where(rc_xor == 0, _dot_nt(q.astype(BF16), k.astype(BF16)), 0.0)

    pre, suf, tot = g, jnp.zeros_like(g), g
    w = 1
    while w < C:
        upper = (row & w) != 0
        e = jnp.exp(jnp.where(upper, pre, suf))
        qw = jnp.where(upper, q * e, 0.0).astype(BF16)
        kw = jnp.where(upper, 0.0, k * e).astype(BF16)
        a_mat = a_mat + jnp.where(rc_xor < 2 * w, _dot_nt(qw, kw), 0.0)
        up = pltpu.roll(tot, w, 0)
        dn = pltpu.roll(tot, C - w, 0)
        pre = pre + jnp.where(upper, up, 0.0)
        suf = suf + jnp.where(upper, 0.0, dn)
        tot = tot + jnp.where(upper, up, dn)
        w *= 2

    s_old = s_ref[...]
    o = _dot(a_mat.astype(BF16), v) + _dot((q * jnp.exp(pre)).astype(BF16), s_old.astype(BF16))
    decay_rows = jnp.exp(jnp.broadcast_to(tot[0:1, :], (HG_DK, HG_DK))).T
    s_new = decay_rows * s_old + _dot_tn((k * jnp.exp(suf)).astype(BF16), v)
    s_ref[...] = s_new

    o_ref[...] = _hgrn_out(o, g_ref[...], gn_ref[...]).astype(o_ref.dtype)

    @pl.when(c == pl.num_programs(2) - 1)
    def _():
        s_out_ref[...] = s_new


def _hgrn_prompt(proj, lb, one_m_lb, gnorm, batch, seq, m_total, chunk):
    n_chunks = seq // chunk
    tok = lambda off: pl.BlockSpec((chunk, LANES), lambda b, h, c: (b * n_chunks + c, off + h))
    vec = pl.BlockSpec((1, LANES), lambda b, h, c: (0, h))
    return pl.pallas_call(
        functools.partial(_hgrn_prompt_kernel, chunk=chunk),
        grid=(batch, HG_HEADS, n_chunks),
        in_specs=[tok(0), tok(HG_HEADS), tok(2 * HG_HEADS), tok(3 * HG_HEADS), vec, vec, vec,
                  pl.BlockSpec(memory_space=pl.ANY)],
        input_output_aliases={7: 0},
        out_specs=[pl.BlockSpec((chunk, LANES), lambda b, h, c: (b * n_chunks + c, h)),
                   pl.BlockSpec((None, None, HG_DK, HG_DV), lambda b, h, c: (b, h, 0, 0))],
        out_shape=[jax.ShapeDtypeStruct((m_total, HG_I), BF16),
                   jax.ShapeDtypeStruct((batch, HG_HEADS, HG_DK, HG_DV), F32)],
        scratch_shapes=[pltpu.VMEM((HG_DK, HG_DV), F32)],
        compiler_params=_params(("parallel", "parallel", "arbitrary")),
        name="hgrn_prompt",
    )(proj, proj, proj, proj, lb, one_m_lb, gnorm, jnp.zeros((m_total, HG_I), BF16))


HGRN_SAMPLE_ROWS = 16


def _hgrn_sample_kernel(q_ref, f_ref, i_ref, g_ref, lb_ref, omlb_ref, gn_ref, s_ref, o_prev_ref,
                        o_ref, s_out_ref):
    del o_prev_ref
    n = HGRN_SAMPLE_ROWS
    q = q_ref[...]
    vi = i_ref[...]
    _, k, f = _hgrn_gates(f_ref[...], lb_ref[...], omlb_ref[...])

    def columns(x):
        pad = jnp.zeros((LANES - n, HG_DK), F32)
        return jnp.concatenate([x, pad], axis=0).T

    f_t, k_t, q_t = columns(f), columns(k), columns(q)
    seq_id = lax.broadcasted_iota(I32, (n, HG_DV), 0)
    o = jnp.zeros((n, HG_DV), F32)
    for j in range(n):
        bc = lambda xt: jnp.broadcast_to(xt[:, j:j + 1], (HG_DK, HG_DV))
        s_new = bc(f_t) * s_ref[j] + bc(k_t) * vi[j:j + 1, :]
        s_out_ref[j] = s_new
        o_j = jnp.sum(bc(q_t) * s_new, axis=0, keepdims=True)
        o = jnp.where(seq_id == j, o_j, o)
    o_ref[...] = _hgrn_out(o, g_ref[...], gn_ref[...]).astype(o_ref.dtype)


def _hgrn_sample(proj, lb, one_m_lb, gnorm, states, layer, o_all, row0):
    n_seq = states.shape[1]
    n = HGRN_SAMPLE_ROWS
    assert n_seq % n == 0 and row0 % n == 0
    blk0 = row0 // n
    tok = lambda off: pl.BlockSpec((n, LANES), lambda j, h: (blk0 + j, off + h))
    vec = pl.BlockSpec((1, LANES), lambda j, h: (0, h))
    st_in = pl.BlockSpec((None, n, None, HG_DK, HG_DV), lambda j, h: (layer, j, h, 0, 0))
    st_out = pl.BlockSpec((n, None, HG_DK, HG_DV), lambda j, h: (j, h, 0, 0))
    return pl.pallas_call(
        _hgrn_sample_kernel,
        grid=(n_seq // n, HG_HEADS),
        in_specs=[tok(0), tok(HG_HEADS), tok(2 * HG_HEADS), tok(3 * HG_HEADS), vec, vec, vec, st_in,
                  pl.BlockSpec(memory_space=pl.ANY)],
        out_specs=[pl.BlockSpec((n, LANES), lambda j, h: (blk0 + j, h)), st_out],
        out_shape=[jax.ShapeDtypeStruct(o_all.shape, o_all.dtype),
                   jax.ShapeDtypeStruct(states.shape[1:], F32)],
        input_output_aliases={8: 0},
        compiler_params=_params(("parallel", "parallel")),
        name="hgrn_sample",
    )(proj, proj, proj, proj, lb, one_m_lb, gnorm, states, o_all)


def _gelu_tanh(x):
    return 0.5 * x * (1.0 + jnp.tanh(0.7978845608028654 * (x + 0.044715 * (x * x * x))))


def _rg_gates(conv, wa_ref, ba_ref, wx_ref, bx_ref, ap_ref):
    cb = conv.astype(BF16)
    r = _sigmoid(_dot(cb, wa_ref[...].astype(BF16)) + ba_ref[...])
    ig = _sigmoid(_dot(cb, wx_ref[...].astype(BF16)) + bx_ref[...])
    log_a = RG_C * r * _log_sigmoid(ap_ref[...])
    a = jnp.exp(log_a)
    t = jnp.tanh(log_a)
    b = jnp.sqrt(-2.0 * t / (1.0 - t)) * ig * conv
    return a, b


def _rg_prompt_kernel(x_ref, gate_ref, cw_ref, cb_ref, wa_ref, ba_ref, wx_ref, bx_ref, ap_ref,
                      y_init_ref, y_ref, h_out_ref, xp_ref, h_ref, *, rows):
    del y_init_ref
    T = rows
    t = pl.program_id(2)

    @pl.when(t == 0)
    def _():
        xp_ref[0:8, :] = jnp.zeros((8, RG_BW), F32)
        h_ref[...] = jnp.zeros_like(h_ref)

    xp_ref[8:8 + T, :] = x_ref[...]
    conv = cb_ref[...]
    for j in range(CONV_W):
        conv = conv + xp_ref[5 + j:5 + j + T, :] * cw_ref[j:j + 1, :]
    xp_ref[0:8, :] = xp_ref[T:T + 8, :]

    a, b = _rg_gates(conv, wa_ref, ba_ref, wx_ref, bx_ref, ap_ref)

    row = lax.broadcasted_iota(I32, (T, RG_BW), 0)
    d = 1
    while d < T:
        ok = row >= d
        b = jnp.where(ok, a * pltpu.roll(b, d, 0) + b, b)
        a = jnp.where(ok, a * pltpu.roll(a, d, 0), a)
        d *= 2
    h = a * h_ref[0:1, :] + b
    h_last = h[T - 1:T, :]
    h_ref[...] = jnp.broadcast_to(h_last, h_ref.shape)

    y_ref[...] = (h * _gelu_tanh(gate_ref[...])).astype(y_ref.dtype)

    @pl.when(t == pl.num_programs(2) - 1)
    def _():
        h_out_ref[...] = h_last


def _rg_prompt(proj, cw, cb, wa, ba, wx, bx, ap, batch, seq, m_total, rows):
    n_t = seq // rows
    xoff = (2 * HG_F + 2 * HG_I) // LANES
    goff = xoff + RG_BLOCKS
    tok = lambda off: pl.BlockSpec((rows, LANES), lambda b, n, t: (b * n_t + t, off + n))
    vec = pl.BlockSpec((1, LANES), lambda b, n, t: (0, n))
    blkw = pl.BlockSpec((None, RG_BW, RG_BW), lambda b, n, t: (n, 0, 0))
    blkb = pl.BlockSpec((None, 1, RG_BW), lambda b, n, t: (n, 0, 0))
    return pl.pallas_call(
        functools.partial(_rg_prompt_kernel, rows=rows),
        grid=(batch, RG_BLOCKS, n_t),
        in_specs=[tok(xoff), tok(goff),
                  pl.BlockSpec((CONV_W, LANES), lambda b, n, t: (0, n)), vec,
                  blkw, blkb, blkw, blkb, vec, pl.BlockSpec(memory_space=pl.ANY)],
        input_output_aliases={9: 0},
        out_specs=[pl.BlockSpec((rows, LANES), lambda b, n, t: (b * n_t + t, n)),
                   pl.BlockSpec((None, 1, LANES), lambda b, n, t: (b, 0, n))],
        out_shape=[jax.ShapeDtypeStruct((m_total, RG_WIDTH), BF16),
                   jax.ShapeDtypeStruct((batch, 1, RG_WIDTH), F32)],
        scratch_shapes=[pltpu.VMEM((rows + 8, RG_BW), F32), pltpu.VMEM((8, RG_BW), F32)],
        compiler_params=_params(("parallel", "parallel", "arbitrary")),
        name="rg_prompt",
    )(proj, proj, cw, cb, wa, ba, wx, bx, ap, jnp.zeros((m_total, RG_WIDTH), BF16))


def _rg_sample_kernel(x_ref, gate_ref, b0_ref, b1_ref, b2_ref, h0_ref, cw_ref, cb_ref,
                      wa_ref, ba_ref, wx_ref, bx_ref, ap_ref, y_prev_ref, y_ref, h_out_ref):
    del y_prev_ref
    conv = cb_ref[...]
    for j, r in enumerate((b0_ref, b1_ref, b2_ref, x_ref)):
        conv = conv + r[...] * cw_ref[j:j + 1, :]
    a, b = _rg_gates(conv, wa_ref, ba_ref, wx_ref, bx_ref, ap_ref)
    h = a * h0_ref[...] + b
    h_out_ref[...] = h
    y_ref[...] = (h * _gelu_tanh(gate_ref[...])).astype(y_ref.dtype)


def _rg_sample(proj, conv_state, h0, cw, cb, wa, ba, wx, bx, ap, y_all, row0):
    n_seq = h0.shape[0]
    assert row0 % n_seq == 0 and n_seq % BF16_SUBLANES == 0
    blk0 = row0 // n_seq
    xoff = (2 * HG_F + 2 * HG_I) // LANES
    goff = xoff + RG_BLOCKS
    tok = lambda off: pl.BlockSpec((n_seq, LANES), lambda n: (blk0, off + n))
    buf = lambda j: pl.BlockSpec((n_seq, LANES), lambda n: (0, j * RG_BLOCKS + n))
    vec = pl.BlockSpec((1, LANES), lambda n: (0, n))
    blkw = pl.BlockSpec((None, RG_BW, RG_BW), lambda n: (n, 0, 0))
    blkb = pl.BlockSpec((None, 1, RG_BW), lambda n: (n, 0, 0))
    flat_state = conv_state.reshape(n_seq, (CONV_W - 1) * RG_WIDTH)
    return pl.pallas_call(
        _rg_sample_kernel,
        grid=(RG_BLOCKS,),
        in_specs=[tok(xoff), tok(goff), buf(0), buf(1), buf(2),
                  pl.BlockSpec((n_seq, LANES), lambda n: (0, n)),
                  pl.BlockSpec((CONV_W, LANES), lambda n: (0, n)), vec,
                  blkw, blkb, blkw, blkb, vec,
                  pl.BlockSpec(memory_space=pl.ANY)],
        out_specs=[pl.BlockSpec((n_seq, LANES), lambda n: (blk0, n)),
                   pl.BlockSpec((n_seq, LANES), lambda n: (0, n))],
        out_shape=[jax.ShapeDtypeStruct(y_all.shape, y_all.dtype),
                   jax.ShapeDtypeStruct((n_seq, RG_WIDTH), F32)],
        input_output_aliases={13: 0},
        compiler_params=_params(("parallel",)),
        name="rg_sample",
    )(proj, proj, flat_state, flat_state, flat_state, h0, cw, cb, wa, ba, wx, bx, ap, y_all)


def _merge_out_kernel(o_ref, y_ref, ma_ref, mb_ref, x_ref, wa_ref, wb_ref, wo_ref, g_ref,
                      xo_ref, h_ref, *maybe_h32_ref):
    merged = (_sigmoid(ma_ref[...]) * _dot(o_ref[...], wa_ref[...])
              + _sigmoid(mb_ref[...]) * _dot(y_ref[...], wb_ref[...]))
    x_new = x_ref[...] + _dot(merged.astype(BF16), wo_ref[...])
    xo_ref[...] = x_new
    h = _rms(x_new, g_ref[...])
    h_ref[...] = h.astype(h_ref.dtype)
    for r in maybe_h32_ref:
        r[...] = h


def _merge_out(o, y, proj, x, w_a, w_b, w_o, g, with_h32):
    m, d = x.shape
    tm = _row_tile(m, 320)
    moff = (2 * HG_F + 2 * HG_I + 2 * RG_WIDTH) // d
    row = lambda width, col: pl.BlockSpec((tm, width), lambda i: (i, col))
    res = lambda shape: pl.BlockSpec(shape, lambda i: (0, 0), pipeline_mode=pl.Buffered(1))
    n_out = 3 if with_h32 else 2
    return pl.pallas_call(
        _merge_out_kernel,
        grid=(m // tm,),
        in_specs=[row(HG_I, 0), row(RG_WIDTH, 0), row(d, moff), row(d, moff + 1), row(d, 0),
                  res(w_a.shape), res(w_b.shape), res(w_o.shape), res((1, d))],
        out_specs=[row(d, 0)] * n_out,
        out_shape=[jax.ShapeDtypeStruct((m, d), F32), jax.ShapeDtypeStruct((m, d), BF16),
                   jax.ShapeDtypeStruct((m, d), F32)][:n_out],
        compiler_params=_params(("parallel",)),
        name="merge_out",
    )(o, y, proj, proj, x, w_a, w_b, w_o, g.reshape(1, d))


def _ffn_kernel(h_ref, x_ref, wg_ref, wu_ref, wd_ref, g_ref, o_ref, xo_ref, acc_ref):
    f = pl.program_id(1)

    @pl.when(f == 0)
    def _():
        acc_ref[...] = jnp.zeros_like(acc_ref)

    h = h_ref[...]
    gate = _dot(h, wg_ref[...])
    act = (gate * _sigmoid(gate)) * _dot(h, wu_ref[...])
    acc_ref[...] += _dot(act.astype(BF16), wd_ref[...])

    @pl.when(f == pl.num_programs(1) - 1)
    def _():
        x_new = x_ref[...] + acc_ref[...]
        xo_ref[...] = x_new
        o_ref[...] = _rms(x_new, g_ref[...]).astype(o_ref.dtype)


def _ffn(h, x, wg, wu, wd, g, out_dtype, tf):
    m, d = x.shape
    ff = wg.shape[1]
    tm = _row_tile(m, 416)
    return pl.pallas_call(
        _ffn_kernel,
        grid=(m // tm, ff // tf),
        in_specs=[pl.BlockSpec((tm, d), lambda i, f: (i, 0)),
                  pl.BlockSpec((tm, d), lambda i, f: (i, 0)),
                  pl.BlockSpec((d, tf), lambda i, f: (0, f)),
                  pl.BlockSpec((d, tf), lambda i, f: (0, f)),
                  pl.BlockSpec((tf, d), lambda i, f: (f, 0)),
                  pl.BlockSpec((1, d), lambda i, f: (0, 0))],
        out_specs=[pl.BlockSpec((tm, d), lambda i, f: (i, 0)),
                   pl.BlockSpec((tm, d), lambda i, f: (i, 0))],
        out_shape=[jax.ShapeDtypeStruct((m, d), out_dtype), jax.ShapeDtypeStruct((m, d), F32)],
        scratch_shapes=[pltpu.VMEM((tm, d), F32)],
        compiler_params=_params(("parallel", "arbitrary")),
        name="ffn_dense",
    )(h, x, wg, wu, wd, g.reshape(1, d))


def _router_kernel(h_ref, w_ref, info_ref, gate_ref, cnt_ref, carry_ref):
    @pl.when(pl.program_id(0) == 0)
    def _():
        carry_ref[...] = jnp.zeros_like(carry_ref)

    logits = _dot(h_ref[...], w_ref[...])
    tm = logits.shape[0]
    lane = lax.broadcasted_iota(I32, logits.shape, 1)
    neg = jnp.float32(-jnp.inf)
    logits = jnp.where(lane < N_EXPERTS, logits, neg)
    m1 = jnp.max(logits, axis=-1, keepdims=True)
    i1 = jnp.min(jnp.where(logits == m1, lane, LANES), axis=-1, keepdims=True)
    rest = jnp.where(lane == i1, neg, logits)
    m2 = jnp.max(rest, axis=-1, keepdims=True)
    i2 = jnp.min(jnp.where(rest == m2, lane, LANES), axis=-1, keepdims=True)
    e = jnp.exp(m2 - m1)
    g1 = 1.0 / (1.0 + e)
    g2 = e / (1.0 + e)
    gate_ref[...] = jnp.where(lane == 0, g1, jnp.where(lane == 1, g2, 0.0))

    chosen = jnp.where((lane == i1) | (lane == i2), 1.0, 0.0)
    lower = jnp.where(lax.broadcasted_iota(I32, (tm, tm), 0) > lax.broadcasted_iota(I32, (tm, tm), 1),
                      1.0, 0.0).astype(BF16)
    before = _dot(lower, chosen.astype(BF16)) + carry_ref[0:1, :]
    r1 = jnp.sum(jnp.where(lane == i1, before, 0.0), axis=-1, keepdims=True).astype(I32)
    r2 = jnp.sum(jnp.where(lane == i2, before, 0.0), axis=-1, keepdims=True).astype(I32)
    info_ref[...] = jnp.where(lane == 0, i1, jnp.where(lane == 1, i2,
                              jnp.where(lane == 2, r1, jnp.where(lane == 3, r2, 0))))
    total = carry_ref[0:1, :] + jnp.sum(chosen, axis=0, keepdims=True)
    carry_ref[...] = jnp.broadcast_to(total, carry_ref.shape)
    cnt_ref[...] = jnp.broadcast_to(total, cnt_ref.shape)


def _router(h, w_router):
    m, d = h.shape
    tm = _row_tile(m, 640)
    w = jnp.pad(w_router, ((0, 0), (0, LANES - N_EXPERTS))).astype(BF16)
    return pl.pallas_call(
        _router_kernel,
        grid=(m // tm,),
        in_specs=[pl.BlockSpec((tm, d), lambda i: (i, 0)),
                  pl.BlockSpec((d, LANES), lambda i: (0, 0))],
        out_specs=[pl.BlockSpec((tm, LANES), lambda i: (i, 0)),
                   pl.BlockSpec((tm, LANES), lambda i: (i, 0)),
                   pl.BlockSpec((SUBLANES, LANES), lambda i: (0, 0))],
        out_shape=[jax.ShapeDtypeStruct((m, LANES), I32),
                   jax.ShapeDtypeStruct((m, LANES), F32),
                   jax.ShapeDtypeStruct((SUBLANES, LANES), F32)],
        scratch_shapes=[pltpu.VMEM((SUBLANES, LANES), F32)],
        compiler_params=_params(("arbitrary",)),
        name="router",
    )(h, w)


def _row_copy(src_ref, src_row, dst_ref, dst_row, sem):
    return pltpu.make_async_copy(src_ref.at[pl.ds(src_row, 1), :], dst_ref.at[pl.ds(dst_row, 1), :], sem)


def _rows_wait(src_ref, dst_ref, n_rows, sem):
    pltpu.make_async_copy(src_ref.at[pl.ds(0, n_rows), :], dst_ref.at[pl.ds(0, n_rows), :], sem).wait()


def _slots_kernel(off_ref, cnt_ref, end_ref, tok_ref, pos_ref, src_ref):
    tb = pos_ref.shape[-1] // TOP_K
    base = pl.program_id(0) * tb

    @pl.when(pl.program_id(0) == 0)
    def _():
        def clear(lo, hi):
            def body(p, carry):
                src_ref[p] = 0
                return carry
            lax.fori_loop(lo, hi, body, 0)

        for e in range(N_EXPERTS):
            clear(off_ref[e] + cnt_ref[e], end_ref[e])
        clear(end_ref[N_EXPERTS - 1], src_ref.shape[0])

    def body(j, carry):
        for k in range(TOP_K):
            p = off_ref[tok_ref[0, 4 * j + k]] + tok_ref[0, 4 * j + 2 + k]
            pos_ref[0, TOP_K * j + k] = p
            src_ref[p] = base + j
        return carry

    lax.fori_loop(0, tb, body, 0)


def _slots(tok_info, offsets, counts, ends, n_rows):
    m = tok_info.shape[0]
    tb = _row_tile(m, 512)
    tok = tok_info[:, :4].reshape(m // tb, 1, 4 * tb)
    smem = pl.BlockSpec(memory_space=pltpu.SMEM)
    return pl.pallas_call(
        _slots_kernel,
        grid=(m // tb,),
        in_specs=[smem, smem, smem,
                  pl.BlockSpec((None, 1, 4 * tb), lambda i: (i, 0, 0), memory_space=pltpu.SMEM)],
        out_specs=[pl.BlockSpec((None, 1, TOP_K * tb), lambda i: (i, 0, 0), memory_space=pltpu.SMEM),
                   smem],
        out_shape=[jax.ShapeDtypeStruct((m // tb, 1, TOP_K * tb), I32),
                   jax.ShapeDtypeStruct((n_rows,), I32)],
        compiler_params=_params(("arbitrary",)),
        name="moe_slots",
    )(offsets, counts, ends, tok)


def _gather_kernel(nv_ref, src_ref, h_ref, xs_ref, buf_ref, sem):
    tmb = buf_ref.shape[0]
    used = pl.program_id(0) < nv_ref[0]

    @pl.when(used)
    def _():
        def body(r, carry):
            _row_copy(h_ref, src_ref[0, r], buf_ref, r, sem).start()
            return carry

        lax.fori_loop(0, tmb, body, 0, unroll=8)
        _rows_wait(h_ref, buf_ref, tmb, sem)
        xs_ref[...] = buf_ref[...].astype(xs_ref.dtype)

    @pl.when(jnp.logical_not(used))
    def _():
        xs_ref[...] = jnp.zeros_like(xs_ref)


def _gather(h32, src, n_valid):
    m, d = h32.shape
    tmb = EXPERT_ROWS
    assert m >= tmb
    n_blocks = src.shape[0] // tmb
    return pl.pallas_call(
        _gather_kernel,
        grid=(n_blocks,),
        in_specs=[pl.BlockSpec(memory_space=pltpu.SMEM),
                  pl.BlockSpec((None, 1, tmb), lambda b: (b, 0, 0), memory_space=pltpu.SMEM),
                  pl.BlockSpec(memory_space=pl.ANY)],
        out_specs=pl.BlockSpec((tmb, d), lambda b: (b, 0)),
        out_shape=jax.ShapeDtypeStruct((n_blocks * tmb, d), BF16),
        scratch_shapes=[pltpu.VMEM((tmb, d), F32), pltpu.SemaphoreType.DMA(())],
        compiler_params=_params(("arbitrary",)),
        name="moe_gather",
    )(n_valid, src.reshape(n_blocks, 1, tmb), h32)


def _experts_kernel(be_ref, nv_ref, x_ref, wg_ref, wu_ref, wd_ref, y_ref):
    del be_ref

    @pl.when(pl.program_id(1) == 0)
    def _():
        y_ref[...] = jnp.zeros_like(y_ref)

    @pl.when(pl.program_id(0) < nv_ref[0])
    def _():
        h = x_ref[...]
        gate = _dot(h, wg_ref[...])
        act = (gate * _sigmoid(gate)) * _dot(h, wu_ref[...])
        y_ref[...] += _dot(act.astype(BF16), wd_ref[...])


def _experts(xs, wg, wu, wd, blk_expert, n_valid, tf):
    n_rows, d = xs.shape
    ff = wg.shape[-1]
    n_f = ff // tf
    tmb = EXPERT_ROWS
    xmap = lambda b, f, be, nv: (jnp.minimum(b, nv[0] - 1), 0)
    fidx = lambda b, f, nv: jnp.where(b < nv[0], f, n_f - 1)
    grid_spec = pltpu.PrefetchScalarGridSpec(
        num_scalar_prefetch=2,
        grid=(n_rows // tmb, n_f),
        in_specs=[pl.BlockSpec((tmb, d), xmap),
                  pl.BlockSpec((None, d, tf), lambda b, f, be, nv: (be[b], 0, fidx(b, f, nv))),
                  pl.BlockSpec((None, d, tf), lambda b, f, be, nv: (be[b], 0, fidx(b, f, nv))),
                  pl.BlockSpec((None, tf, d), lambda b, f, be, nv: (be[b], fidx(b, f, nv), 0))],
        out_specs=pl.BlockSpec((tmb, d), lambda b, f, be, nv: (b, 0)),
    )
    return pl.pallas_call(
        _experts_kernel,
        grid_spec=grid_spec,
        out_shape=jax.ShapeDtypeStruct((n_rows, d), F32),
        compiler_params=_params(("arbitrary", "arbitrary")),
        name="moe_experts",
    )(blk_expert, n_valid, xs, wg, wu, wd)


def _combine_kernel(pos_ref, x_ref, gate_ref, g_ref, y_ref, o_ref, ya_ref, yb_ref, sem):
    tb = x_ref.shape[0]

    def body(j, carry):
        _row_copy(y_ref, pos_ref[0, TOP_K * j], ya_ref, j, sem).start()
        _row_copy(y_ref, pos_ref[0, TOP_K * j + 1], yb_ref, j, sem).start()
        return carry

    lax.fori_loop(0, tb, body, 0)
    _rows_wait(y_ref, ya_ref, tb, sem)
    _rows_wait(y_ref, yb_ref, tb, sem)
    gate = gate_ref[...]
    x_new = x_ref[...] + (gate[:, 0:1] * ya_ref[...] + gate[:, 1:2] * yb_ref[...])
    o_ref[...] = _rms(x_new, g_ref[...]).astype(o_ref.dtype)


def _combine(x, y, pos, gates, g, out_dtype, row0, n_rows):
    d = x.shape[1]
    tb = _row_tile(n_rows, 512)
    assert row0 % tb == 0
    blk0 = row0 // tb
    pos = pos.reshape(-1)[TOP_K * row0:TOP_K * (row0 + n_rows)].reshape(n_rows // tb, 1, TOP_K * tb)
    return pl.pallas_call(
        _combine_kernel,
        grid=(n_rows // tb,),
        in_specs=[pl.BlockSpec((None, 1, TOP_K * tb), lambda i: (i, 0, 0), memory_space=pltpu.SMEM),
                  pl.BlockSpec((tb, d), lambda i: (blk0 + i, 0)),
                  pl.BlockSpec((tb, LANES), lambda i: (blk0 + i, 0)),
                  pl.BlockSpec((1, d), lambda i: (0, 0)),
                  pl.BlockSpec(memory_space=pl.ANY)],
        out_specs=pl.BlockSpec((tb, d), lambda i: (i, 0)),
        out_shape=jax.ShapeDtypeStruct((n_rows, d), out_dtype),
        scratch_shapes=[pltpu.VMEM((tb, d), F32), pltpu.VMEM((tb, d), F32),
                        pltpu.SemaphoreType.DMA(())],
        compiler_params=_params(("arbitrary",)),
        name="moe_combine",
    )(pos, x, gates, g.reshape(1, d), y)


def _moe(h, h32, x, w_router, wg, wu, wd, g, out_dtype, m_prompt):
    m, d = x.shape
    tmb = EXPERT_ROWS
    info, gates, cnt = _router(h, w_router)

    counts = cnt[0, :N_EXPERTS].astype(I32)
    padded = (counts + (tmb - 1)) // tmb * tmb
    ends = jnp.cumsum(padded)
    offsets = ends - padded
    n_blocks = -(-(TOP_K * m + N_EXPERTS * (tmb - 1)) // tmb)
    n_valid = ends[-1] // tmb
    starts = jnp.arange(n_blocks, dtype=I32) * tmb
    owner = jnp.sum((starts[:, None] >= ends[None, :]).astype(I32), axis=1)
    blk_expert = jnp.minimum(jnp.where(starts < ends[-1], owner, owner[n_valid - 1]), N_EXPERTS - 1)

    n_valid = n_valid.reshape(1)
    pos, src = _slots(info, offsets, counts, ends, n_blocks * tmb)
    xs = _gather(h32, src, n_valid)
    y = _experts(xs, wg, wu, wd, blk_expert, n_valid, 1024)
    return (_combine(x, y, pos, gates, g, out_dtype, 0, m_prompt),
            _combine(x, y, pos, gates, g, out_dtype, m_prompt, m - m_prompt))


def _pad_ff(w, axis, mult):
    ff = w.shape[axis]
    pad = (-ff) % mult
    if pad:
        widths = [(0, 0)] * w.ndim
        widths[axis] = (0, pad)
        w = jnp.pad(w, widths)
    return w


def kernel(x_prompt, x_sample, state_hgrn, state_rglru, state_conv, hgrn_lb_logits, norm_mix, w_in,
           hgrn_gnorm, rg_conv_w, rg_conv_b, rg_wa, rg_ba, rg_wx, rg_bx, rg_a_param, w_br_a, w_br_b,
           w_out, norm_ffn, ffn_w_gate, ffn_w_up, ffn_w_down, moe_router, moe_w_gate, moe_w_up,
           moe_w_down, norm_final):
    depth = w_in.shape[0]
    batch, seq, d = x_prompt.shape
    n_seq = x_sample.shape[0]
    m_prompt = batch * seq
    m = m_prompt + n_seq

    sm = jax.nn.softmax(hgrn_lb_logits.astype(F32), axis=0)
    lbs = jnp.maximum(jnp.cumsum(sm, axis=0) - sm[0:1], 0.0)
    one_m_lbs = 1.0 - lbs

    x = jnp.concatenate([x_prompt.reshape(m_prompt, d), x_sample.reshape(n_seq, d)], axis=0)
    h = _rmsnorm(x, norm_mix[0], BF16)

    chunk = 256 if seq % 256 == 0 else seq
    rg_rows = 512 if seq % 512 == 0 else chunk
    hg_p, rg_p, cv_p, hg_s, rg_s, cv_s = [], [], [], [], [], []
    for l in range(depth):
        proj = _in_proj(h, w_in, l)
        vrow = lambda a: a[l].reshape(1, -1)
        o_all, s_prompt = _hgrn_prompt(proj, vrow(lbs), vrow(one_m_lbs), vrow(hgrn_gnorm),
                                       batch, seq, m, chunk)
        o_all, s_sample = _hgrn_sample(proj, vrow(lbs), vrow(one_m_lbs), vrow(hgrn_gnorm),
                                       state_hgrn, l, o_all, m_prompt)
        rg_args = (rg_conv_w[l], vrow(rg_conv_b), rg_wa[l], rg_ba[l].reshape(RG_BLOCKS, 1, RG_BW),
                   rg_wx[l], rg_bx[l].reshape(RG_BLOCKS, 1, RG_BW), vrow(rg_a_param))
        y_all, h_prompt = _rg_prompt(proj, *rg_args, batch, seq, m, rg_rows)
        y_all, h_sample = _rg_sample(proj, state_conv[l], state_rglru[l], *rg_args, y_all, m_prompt)

        dense = l % 2 == 0
        merged = _merge_out(o_all, y_all, proj, x, w_br_a[l].astype(BF16), w_br_b[l].astype(BF16),
                            w_out[l].astype(BF16), norm_ffn[l], with_h32=not dense)
        x, h2 = merged[0], merged[1]

        last = l == depth - 1
        g_next = norm_final if last else norm_mix[l + 1]
        out_dtype = F32 if last else BF16
        j = l // 2
        if dense:
            wg = _pad_ff(ffn_w_gate[j], 1, 512).astype(BF16)
            wu = _pad_ff(ffn_w_up[j], 1, 512).astype(BF16)
            wd = _pad_ff(ffn_w_down[j], 0, 512).astype(BF16)
            h, x = _ffn(h2, x, wg, wu, wd, g_next, out_dtype, 512)
        else:
            assert last
            y_prompt, y_sample = _moe(h2, merged[2], x, moe_router[j], moe_w_gate[j].astype(BF16),
                                      moe_w_up[j].astype(BF16), moe_w_down[j].astype(BF16),
                                      g_next, out_dtype, m_prompt)

        xr = proj[:, 2 * HG_F + 2 * HG_I:2 * HG_F + 2 * HG_I + RG_WIDTH]
        hg_p.append(s_prompt)
        rg_p.append(h_prompt.reshape(batch, RG_WIDTH))
        cv_p.append(xr[:m_prompt].reshape(batch, seq, RG_WIDTH)[:, seq - (CONV_W - 1):])
        hg_s.append(s_sample)
        rg_s.append(h_sample)
        cv_s.append(jnp.concatenate([state_conv[l][:, 1:], xr[m_prompt:, None, :]], axis=1))

    y_prompt = y_prompt.reshape(batch, seq, d)
    y_sample = y_sample.reshape(n_seq, 1, d)
    return (y_prompt, y_sample, jnp.stack(hg_p), jnp.stack(rg_p), jnp.stack(cv_p),
            jnp.stack(hg_s), jnp.stack(rg_s), jnp.stack(cv_s))
```

```python
import functools

import jax
import jax.numpy as jnp
from jax import lax
from jax.experimental import pallas as pl
from jax.experimental.pallas import tpu as pltpu

D_MODEL = 2048
HG_HEADS = 8
HG_DK = 128
HG_DV = 128
HG_F = HG_HEADS * HG_DK
HG_I = HG_HEADS * HG_DV
RG_WIDTH = 1024
RG_BLOCKS = 8
RG_BW = RG_WIDTH // RG_BLOCKS
CONV_W = 4
RG_C = 8.0
IN_COLS = 2 * HG_F + 2 * HG_I + 2 * RG_WIDTH + 2 * D_MODEL
N_EXPERTS = 8
TOP_K = 2
EPS = 1e-6

LANES = 128
SUBLANES = 8
BF16_SUBLANES = 16
VMEM_LIMIT = 56 * 1024 * 1024
EXPERT_ROWS = 512

BF16 = jnp.bfloat16
F32 = jnp.float32
I32 = jnp.int32


def _params(sem, vmem=VMEM_LIMIT):
    return pltpu.CompilerParams(dimension_semantics=sem, vmem_limit_bytes=vmem)


def _row_tile(m, target):
    best = None
    for t in range(BF16_SUBLANES, min(m, target) + 1, BF16_SUBLANES):
        if m % t == 0:
            best = t
    assert best is not None, (m, target)
    return best


def _dot(a, b):
    return jnp.dot(a, b, preferred_element_type=F32)


def _dot_nt(a, b):
    return lax.dot_general(a, b, (((1,), (1,)), ((), ())), preferred_element_type=F32)


def _dot_tn(a, b):
    return lax.dot_general(a, b, (((0,), (0,)), ((), ())), preferred_element_type=F32)


def _sigmoid(x):
    return 1.0 / (1.0 + jnp.exp(-x))


def _log_sigmoid(x):
    return jnp.minimum(x, 0.0) - jnp.log1p(jnp.exp(-jnp.abs(x)))


def _rms(x, g):
    ms = jnp.mean(x * x, axis=-1, keepdims=True)
    return x * lax.rsqrt(ms + EPS) * g


def _rmsnorm_kernel(x_ref, g_ref, o_ref):
    o_ref[...] = _rms(x_ref[...], g_ref[...]).astype(o_ref.dtype)


def _rmsnorm(x, g, out_dtype):
    m, d = x.shape
    tm = _row_tile(m, 640)
    return pl.pallas_call(
        _rmsnorm_kernel,
        grid=(m // tm,),
        in_specs=[pl.BlockSpec((tm, d), lambda i: (i, 0)),
                  pl.BlockSpec((1, d), lambda i: (0, 0))],
        out_specs=pl.BlockSpec((tm, d), lambda i: (i, 0)),
        out_shape=jax.ShapeDtypeStruct((m, d), out_dtype),
        compiler_params=_params(("parallel",)),
        name="rmsnorm",
    )(x, g.reshape(1, d))


def _in_proj_kernel(x_ref, w_ref, o_ref, wbf_ref):
    @pl.when(pl.program_id(1) == 0)
    def _():
        wbf_ref[...] = w_ref[...].astype(BF16)

    o_ref[...] = _dot(x_ref[...], wbf_ref[...])


def _in_proj(h, w_in, layer):
    m, d = h.shape
    n = w_in.shape[-1]
    tm = _row_tile(m, 1664)
    tn = 1024
    return pl.pallas_call(
        _in_proj_kernel,
        grid=(n // tn, m // tm),
        in_specs=[pl.BlockSpec((tm, d), lambda j, i: (i, 0)),
                  pl.BlockSpec((None, d, tn), lambda j, i: (layer, 0, j))],
        out_specs=pl.BlockSpec((tm, tn), lambda j, i: (i, j)),
        out_shape=jax.ShapeDtypeStruct((m, n), F32),
        scratch_shapes=[pltpu.VMEM((d, tn), BF16)],
        compiler_params=_params(("arbitrary", "arbitrary")),
        name="in_proj",
    )(h, w_in)


def _hgrn_gates(fz, lb, one_m_lb):
    u = jnp.exp(-jnp.abs(fz))
    r = 1.0 / (1.0 + u)
    nonneg = fz >= 0.0
    sig = jnp.where(nonneg, r, u * r)
    sig_neg = jnp.where(nonneg, u * r, r)
    f = lb + one_m_lb * sig
    return jnp.log(f), one_m_lb * sig_neg, f


def _hgrn_out(o, gate, gnorm):
    return _rms(o, gnorm) * (gate * _sigmoid(gate))


def _hgrn_prompt_kernel(q_ref, f_ref, i_ref, g_ref, lb_ref, omlb_ref, gn_ref, o_init_ref,
                        o_ref, s_out_ref, s_ref, *, chunk):
    del o_init_ref
    C = chunk
    c = pl.program_id(2)

    @pl.when(c == 0)
    def _():
        s_ref[...] = jnp.zeros_like(s_ref)

    row = lax.broadcasted_iota(I32, (C, HG_DK), 0)
    rc_xor = lax.broadcasted_iota(I32, (C, C), 0) ^ lax.broadcasted_iota(I32, (C, C), 1)

    for hh in range(HGRN_HEADS_PER_STEP):
        cols = slice(hh * HG_DK, (hh + 1) * HG_DK)
        q = q_ref[:, cols]
        v = i_ref[:, cols].astype(BF16)
        g, k, _ = _hgrn_gates(f_ref[:, cols], lb_ref[:, cols], omlb_ref[:, cols])

        a_mat = jnp.where(rc_xor == 0, _dot_nt(q.astype(BF16), k.astype(BF16)), 0.0)

        pre, suf, tot = g, jnp.zeros_like(g), g
        w = 1
        while w < C:
            upper = (row & w) != 0
            e = jnp.exp(jnp.where(upper, pre, suf))
            qw = jnp.where(upper, q * e, 0.0).astype(BF16)
            kw = jnp.where(upper, 0.0, k * e).astype(BF16)
            a_mat = a_mat + jnp.where(rc_xor < 2 * w, _dot_nt(qw, kw), 0.0)
            up = pltpu.roll(tot, w, 0)
            dn = pltpu.roll(tot, C - w, 0)
            pre = pre + jnp.where(upper, up, 0.0)
            suf = suf + jnp.where(upper, 0.0, dn)
            tot = tot + jnp.where(upper, up, dn)
            w *= 2

        s_old = s_ref[hh]
        o = (_dot(a_mat.astype(BF16), v)
             + _dot((q * jnp.exp(pre)).astype(BF16), s_old.astype(BF16)))
        decay_rows = jnp.exp(jnp.broadcast_to(tot[0:1, :], (HG_DK, HG_DK))).T
        s_new = decay_rows * s_old + _dot_tn((k * jnp.exp(suf)).astype(BF16), v)
        s_ref[hh] = s_new

        o_ref[:, cols] = _hgrn_out(o, g_ref[:, cols], gn_ref[:, cols]).astype(o_ref.dtype)

    @pl.when(c == pl.num_programs(2) - 1)
    def _():
        s_out_ref[...] = s_ref[...]


HGRN_HEADS_PER_STEP = 2


def _hgrn_prompt(proj, lb, one_m_lb, gnorm, batch, seq, m_total, chunk):
    n_chunks = seq // chunk
    hps = HGRN_HEADS_PER_STEP
    width = hps * LANES
    groups = HG_HEADS // hps
    tok = lambda off: pl.BlockSpec((chunk, width), lambda b, h, c: (b * n_chunks + c, off + h))
    vec = pl.BlockSpec((1, width), lambda b, h, c: (0, h))
    return pl.pallas_call(
        functools.partial(_hgrn_prompt_kernel, chunk=chunk),
        grid=(batch, groups, n_chunks),
        in_specs=[tok(0), tok(groups), tok(2 * groups), tok(3 * groups), vec, vec, vec,
                  pl.BlockSpec(memory_space=pl.ANY)],
        input_output_aliases={7: 0},
        out_specs=[pl.BlockSpec((chunk, width), lambda b, h, c: (b * n_chunks + c, h)),
                   pl.BlockSpec((None, hps, HG_DK, HG_DV), lambda b, h, c: (b, h, 0, 0))],
        out_shape=[jax.ShapeDtypeStruct((m_total, HG_I), BF16),
                   jax.ShapeDtypeStruct((batch, HG_HEADS, HG_DK, HG_DV), F32)],
        scratch_shapes=[pltpu.VMEM((hps, HG_DK, HG_DV), F32)],
        compiler_params=_params(("parallel", "parallel", "arbitrary")),
        name="hgrn_prompt",
    )(proj, proj, proj, proj, lb, one_m_lb, gnorm, jnp.zeros((m_total, HG_I), BF16))


HGRN_SAMPLE_ROWS = 16


def _hgrn_sample_kernel(q_ref, f_ref, i_ref, g_ref, lb_ref, omlb_ref, gn_ref, s_ref, o_prev_ref,
                        *rest, layer, first):
    del o_prev_ref
    o_ref, s_out_ref = rest[-2:]
    n = HGRN_SAMPLE_ROWS
    q = q_ref[...]
    vi = i_ref[...]
    _, k, f = _hgrn_gates(f_ref[...], lb_ref[...], omlb_ref[...])

    def columns(x):
        pad = jnp.zeros((LANES - n, HG_DK), F32)
        return jnp.concatenate([x, pad], axis=0).T

    f_t, k_t = columns(f), columns(k)
    qb = q.astype(BF16)
    seq_id = lax.broadcasted_iota(I32, (n, HG_DV), 0)
    o = jnp.zeros((n, HG_DV), F32)
    for j in range(n):
        bc = lambda xt: jnp.broadcast_to(xt[:, j:j + 1], (HG_DK, HG_DV))
        s_new = bc(f_t) * s_ref[j] + bc(k_t) * vi[j:j + 1, :]
        if first:
            for l2 in range(s_out_ref.shape[0]):
                s_out_ref[l2, j] = s_new if l2 == layer else jnp.zeros_like(s_new)
        else:
            s_out_ref[j] = s_new
        o_j = _dot(qb[j:j + 1, :], s_new.astype(BF16))
        o = jnp.where(seq_id == j, o_j, o)
    o_ref[...] = _hgrn_out(o, g_ref[...], gn_ref[...]).astype(o_ref.dtype)


def _hgrn_sample(proj, lb, one_m_lb, gnorm, states, layer, o_all, row0, new_states):
    depth, n_seq = states.shape[:2]
    n = HGRN_SAMPLE_ROWS
    assert n_seq % n == 0 and row0 % n == 0
    blk0 = row0 // n
    first = new_states is None
    tok = lambda off: pl.BlockSpec((n, LANES), lambda j, h: (blk0 + j, off + h))
    vec = pl.BlockSpec((1, LANES), lambda j, h: (0, h))
    one_layer = pl.BlockSpec((None, n, None, HG_DK, HG_DV), lambda j, h: (layer, j, h, 0, 0))
    all_layers = pl.BlockSpec((depth, n, None, HG_DK, HG_DV), lambda j, h: (0, j, h, 0, 0))
    hbm = pl.BlockSpec(memory_space=pl.ANY)
    args = [proj, proj, proj, proj, lb, one_m_lb, gnorm, states, o_all]
    in_specs = [tok(0), tok(HG_HEADS), tok(2 * HG_HEADS), tok(3 * HG_HEADS), vec, vec, vec,
                one_layer, hbm]
    aliases = {8: 0}
    if not first:
        args.append(new_states)
        in_specs.append(hbm)
        aliases[9] = 1
    return pl.pallas_call(
        functools.partial(_hgrn_sample_kernel, layer=layer, first=first),
        grid=(n_seq // n, HG_HEADS),
        in_specs=in_specs,
        out_specs=[pl.BlockSpec((n, LANES), lambda j, h: (blk0 + j, h)),
                   all_layers if first else one_layer],
        out_shape=[jax.ShapeDtypeStruct(o_all.shape, o_all.dtype),
                   jax.ShapeDtypeStruct(states.shape, F32)],
        input_output_aliases=aliases,
        compiler_params=_params(("parallel", "parallel")),
        name="hgrn_sample",
    )(*args)


def _gelu_tanh(x):
    return 0.5 * x * (1.0 + jnp.tanh(0.7978845608028654 * (x + 0.044715 * (x * x * x))))


def _rg_gates(conv, wa_ref, ba_ref, wx_ref, bx_ref, ap_ref):
    cb = conv.astype(BF16)
    r = _sigmoid(_dot(cb, wa_ref[...].astype(BF16)) + ba_ref[...])
    ig = _sigmoid(_dot(cb, wx_ref[...].astype(BF16)) + bx_ref[...])
    log_a = RG_C * r * _log_sigmoid(ap_ref[...])
    a = jnp.exp(log_a)
    t = jnp.tanh(log_a)
    b = jnp.sqrt(-2.0 * t / (1.0 - t)) * ig * conv
    return a, b


def _rg_prompt_kernel(x_ref, gate_ref, cw_ref, cb_ref, wa_ref, ba_ref, wx_ref, bx_ref, ap_ref,
                      y_init_ref, y_ref, h_out_ref, xp_ref, h_ref, *, rows):
    del y_init_ref
    T = rows
    t = pl.program_id(2)

    @pl.when(t == 0)
    def _():
        xp_ref[0:8, :] = jnp.zeros((8, RG_BW), F32)
        h_ref[...] = jnp.zeros_like(h_ref)

    xp_ref[8:8 + T, :] = x_ref[...]
    conv = cb_ref[...]
    for j in range(CONV_W):
        conv = conv + xp_ref[5 + j:5 + j + T, :] * cw_ref[j:j + 1, :]
    xp_ref[0:8, :] = xp_ref[T:T + 8, :]

    a, b = _rg_gates(conv, wa_ref, ba_ref, wx_ref, bx_ref, ap_ref)

    row = lax.broadcasted_iota(I32, (T, RG_BW), 0)
    d = 1
    while d < T:
        ok = row >= d
        b = jnp.where(ok, a * pltpu.roll(b, d, 0) + b, b)
        a = jnp.where(ok, a * pltpu.roll(a, d, 0), a)
        d *= 2
    h = a * h_ref[0:1, :] + b
    h_last = h[T - 1:T, :]
    h_ref[...] = jnp.broadcast_to(h_last, h_ref.shape)

    y_ref[...] = (h * _gelu_tanh(gate_ref[...])).astype(y_ref.dtype)

    @pl.when(t == pl.num_programs(2) - 1)
    def _():
        h_out_ref[...] = h_last


def _rg_prompt(proj, cw, cb, wa, ba, wx, bx, ap, batch, seq, m_total, rows):
    n_t = seq // rows
    xoff = (2 * HG_F + 2 * HG_I) // LANES
    goff = xoff + RG_BLOCKS
    tok = lambda off: pl.BlockSpec((rows, LANES), lambda b, n, t: (b * n_t + t, off + n))
    vec = pl.BlockSpec((1, LANES), lambda b, n, t: (0, n))
    blkw = pl.BlockSpec((None, RG_BW, RG_BW), lambda b, n, t: (n, 0, 0))
    blkb = pl.BlockSpec((None, 1, RG_BW), lambda b, n, t: (n, 0, 0))
    return pl.pallas_call(
        functools.partial(_rg_prompt_kernel, rows=rows),
        grid=(batch, RG_BLOCKS, n_t),
        in_specs=[tok(xoff), tok(goff),
                  pl.BlockSpec((CONV_W, LANES), lambda b, n, t: (0, n)), vec,
                  blkw, blkb, blkw, blkb, vec, pl.BlockSpec(memory_space=pl.ANY)],
        input_output_aliases={9: 0},
        out_specs=[pl.BlockSpec((rows, LANES), lambda b, n, t: (b * n_t + t, n)),
                   pl.BlockSpec((None, 1, LANES), lambda b, n, t: (b, 0, n))],
        out_shape=[jax.ShapeDtypeStruct((m_total, RG_WIDTH), BF16),
                   jax.ShapeDtypeStruct((batch, 1, RG_WIDTH), F32)],
        scratch_shapes=[pltpu.VMEM((rows + 8, RG_BW), F32), pltpu.VMEM((8, RG_BW), F32)],
        compiler_params=_params(("parallel", "parallel", "arbitrary")),
        name="rg_prompt",
    )(proj, proj, cw, cb, wa, ba, wx, bx, ap, jnp.zeros((m_total, RG_WIDTH), BF16))


def _rg_sample_kernel(x_ref, gate_ref, b0_ref, b1_ref, b2_ref, h0_ref, cw_ref, cb_ref,
                      wa_ref, ba_ref, wx_ref, bx_ref, ap_ref, y_prev_ref, y_ref, h_out_ref):
    del y_prev_ref
    conv = cb_ref[...]
    for j, r in enumerate((b0_ref, b1_ref, b2_ref, x_ref)):
        conv = conv + r[...] * cw_ref[j:j + 1, :]
    a, b = _rg_gates(conv, wa_ref, ba_ref, wx_ref, bx_ref, ap_ref)
    h = a * h0_ref[...] + b
    h_out_ref[...] = h
    y_ref[...] = (h * _gelu_tanh(gate_ref[...])).astype(y_ref.dtype)


def _rg_sample(proj, conv_state, h0, cw, cb, wa, ba, wx, bx, ap, y_all, row0):
    n_seq = h0.shape[0]
    assert row0 % n_seq == 0 and n_seq % BF16_SUBLANES == 0
    blk0 = row0 // n_seq
    xoff = (2 * HG_F + 2 * HG_I) // LANES
    goff = xoff + RG_BLOCKS
    tok = lambda off: pl.BlockSpec((n_seq, LANES), lambda n: (blk0, off + n))
    buf = lambda j: pl.BlockSpec((n_seq, LANES), lambda n: (0, j * RG_BLOCKS + n))
    vec = pl.BlockSpec((1, LANES), lambda n: (0, n))
    blkw = pl.BlockSpec((None, RG_BW, RG_BW), lambda n: (n, 0, 0))
    blkb = pl.BlockSpec((None, 1, RG_BW), lambda n: (n, 0, 0))
    flat_state = conv_state.reshape(n_seq, (CONV_W - 1) * RG_WIDTH)
    return pl.pallas_call(
        _rg_sample_kernel,
        grid=(RG_BLOCKS,),
        in_specs=[tok(xoff), tok(goff), buf(0), buf(1), buf(2),
                  pl.BlockSpec((n_seq, LANES), lambda n: (0, n)),
                  pl.BlockSpec((CONV_W, LANES), lambda n: (0, n)), vec,
                  blkw, blkb, blkw, blkb, vec,
                  pl.BlockSpec(memory_space=pl.ANY)],
        out_specs=[pl.BlockSpec((n_seq, LANES), lambda n: (blk0, n)),
                   pl.BlockSpec((n_seq, LANES), lambda n: (0, n))],
        out_shape=[jax.ShapeDtypeStruct(y_all.shape, y_all.dtype),
                   jax.ShapeDtypeStruct((n_seq, RG_WIDTH), F32)],
        input_output_aliases={13: 0},
        compiler_params=_params(("parallel",)),
        name="rg_sample",
    )(proj, proj, flat_state, flat_state, flat_state, h0, cw, cb, wa, ba, wx, bx, ap, y_all)


def _merge_out_kernel(o_ref, y_ref, ma_ref, mb_ref, x_ref, wa_ref, wb_ref, wo_ref, g_ref,
                      xo_ref, h_ref, *maybe_h32_ref):
    merged = (_sigmoid(ma_ref[...]) * _dot(o_ref[...], wa_ref[...])
              + _sigmoid(mb_ref[...]) * _dot(y_ref[...], wb_ref[...]))
    x_new = x_ref[...] + _dot(merged.astype(BF16), wo_ref[...])
    xo_ref[...] = x_new
    h = _rms(x_new, g_ref[...])
    h_ref[...] = h.astype(h_ref.dtype)
    for r in maybe_h32_ref:
        r[...] = h


def _merge_out(o, y, proj, x, w_a, w_b, w_o, g, with_h32):
    m, d = x.shape
    tm = _row_tile(m, 320)
    moff = (2 * HG_F + 2 * HG_I + 2 * RG_WIDTH) // d
    row = lambda width, col: pl.BlockSpec((tm, width), lambda i: (i, col))
    res = lambda shape: pl.BlockSpec(shape, lambda i: (0, 0), pipeline_mode=pl.Buffered(1))
    n_out = 3 if with_h32 else 2
    return pl.pallas_call(
        _merge_out_kernel,
        grid=(m // tm,),
        in_specs=[row(HG_I, 0), row(RG_WIDTH, 0), row(d, moff), row(d, moff + 1), row(d, 0),
                  res(w_a.shape), res(w_b.shape), res(w_o.shape), res((1, d))],
        out_specs=[row(d, 0)] * n_out,
        out_shape=[jax.ShapeDtypeStruct((m, d), F32), jax.ShapeDtypeStruct((m, d), BF16),
                   jax.ShapeDtypeStruct((m, d), F32)][:n_out],
        compiler_params=_params(("parallel",)),
        name="merge_out",
    )(o, y, proj, proj, x, w_a, w_b, w_o, g.reshape(1, d))


def _ffn_kernel(h_ref, x_ref, wg_ref, wu_ref, wd_ref, g_ref, o_ref, xo_ref):
    f = pl.program_id(1)

    @pl.when(f == 0)
    def _():
        xo_ref[...] = jnp.zeros_like(xo_ref)

    h = h_ref[...]
    gate = _dot(h, wg_ref[...])
    act = (gate * _sigmoid(gate)) * _dot(h, wu_ref[...])
    xo_ref[...] += _dot(act.astype(BF16), wd_ref[...])

    @pl.when(f == pl.num_programs(1) - 1)
    def _():
        x_new = x_ref[...] + xo_ref[...]
        xo_ref[...] = x_new
        o_ref[...] = _rms(x_new, g_ref[...]).astype(o_ref.dtype)


def _ffn(h, x, wg, wu, wd, g, out_dtype, tf):
    m, d = x.shape
    ff = wg.shape[1]
    tm = _row_tile(m, 640)
    return pl.pallas_call(
        _ffn_kernel,
        grid=(m // tm, ff // tf),
        in_specs=[pl.BlockSpec((tm, d), lambda i, f: (i, 0)),
                  pl.BlockSpec((tm, d), lambda i, f: (i, 0)),
                  pl.BlockSpec((d, tf), lambda i, f: (0, f)),
                  pl.BlockSpec((d, tf), lambda i, f: (0, f)),
                  pl.BlockSpec((tf, d), lambda i, f: (f, 0)),
                  pl.BlockSpec((1, d), lambda i, f: (0, 0))],
        out_specs=[pl.BlockSpec((tm, d), lambda i, f: (i, 0)),
                   pl.BlockSpec((tm, d), lambda i, f: (i, 0))],
        out_shape=[jax.ShapeDtypeStruct((m, d), out_dtype), jax.ShapeDtypeStruct((m, d), F32)],
        compiler_params=_params(("parallel", "arbitrary")),
        name="ffn_dense",
    )(h, x, wg, wu, wd, g.reshape(1, d))


def _router_kernel(h_ref, w_ref, info_ref, gate_ref, cnt_ref, carry_ref):
    @pl.when(pl.program_id(0) == 0)
    def _():
        carry_ref[...] = jnp.zeros_like(carry_ref)

    logits = _dot(h_ref[...], w_ref[...])
    tm = logits.shape[0]
    lane = lax.broadcasted_iota(I32, logits.shape, 1)
    neg = jnp.float32(-jnp.inf)
    logits = jnp.where(lane < N_EXPERTS, logits, neg)
    m1 = jnp.max(logits, axis=-1, keepdims=True)
    i1 = jnp.min(jnp.where(logits == m1, lane, LANES), axis=-1, keepdims=True)
    rest = jnp.where(lane == i1, neg, logits)
    m2 = jnp.max(rest, axis=-1, keepdims=True)
    i2 = jnp.min(jnp.where(rest == m2, lane, LANES), axis=-1, keepdims=True)
    e = jnp.exp(m2 - m1)
    g1 = 1.0 / (1.0 + e)
    g2 = e / (1.0 + e)
    gate_ref[...] = jnp.where(lane == 0, g1, jnp.where(lane == 1, g2, 0.0))

    chosen = jnp.where((lane == i1) | (lane == i2), 1.0, 0.0)
    lower = jnp.where(lax.broadcasted_iota(I32, (tm, tm), 0) > lax.broadcasted_iota(I32, (tm, tm), 1),
                      1.0, 0.0).astype(BF16)
    before = _dot(lower, chosen.astype(BF16)) + carry_ref[0:1, :]
    r1 = jnp.sum(jnp.where(lane == i1, before, 0.0), axis=-1, keepdims=True).astype(I32)
    r2 = jnp.sum(jnp.where(lane == i2, before, 0.0), axis=-1, keepdims=True).astype(I32)
    info_ref[...] = jnp.where(lane == 0, i1, jnp.where(lane == 1, i2,
                              jnp.where(lane == 2, r1, jnp.where(lane == 3, r2, 0))))
    total = carry_ref[0:1, :] + jnp.sum(chosen, axis=0, keepdims=True)
    carry_ref[...] = jnp.broadcast_to(total, carry_ref.shape)
    cnt_ref[...] = jnp.broadcast_to(total, cnt_ref.shape)


def _router(h, w_router):
    m, d = h.shape
    tm = _row_tile(m, 640)
    w = jnp.pad(w_router, ((0, 0), (0, LANES - N_EXPERTS))).astype(BF16)
    return pl.pallas_call(
        _router_kernel,
        grid=(m // tm,),
        in_specs=[pl.BlockSpec((tm, d), lambda i: (i, 0)),
                  pl.BlockSpec((d, LANES), lambda i: (0, 0))],
        out_specs=[pl.BlockSpec((tm, LANES), lambda i: (i, 0)),
                   pl.BlockSpec((tm, LANES), lambda i: (i, 0)),
                   pl.BlockSpec((SUBLANES, LANES), lambda i: (0, 0))],
        out_shape=[jax.ShapeDtypeStruct((m, LANES), I32),
                   jax.ShapeDtypeStruct((m, LANES), F32),
                   jax.ShapeDtypeStruct((SUBLANES, LANES), F32)],
        scratch_shapes=[pltpu.VMEM((SUBLANES, LANES), F32)],
        compiler_params=_params(("arbitrary",)),
        name="router",
    )(h, w)


def _row_copy(src_ref, src_row, dst_ref, dst_row, sem):
    return pltpu.make_async_copy(src_ref.at[pl.ds(src_row, 1), :], dst_ref.at[pl.ds(dst_row, 1), :], sem)


def _rows_wait(src_ref, dst_ref, n_rows, sem):
    pltpu.make_async_copy(src_ref.at[pl.ds(0, n_rows), :], dst_ref.at[pl.ds(0, n_rows), :], sem).wait()


def _slots_kernel(off_ref, cnt_ref, end_ref, tok_ref, pos_ref, src_ref):
    tb = pos_ref.shape[-1] // TOP_K
    base = pl.program_id(0) * tb

    @pl.when(pl.program_id(0) == 0)
    def _():
        def clear(lo, hi):
            def body(p, carry):
                src_ref[p] = 0
                return carry
            lax.fori_loop(lo, hi, body, 0)

        for e in range(N_EXPERTS):
            clear(off_ref[e] + cnt_ref[e], end_ref[e])
        clear(end_ref[N_EXPERTS - 1], src_ref.shape[0])

    def body(j, carry):
        for k in range(TOP_K):
            p = off_ref[tok_ref[0, 4 * j + k]] + tok_ref[0, 4 * j + 2 + k]
            pos_ref[0, TOP_K * j + k] = p
            src_ref[p] = base + j
        return carry

    lax.fori_loop(0, tb, body, 0, unroll=4)


def _slots(tok_info, offsets, counts, ends, n_rows):
    m = tok_info.shape[0]
    tb = _row_tile(m, 512)
    tok = tok_info[:, :4].reshape(m // tb, 1, 4 * tb)
    smem = pl.BlockSpec(memory_space=pltpu.SMEM)
    return pl.pallas_call(
        _slots_kernel,
        grid=(m // tb,),
        in_specs=[smem, smem, smem,
                  pl.BlockSpec((None, 1, 4 * tb), lambda i: (i, 0, 0), memory_space=pltpu.SMEM)],
        out_specs=[pl.BlockSpec((None, 1, TOP_K * tb), lambda i: (i, 0, 0), memory_space=pltpu.SMEM),
                   smem],
        out_shape=[jax.ShapeDtypeStruct((m // tb, 1, TOP_K * tb), I32),
                   jax.ShapeDtypeStruct((n_rows,), I32)],
        compiler_params=_params(("arbitrary",)),
        name="moe_slots",
    )(offsets, counts, ends, tok)


def _gather_kernel(nv_ref, src_ref, h_ref, xs_ref, buf_ref, sem):
    tmb = buf_ref.shape[0]
    used = pl.program_id(0) < nv_ref[0]

    @pl.when(used)
    def _():
        def body(r2, carry):
            for p in range(2):
                r = 2 * r2 + p
                _row_copy(h_ref, src_ref[0, r], buf_ref, r, sem).start(priority=p)
            return carry

        lax.fori_loop(0, tmb // 2, body, 0, unroll=4)
        _rows_wait(h_ref, buf_ref, tmb, sem)
        xs_ref[...] = buf_ref[...].astype(xs_ref.dtype)

    @pl.when(jnp.logical_not(used))
    def _():
        xs_ref[...] = jnp.zeros_like(xs_ref)


def _gather(h32, src, n_valid):
    m, d = h32.shape
    tmb = EXPERT_ROWS
    assert m >= tmb
    n_blocks = src.shape[0] // tmb
    return pl.pallas_call(
        _gather_kernel,
        grid=(n_blocks,),
        in_specs=[pl.BlockSpec(memory_space=pltpu.SMEM),
                  pl.BlockSpec((None, 1, tmb), lambda b: (b, 0, 0), memory_space=pltpu.SMEM),
                  pl.BlockSpec(memory_space=pl.ANY)],
        out_specs=pl.BlockSpec((tmb, d), lambda b: (b, 0)),
        out_shape=jax.ShapeDtypeStruct((n_blocks * tmb, d), BF16),
        scratch_shapes=[pltpu.VMEM((tmb, d), F32), pltpu.SemaphoreType.DMA(())],
        compiler_params=_params(("arbitrary",)),
        name="moe_gather",
    )(n_valid, src.reshape(n_blocks, 1, tmb), h32)


def _experts_kernel(be_ref, nv_ref, x_ref, wg_ref, wu_ref, wd_ref, y_ref):
    del be_ref

    @pl.when(pl.program_id(1) == 0)
    def _():
        y_ref[...] = jnp.zeros_like(y_ref)

    @pl.when(pl.program_id(0) < nv_ref[0])
    def _():
        h = x_ref[...]
        gate = _dot(h, wg_ref[...])
        act = (gate * _sigmoid(gate)) * _dot(h, wu_ref[...])
        y_ref[...] += _dot(act.astype(BF16), wd_ref[...])


def _experts(xs, wg, wu, wd, blk_expert, n_valid, tf):
    n_rows, d = xs.shape
    ff = wg.shape[-1]
    n_f = ff // tf
    tmb = EXPERT_ROWS
    xmap = lambda b, f, be, nv: (jnp.minimum(b, nv[0] - 1), 0)
    fidx = lambda b, f, nv: jnp.where(b < nv[0], f, n_f - 1)
    grid_spec = pltpu.PrefetchScalarGridSpec(
        num_scalar_prefetch=2,
        grid=(n_rows // tmb, n_f),
        in_specs=[pl.BlockSpec((tmb, d), xmap),
                  pl.BlockSpec((None, d, tf), lambda b, f, be, nv: (be[b], 0, fidx(b, f, nv))),
                  pl.BlockSpec((None, d, tf), lambda b, f, be, nv: (be[b], 0, fidx(b, f, nv))),
                  pl.BlockSpec((None, tf, d), lambda b, f, be, nv: (be[b], fidx(b, f, nv), 0))],
        out_specs=pl.BlockSpec((tmb, d), lambda b, f, be, nv: (b, 0)),
    )
    return pl.pallas_call(
        _experts_kernel,
        grid_spec=grid_spec,
        out_shape=jax.ShapeDtypeStruct((n_rows, d), F32),
        compiler_params=_params(("arbitrary", "arbitrary")),
        name="moe_experts",
    )(blk_expert, n_valid, xs, wg, wu, wd)


def _combine_kernel(pos_ref, x_ref, gate_ref, g_ref, y_ref, o_ref, ya_ref, yb_ref, sem):
    tb = x_ref.shape[0]

    def body(j, carry):
        _row_copy(y_ref, pos_ref[0, TOP_K * j], ya_ref, j, sem).start(priority=0)
        _row_copy(y_ref, pos_ref[0, TOP_K * j + 1], yb_ref, j, sem).start(priority=1)
        return carry

    lax.fori_loop(0, tb, body, 0, unroll=4)
    _rows_wait(y_ref, ya_ref, tb, sem)
    _rows_wait(y_ref, yb_ref, tb, sem)
    gate = gate_ref[...]
    x_new = x_ref[...] + (gate[:, 0:1] * ya_ref[...] + gate[:, 1:2] * yb_ref[...])
    o_ref[...] = _rms(x_new, g_ref[...]).astype(o_ref.dtype)


def _combine(x, y, pos, gates, g, out_dtype, row0, n_rows):
    d = x.shape[1]
    tb = _row_tile(n_rows, 512)
    assert row0 % tb == 0
    blk0 = row0 // tb
    pos = pos.reshape(-1)[TOP_K * row0:TOP_K * (row0 + n_rows)].reshape(n_rows // tb, 1, TOP_K * tb)
    return pl.pallas_call(
        _combine_kernel,
        grid=(n_rows // tb,),
        in_specs=[pl.BlockSpec((None, 1, TOP_K * tb), lambda i: (i, 0, 0), memory_space=pltpu.SMEM),
                  pl.BlockSpec((tb, d), lambda i: (blk0 + i, 0)),
                  pl.BlockSpec((tb, LANES), lambda i: (blk0 + i, 0)),
                  pl.BlockSpec((1, d), lambda i: (0, 0)),
                  pl.BlockSpec(memory_space=pl.ANY)],
        out_specs=pl.BlockSpec((tb, d), lambda i: (i, 0)),
        out_shape=jax.ShapeDtypeStruct((n_rows, d), out_dtype),
        scratch_shapes=[pltpu.VMEM((tb, d), F32), pltpu.VMEM((tb, d), F32),
                        pltpu.SemaphoreType.DMA(())],
        compiler_params=_params(("arbitrary",)),
        name="moe_combine",
    )(pos, x, gates, g.reshape(1, d), y)


def _moe(h, h32, x, w_router, wg, wu, wd, g, out_dtype, m_prompt):
    m, d = x.shape
    tmb = EXPERT_ROWS
    info, gates, cnt = _router(h, w_router)

    counts = cnt[0, :N_EXPERTS].astype(I32)
    padded = (counts + (tmb - 1)) // tmb * tmb
    ends = jnp.cumsum(padded)
    offsets = ends - padded
    n_blocks = -(-(TOP_K * m + N_EXPERTS * (tmb - 1)) // tmb)
    n_valid = ends[-1] // tmb
    starts = jnp.arange(n_blocks, dtype=I32) * tmb
    owner = jnp.sum((starts[:, None] >= ends[None, :]).astype(I32), axis=1)
    blk_expert = jnp.minimum(jnp.where(starts < ends[-1], owner, owner[n_valid - 1]), N_EXPERTS - 1)

    n_valid = n_valid.reshape(1)
    pos, src = _slots(info, offsets, counts, ends, n_blocks * tmb)
    xs = _gather(h32, src, n_valid)
    y = _experts(xs, wg, wu, wd, blk_expert, n_valid, 1024)
    return (_combine(x, y, pos, gates, g, out_dtype, 0, m_prompt),
            _combine(x, y, pos, gates, g, out_dtype, m_prompt, m - m_prompt))


def _pad_ff(w, axis, mult):
    ff = w.shape[axis]
    pad = (-ff) % mult
    if pad:
        widths = [(0, 0)] * w.ndim
        widths[axis] = (0, pad)
        w = jnp.pad(w, widths)
    return w


def kernel(x_prompt, x_sample, state_hgrn, state_rglru, state_conv, hgrn_lb_logits, norm_mix, w_in,
           hgrn_gnorm, rg_conv_w, rg_conv_b, rg_wa, rg_ba, rg_wx, rg_bx, rg_a_param, w_br_a, w_br_b,
           w_out, norm_ffn, ffn_w_gate, ffn_w_up, ffn_w_down, moe_router, moe_w_gate, moe_w_up,
           moe_w_down, norm_final):
    depth = w_in.shape[0]
    batch, seq, d = x_prompt.shape
    n_seq = x_sample.shape[0]
    m_prompt = batch * seq
    m = m_prompt + n_seq

    sm = jax.nn.softmax(hgrn_lb_logits.astype(F32), axis=0)
    lbs = jnp.maximum(jnp.cumsum(sm, axis=0) - sm[0:1], 0.0)
    one_m_lbs = 1.0 - lbs

    x = jnp.concatenate([x_prompt.reshape(m_prompt, d), x_sample.reshape(n_seq, d)], axis=0)
    h = _rmsnorm(x, norm_mix[0], BF16)

    chunk = 256 if seq % 256 == 0 else seq
    rg_rows = 512 if seq % 512 == 0 else chunk
    hg_p, rg_p, cv_p, rg_s, cv_s = [], [], [], [], []
    hg_s = None
    for l in range(depth):
        proj = _in_proj(h, w_in, l)
        vrow = lambda a: a[l].reshape(1, -1)
        o_all, s_prompt = _hgrn_prompt(proj, vrow(lbs), vrow(one_m_lbs), vrow(hgrn_gnorm),
                                       batch, seq, m, chunk)
        o_all, hg_s = _hgrn_sample(proj, vrow(lbs), vrow(one_m_lbs), vrow(hgrn_gnorm),
                                   state_hgrn, l, o_all, m_prompt, hg_s)
        rg_args = (rg_conv_w[l], vrow(rg_conv_b), rg_wa[l], rg_ba[l].reshape(RG_BLOCKS, 1, RG_BW),
                   rg_wx[l], rg_bx[l].reshape(RG_BLOCKS, 1, RG_BW), vrow(rg_a_param))
        y_all, h_prompt = _rg_prompt(proj, *rg_args, batch, seq, m, rg_rows)
        y_all, h_sample = _rg_sample(proj, state_conv[l], state_rglru[l], *rg_args, y_all, m_prompt)

        dense = l % 2 == 0
        merged = _merge_out(o_all, y_all, proj, x, w_br_a[l].astype(BF16), w_br_b[l].astype(BF16),
                            w_out[l].astype(BF16), norm_ffn[l], with_h32=not dense)
        x, h2 = merged[0], merged[1]

        last = l == depth - 1
        g_next = norm_final if last else norm_mix[l + 1]
        out_dtype = F32 if last else BF16
        j = l // 2
        if dense:
            wg = _pad_ff(ffn_w_gate[j], 1, 512).astype(BF16)
            wu = _pad_ff(ffn_w_up[j], 1, 512).astype(BF16)
            wd = _pad_ff(ffn_w_down[j], 0, 512).astype(BF16)
            h, x = _ffn(h2, x, wg, wu, wd, g_next, out_dtype, 512)
        else:
            assert last
            y_prompt, y_sample = _moe(h2, merged[2], x, moe_router[j], moe_w_gate[j].astype(BF16),
                                      moe_w_up[j].astype(BF16), moe_w_down[j].astype(BF16),
                                      g_next, out_dtype, m_prompt)

        xr = proj[:, 2 * HG_F + 2 * HG_I:2 * HG_F + 2 * HG_I + RG_WIDTH]
        hg_p.append(s_prompt)
        rg_p.append(h_prompt.reshape(batch, RG_WIDTH))
        cv_p.append(xr[:m_prompt].reshape(batch, seq, RG_WIDTH)[:, seq - (CONV_W - 1):])
        rg_s.append(h_sample)
        cv_s.append(jnp.concatenate([state_conv[l][:, 1:], xr[m_prompt:, None, :]], axis=1))

    y_prompt = y_prompt.reshape(batch, seq, d)
    y_sample = y_sample.reshape(n_seq, 1, d)
    return (y_prompt, y_sample, jnp.stack(hg_p), jnp.stack(rg_p), jnp.stack(cv_p),
            hg_s, jnp.stack(rg_s), jnp.stack(cv_s))
```

```python
import functools

import jax
import jax.numpy as jnp
from jax import lax
from jax.experimental import pallas as pl
from jax.experimental.pallas import tpu as pltpu

D_MODEL = 2048
HG_HEADS = 8
HG_DK = 128
HG_DV = 128
HG_F = HG_HEADS * HG_DK
HG_I = HG_HEADS * HG_DV
RG_WIDTH = 1024
RG_BLOCKS = 8
RG_BW = RG_WIDTH // RG_BLOCKS
CONV_W = 4
RG_C = 8.0
IN_COLS = 2 * HG_F + 2 * HG_I + 2 * RG_WIDTH + 2 * D_MODEL
N_EXPERTS = 8
TOP_K = 2
EPS = 1e-6

LANES = 128
SUBLANES = 8
BF16_SUBLANES = 16
VMEM_LIMIT = 56 * 1024 * 1024
EXPERT_ROWS = 512
EXPERT_WINDOW_SUBS = 3

BF16 = jnp.bfloat16
F32 = jnp.float32
I32 = jnp.int32


def _params(sem, vmem=VMEM_LIMIT):
    return pltpu.CompilerParams(dimension_semantics=sem, vmem_limit_bytes=vmem)


def _row_tile(m, target):
    best = None
    for t in range(BF16_SUBLANES, min(m, target) + 1, BF16_SUBLANES):
        if m % t == 0:
            best = t
    assert best is not None, (m, target)
    return best


def _dot(a, b):
    return jnp.dot(a, b, preferred_element_type=F32)


def _dot_nt(a, b):
    return lax.dot_general(a, b, (((1,), (1,)), ((), ())), preferred_element_type=F32)


def _dot_tn(a, b):
    return lax.dot_general(a, b, (((0,), (0,)), ((), ())), preferred_element_type=F32)


def _sigmoid(x):
    return 1.0 / (1.0 + jnp.exp(-x))


def _log_sigmoid(x):
    return jnp.minimum(x, 0.0) - jnp.log1p(jnp.exp(-jnp.abs(x)))


def _rms(x, g):
    ms = jnp.mean(x * x, axis=-1, keepdims=True)
    return x * lax.rsqrt(ms + EPS) * g


def _rmsnorm_kernel(x_ref, g_ref, o_ref):
    o_ref[...] = _rms(x_ref[...], g_ref[...]).astype(o_ref.dtype)


def _rmsnorm(x, g, out_dtype):
    m, d = x.shape
    tm = _row_tile(m, 640)
    return pl.pallas_call(
        _rmsnorm_kernel,
        grid=(m // tm,),
        in_specs=[pl.BlockSpec((tm, d), lambda i: (i, 0)),
                  pl.BlockSpec((1, d), lambda i: (0, 0))],
        out_specs=pl.BlockSpec((tm, d), lambda i: (i, 0)),
        out_shape=jax.ShapeDtypeStruct((m, d), out_dtype),
        compiler_params=_params(("parallel",)),
        name="rmsnorm",
    )(x, g.reshape(1, d))


def _in_proj_kernel(x_ref, w_ref, o_ref, wbf_ref):
    @pl.when(pl.program_id(1) == 0)
    def _():
        wbf_ref[...] = w_ref[...].astype(BF16)

    o_ref[...] = _dot(x_ref[...], wbf_ref[...])


def _in_proj(h, w_in, layer):
    m, d = h.shape
    n = w_in.shape[-1]
    tm = _row_tile(m, 1664)
    tn = 1024
    return pl.pallas_call(
        _in_proj_kernel,
        grid=(n // tn, m // tm),
        in_specs=[pl.BlockSpec((tm, d), lambda j, i: (i, 0)),
                  pl.BlockSpec((None, d, tn), lambda j, i: (layer, 0, j))],
        out_specs=pl.BlockSpec((tm, tn), lambda j, i: (i, j)),
        out_shape=jax.ShapeDtypeStruct((m, n), F32),
        scratch_shapes=[pltpu.VMEM((d, tn), BF16)],
        compiler_params=_params(("arbitrary", "arbitrary")),
        name="in_proj",
    )(h, w_in)


def _hgrn_gates(fz, lb, one_m_lb):
    u = jnp.exp(-jnp.abs(fz))
    r = 1.0 / (1.0 + u)
    nonneg = fz >= 0.0
    sig = jnp.where(nonneg, r, u * r)
    sig_neg = jnp.where(nonneg, u * r, r)
    f = lb + one_m_lb * sig
    return jnp.log(f), one_m_lb * sig_neg, f


def _hgrn_out(o, gate, gnorm):
    return _rms(o, gnorm) * (gate * _sigmoid(gate))


def _hgrn_prompt_kernel(q_ref, f_ref, i_ref, g_ref, lb_ref, omlb_ref, gn_ref, o_init_ref,
                        o_ref, s_out_ref, s_ref, *, chunk):
    del o_init_ref
    C = chunk
    c = pl.program_id(2)

    @pl.when(c == 0)
    def _():
        s_ref[...] = jnp.zeros_like(s_ref)

    row = lax.broadcasted_iota(I32, (C, HG_DK), 0)
    rc_xor = lax.broadcasted_iota(I32, (C, C), 0) ^ lax.broadcasted_iota(I32, (C, C), 1)

    for hh in range(HGRN_HEADS_PER_STEP):
        cols = slice(hh * HG_DK, (hh + 1) * HG_DK)
        q = q_ref[:, cols]
        v = i_ref[:, cols].astype(BF16)
        g, k, _ = _hgrn_gates(f_ref[:, cols], lb_ref[:, cols], omlb_ref[:, cols])

        a_mat = jnp.where(rc_xor == 0, _dot_nt(q.astype(BF16), k.astype(BF16)), 0.0)

        pre, suf, tot = g, jnp.zeros_like(g), g
        w = 1
        while w < C:
            upper = (row & w) != 0
            e = jnp.exp(jnp.where(upper, pre, suf))
            qw = jnp.where(upper, q * e, 0.0).astype(BF16)
            kw = jnp.where(upper, 0.0, k * e).astype(BF16)
            a_mat = a_mat + jnp.where(rc_xor < 2 * w, _dot_nt(qw, kw), 0.0)
            up = pltpu.roll(tot, w, 0)
            dn = pltpu.roll(tot, C - w, 0)
            pre = pre + jnp.where(upper, up, 0.0)
            suf = suf + jnp.where(upper, 0.0, dn)
            tot = tot + jnp.where(upper, up, dn)
            w *= 2

        s_old = s_ref[hh]
        o = (_dot(a_mat.astype(BF16), v)
             + _dot((q * jnp.exp(pre)).astype(BF16), s_old.astype(BF16)))
        decay_rows = jnp.exp(jnp.broadcast_to(tot[0:1, :], (HG_DK, HG_DK))).T
        s_new = decay_rows * s_old + _dot_tn((k * jnp.exp(suf)).astype(BF16), v)
        s_ref[hh] = s_new

        o_ref[:, cols] = _hgrn_out(o, g_ref[:, cols], gn_ref[:, cols]).astype(o_ref.dtype)

    @pl.when(c == pl.num_programs(2) - 1)
    def _():
        s_out_ref[...] = s_ref[...]


HGRN_HEADS_PER_STEP = 2


def _hgrn_prompt(proj, lb, one_m_lb, gnorm, batch, seq, m_total, chunk):
    n_chunks = seq // chunk
    hps = HGRN_HEADS_PER_STEP
    width = hps * LANES
    groups = HG_HEADS // hps
    tok = lambda off: pl.BlockSpec((chunk, width), lambda b, h, c: (b * n_chunks + c, off + h))
    vec = pl.BlockSpec((1, width), lambda b, h, c: (0, h))
    return pl.pallas_call(
        functools.partial(_hgrn_prompt_kernel, chunk=chunk),
        grid=(batch, groups, n_chunks),
        in_specs=[tok(0), tok(groups), tok(2 * groups), tok(3 * groups), vec, vec, vec,
                  pl.BlockSpec(memory_space=pl.ANY)],
        input_output_aliases={7: 0},
        out_specs=[pl.BlockSpec((chunk, width), lambda b, h, c: (b * n_chunks + c, h)),
                   pl.BlockSpec((None, hps, HG_DK, HG_DV), lambda b, h, c: (b, h, 0, 0))],
        out_shape=[jax.ShapeDtypeStruct((m_total, HG_I), BF16),
                   jax.ShapeDtypeStruct((batch, HG_HEADS, HG_DK, HG_DV), F32)],
        scratch_shapes=[pltpu.VMEM((hps, HG_DK, HG_DV), F32)],
        compiler_params=_params(("parallel", "parallel", "arbitrary")),
        name="hgrn_prompt",
    )(proj, proj, proj, proj, lb, one_m_lb, gnorm, jnp.zeros((m_total, HG_I), BF16))


HGRN_SAMPLE_ROWS = 16


def _hgrn_sample_kernel(q_ref, f_ref, i_ref, g_ref, lb_ref, omlb_ref, gn_ref, s_ref, o_prev_ref,
                        *rest, layer, first):
    del o_prev_ref
    o_ref, s_out_ref = rest[-2:]
    n = HGRN_SAMPLE_ROWS
    q = q_ref[...]
    vi = i_ref[...]
    _, k, f = _hgrn_gates(f_ref[...], lb_ref[...], omlb_ref[...])

    def columns(x):
        pad = jnp.zeros((LANES - n, HG_DK), F32)
        return jnp.concatenate([x, pad], axis=0).T

    f_t, k_t = columns(f), columns(k)
    qb = q.astype(BF16)
    seq_id = lax.broadcasted_iota(I32, (n, HG_DV), 0)
    o = jnp.zeros((n, HG_DV), F32)
    for j in range(n):
        bc = lambda xt: jnp.broadcast_to(xt[:, j:j + 1], (HG_DK, HG_DV))
        s_new = bc(f_t) * s_ref[j] + bc(k_t) * vi[j:j + 1, :]
        if first:
            for l2 in range(s_out_ref.shape[0]):
                s_out_ref[l2, j] = s_new if l2 == layer else jnp.zeros_like(s_new)
        else:
            s_out_ref[j] = s_new
        o_j = _dot(qb[j:j + 1, :], s_new.astype(BF16))
        o = jnp.where(seq_id == j, o_j, o)
    o_ref[...] = _hgrn_out(o, g_ref[...], gn_ref[...]).astype(o_ref.dtype)


def _hgrn_sample(proj, lb, one_m_lb, gnorm, states, layer, o_all, row0, new_states):
    depth, n_seq = states.shape[:2]
    n = HGRN_SAMPLE_ROWS
    assert n_seq % n == 0 and row0 % n == 0
    blk0 = row0 // n
    first = new_states is None
    tok = lambda off: pl.BlockSpec((n, LANES), lambda j, h: (blk0 + j, off + h))
    vec = pl.BlockSpec((1, LANES), lambda j, h: (0, h))
    one_layer = pl.BlockSpec((None, n, None, HG_DK, HG_DV), lambda j, h: (layer, j, h, 0, 0))
    all_layers = pl.BlockSpec((depth, n, None, HG_DK, HG_DV), lambda j, h: (0, j, h, 0, 0))
    hbm = pl.BlockSpec(memory_space=pl.ANY)
    args = [proj, proj, proj, proj, lb, one_m_lb, gnorm, states, o_all]
    in_specs = [tok(0), tok(HG_HEADS), tok(2 * HG_HEADS), tok(3 * HG_HEADS), vec, vec, vec,
                one_layer, hbm]
    aliases = {8: 0}
    if not first:
        args.append(new_states)
        in_specs.append(hbm)
        aliases[9] = 1
    return pl.pallas_call(
        functools.partial(_hgrn_sample_kernel, layer=layer, first=first),
        grid=(n_seq // n, HG_HEADS),
        in_specs=in_specs,
        out_specs=[pl.BlockSpec((n, LANES), lambda j, h: (blk0 + j, h)),
                   all_layers if first else one_layer],
        out_shape=[jax.ShapeDtypeStruct(o_all.shape, o_all.dtype),
                   jax.ShapeDtypeStruct(states.shape, F32)],
        input_output_aliases=aliases,
        compiler_params=_params(("parallel", "parallel")),
        name="hgrn_sample",
    )(*args)


def _gelu_tanh(x):
    return 0.5 * x * (1.0 + jnp.tanh(0.7978845608028654 * (x + 0.044715 * (x * x * x))))


def _rg_gates(conv, wa_ref, ba_ref, wx_ref, bx_ref, ap_ref):
    cb = conv.astype(BF16)
    r = _sigmoid(_dot(cb, wa_ref[...].astype(BF16)) + ba_ref[...])
    ig = _sigmoid(_dot(cb, wx_ref[...].astype(BF16)) + bx_ref[...])
    log_a = RG_C * r * _log_sigmoid(ap_ref[...])
    a = jnp.exp(log_a)
    t = jnp.tanh(log_a)
    b = jnp.sqrt(-2.0 * t / (1.0 - t)) * ig * conv
    return a, b


def _rg_prompt_kernel(x_ref, gate_ref, cw_ref, cb_ref, wa_ref, ba_ref, wx_ref, bx_ref, ap_ref,
                      y_init_ref, y_ref, h_out_ref, xp_ref, h_ref, *, rows):
    del y_init_ref
    T = rows
    t = pl.program_id(2)

    @pl.when(t == 0)
    def _():
        xp_ref[0:8, :] = jnp.zeros((8, RG_BW), F32)
        h_ref[...] = jnp.zeros_like(h_ref)

    xp_ref[8:8 + T, :] = x_ref[...]
    conv = cb_ref[...]
    for j in range(CONV_W):
        conv = conv + xp_ref[5 + j:5 + j + T, :] * cw_ref[j:j + 1, :]
    xp_ref[0:8, :] = xp_ref[T:T + 8, :]

    a, b = _rg_gates(conv, wa_ref, ba_ref, wx_ref, bx_ref, ap_ref)

    row = lax.broadcasted_iota(I32, (T, RG_BW), 0)
    d = 1
    while d < T:
        ok = row >= d
        b = jnp.where(ok, a * pltpu.roll(b, d, 0) + b, b)
        a = jnp.where(ok, a * pltpu.roll(a, d, 0), a)
        d *= 2
    h = a * h_ref[0:1, :] + b
    h_last = h[T - 1:T, :]
    h_ref[...] = jnp.broadcast_to(h_last, h_ref.shape)

    y_ref[...] = (h * _gelu_tanh(gate_ref[...])).astype(y_ref.dtype)

    @pl.when(t == pl.num_programs(2) - 1)
    def _():
        h_out_ref[...] = h_last


def _rg_prompt(proj, cw, cb, wa, ba, wx, bx, ap, batch, seq, m_total, rows):
    n_t = seq // rows
    xoff = (2 * HG_F + 2 * HG_I) // LANES
    goff = xoff + RG_BLOCKS
    tok = lambda off: pl.BlockSpec((rows, LANES), lambda b, n, t: (b * n_t + t, off + n))
    vec = pl.BlockSpec((1, LANES), lambda b, n, t: (0, n))
    blkw = pl.BlockSpec((None, RG_BW, RG_BW), lambda b, n, t: (n, 0, 0))
    blkb = pl.BlockSpec((None, 1, RG_BW), lambda b, n, t: (n, 0, 0))
    return pl.pallas_call(
        functools.partial(_rg_prompt_kernel, rows=rows),
        grid=(batch, RG_BLOCKS, n_t),
        in_specs=[tok(xoff), tok(goff),
                  pl.BlockSpec((CONV_W, LANES), lambda b, n, t: (0, n)), vec,
                  blkw, blkb, blkw, blkb, vec, pl.BlockSpec(memory_space=pl.ANY)],
        input_output_aliases={9: 0},
        out_specs=[pl.BlockSpec((rows, LANES), lambda b, n, t: (b * n_t + t, n)),
                   pl.BlockSpec((None, 1, LANES), lambda b, n, t: (b, 0, n))],
        out_shape=[jax.ShapeDtypeStruct((m_total, RG_WIDTH), BF16),
                   jax.ShapeDtypeStruct((batch, 1, RG_WIDTH), F32)],
        scratch_shapes=[pltpu.VMEM((rows + 8, RG_BW), F32), pltpu.VMEM((8, RG_BW), F32)],
        compiler_params=_params(("parallel", "parallel", "arbitrary")),
        name="rg_prompt",
    )(proj, proj, cw, cb, wa, ba, wx, bx, ap, jnp.zeros((m_total, RG_WIDTH), BF16))


def _rg_sample_kernel(x_ref, gate_ref, b0_ref, b1_ref, b2_ref, h0_ref, cw_ref, cb_ref,
                      wa_ref, ba_ref, wx_ref, bx_ref, ap_ref, y_prev_ref, y_ref, h_out_ref):
    del y_prev_ref
    conv = cb_ref[...]
    for j, r in enumerate((b0_ref, b1_ref, b2_ref, x_ref)):
        conv = conv + r[...] * cw_ref[j:j + 1, :]
    a, b = _rg_gates(conv, wa_ref, ba_ref, wx_ref, bx_ref, ap_ref)
    h = a * h0_ref[...] + b
    h_out_ref[...] = h
    y_ref[...] = (h * _gelu_tanh(gate_ref[...])).astype(y_ref.dtype)


def _rg_sample(proj, conv_state, h0, cw, cb, wa, ba, wx, bx, ap, y_all, row0):
    n_seq = h0.shape[0]
    assert row0 % n_seq == 0 and n_seq % BF16_SUBLANES == 0
    blk0 = row0 // n_seq
    xoff = (2 * HG_F + 2 * HG_I) // LANES
    goff = xoff + RG_BLOCKS
    tok = lambda off: pl.BlockSpec((n_seq, LANES), lambda n: (blk0, off + n))
    buf = lambda j: pl.BlockSpec((n_seq, LANES), lambda n: (0, j * RG_BLOCKS + n))
    vec = pl.BlockSpec((1, LANES), lambda n: (0, n))
    blkw = pl.BlockSpec((None, RG_BW, RG_BW), lambda n: (n, 0, 0))
    blkb = pl.BlockSpec((None, 1, RG_BW), lambda n: (n, 0, 0))
    flat_state = conv_state.reshape(n_seq, (CONV_W - 1) * RG_WIDTH)
    return pl.pallas_call(
        _rg_sample_kernel,
        grid=(RG_BLOCKS,),
        in_specs=[tok(xoff), tok(goff), buf(0), buf(1), buf(2),
                  pl.BlockSpec((n_seq, LANES), lambda n: (0, n)),
                  pl.BlockSpec((CONV_W, LANES), lambda n: (0, n)), vec,
                  blkw, blkb, blkw, blkb, vec,
                  pl.BlockSpec(memory_space=pl.ANY)],
        out_specs=[pl.BlockSpec((n_seq, LANES), lambda n: (blk0, n)),
                   pl.BlockSpec((n_seq, LANES), lambda n: (0, n))],
        out_shape=[jax.ShapeDtypeStruct(y_all.shape, y_all.dtype),
                   jax.ShapeDtypeStruct((n_seq, RG_WIDTH), F32)],
        input_output_aliases={13: 0},
        compiler_params=_params(("parallel",)),
        name="rg_sample",
    )(proj, proj, flat_state, flat_state, flat_state, h0, cw, cb, wa, ba, wx, bx, ap, y_all)


def _merge_out_kernel(o_ref, y_ref, ma_ref, mb_ref, x_ref, wa_ref, wb_ref, wo_ref, g_ref,
                      xo_ref, h_ref, *maybe_h32_ref):
    merged = (_sigmoid(ma_ref[...]) * _dot(o_ref[...], wa_ref[...])
              + _sigmoid(mb_ref[...]) * _dot(y_ref[...], wb_ref[...]))
    x_new = x_ref[...] + _dot(merged.astype(BF16), wo_ref[...])
    xo_ref[...] = x_new
    h = _rms(x_new, g_ref[...])
    h_ref[...] = h.astype(h_ref.dtype)
    for r in maybe_h32_ref:
        r[...] = h


def _merge_out(o, y, proj, x, w_a, w_b, w_o, g, with_h32):
    m, d = x.shape
    tm = _row_tile(m, 320)
    moff = (2 * HG_F + 2 * HG_I + 2 * RG_WIDTH) // d
    row = lambda width, col: pl.BlockSpec((tm, width), lambda i: (i, col))
    res = lambda shape: pl.BlockSpec(shape, lambda i: (0, 0), pipeline_mode=pl.Buffered(1))
    n_out = 3 if with_h32 else 2
    return pl.pallas_call(
        _merge_out_kernel,
        grid=(m // tm,),
        in_specs=[row(HG_I, 0), row(RG_WIDTH, 0), row(d, moff), row(d, moff + 1), row(d, 0),
                  res(w_a.shape), res(w_b.shape), res(w_o.shape), res((1, d))],
        out_specs=[row(d, 0)] * n_out,
        out_shape=[jax.ShapeDtypeStruct((m, d), F32), jax.ShapeDtypeStruct((m, d), BF16),
                   jax.ShapeDtypeStruct((m, d), F32)][:n_out],
        compiler_params=_params(("parallel",)),
        name="merge_out",
    )(o, y, proj, proj, x, w_a, w_b, w_o, g.reshape(1, d))


def _ffn_kernel(h_ref, x_ref, wg_ref, wu_ref, wd_ref, g_ref, o_ref, xo_ref):
    f = pl.program_id(1)

    @pl.when(f == 0)
    def _():
        xo_ref[...] = jnp.zeros_like(xo_ref)

    h = h_ref[...]
    gate = _dot(h, wg_ref[...])
    act = (gate * _sigmoid(gate)) * _dot(h, wu_ref[...])
    xo_ref[...] += _dot(act.astype(BF16), wd_ref[...])

    @pl.when(f == pl.num_programs(1) - 1)
    def _():
        x_new = x_ref[...] + xo_ref[...]
        xo_ref[...] = x_new
        o_ref[...] = _rms(x_new, g_ref[...]).astype(o_ref.dtype)


def _ffn(h, x, wg, wu, wd, g, out_dtype, tf):
    m, d = x.shape
    ff = wg.shape[1]
    tm = _row_tile(m, 640)
    return pl.pallas_call(
        _ffn_kernel,
        grid=(m // tm, ff // tf),
        in_specs=[pl.BlockSpec((tm, d), lambda i, f: (i, 0)),
                  pl.BlockSpec((tm, d), lambda i, f: (i, 0)),
                  pl.BlockSpec((d, tf), lambda i, f: (0, f)),
                  pl.BlockSpec((d, tf), lambda i, f: (0, f)),
                  pl.BlockSpec((tf, d), lambda i, f: (f, 0)),
                  pl.BlockSpec((1, d), lambda i, f: (0, 0))],
        out_specs=[pl.BlockSpec((tm, d), lambda i, f: (i, 0)),
                   pl.BlockSpec((tm, d), lambda i, f: (i, 0))],
        out_shape=[jax.ShapeDtypeStruct((m, d), out_dtype), jax.ShapeDtypeStruct((m, d), F32)],
        compiler_params=_params(("parallel", "arbitrary")),
        name="ffn_dense",
    )(h, x, wg, wu, wd, g.reshape(1, d))


def _router_kernel(h_ref, w_ref, info_ref, gate_ref, cnt_ref, carry_ref):
    @pl.when(pl.program_id(0) == 0)
    def _():
        carry_ref[...] = jnp.zeros_like(carry_ref)

    logits = _dot(h_ref[...], w_ref[...])
    tm = logits.shape[0]
    lane = lax.broadcasted_iota(I32, logits.shape, 1)
    neg = jnp.float32(-jnp.inf)
    logits = jnp.where(lane < N_EXPERTS, logits, neg)
    m1 = jnp.max(logits, axis=-1, keepdims=True)
    i1 = jnp.min(jnp.where(logits == m1, lane, LANES), axis=-1, keepdims=True)
    rest = jnp.where(lane == i1, neg, logits)
    m2 = jnp.max(rest, axis=-1, keepdims=True)
    i2 = jnp.min(jnp.where(rest == m2, lane, LANES), axis=-1, keepdims=True)
    e = jnp.exp(m2 - m1)
    g1 = 1.0 / (1.0 + e)
    g2 = e / (1.0 + e)
    gate_ref[...] = jnp.where(lane == 0, g1, jnp.where(lane == 1, g2, 0.0))

    chosen = jnp.where((lane == i1) | (lane == i2), 1.0, 0.0)
    lower = jnp.where(lax.broadcasted_iota(I32, (tm, tm), 0) > lax.broadcasted_iota(I32, (tm, tm), 1),
                      1.0, 0.0).astype(BF16)
    before = _dot(lower, chosen.astype(BF16)) + carry_ref[0:1, :]
    r1 = jnp.sum(jnp.where(lane == i1, before, 0.0), axis=-1, keepdims=True).astype(I32)
    r2 = jnp.sum(jnp.where(lane == i2, before, 0.0), axis=-1, keepdims=True).astype(I32)
    info_ref[...] = jnp.where(lane == 0, i1, jnp.where(lane == 1, i2,
                              jnp.where(lane == 2, r1, jnp.where(lane == 3, r2, 0))))
    total = carry_ref[0:1, :] + jnp.sum(chosen, axis=0, keepdims=True)
    carry_ref[...] = jnp.broadcast_to(total, carry_ref.shape)
    cnt_ref[...] = jnp.broadcast_to(total, cnt_ref.shape)


def _router(h, w_router):
    m, d = h.shape
    tm = _row_tile(m, 640)
    w = jnp.pad(w_router, ((0, 0), (0, LANES - N_EXPERTS))).astype(BF16)
    return pl.pallas_call(
        _router_kernel,
        grid=(m // tm,),
        in_specs=[pl.BlockSpec((tm, d), lambda i: (i, 0)),
                  pl.BlockSpec((d, LANES), lambda i: (0, 0))],
        out_specs=[pl.BlockSpec((tm, LANES), lambda i: (i, 0)),
                   pl.BlockSpec((tm, LANES), lambda i: (i, 0)),
                   pl.BlockSpec((SUBLANES, LANES), lambda i: (0, 0))],
        out_shape=[jax.ShapeDtypeStruct((m, LANES), I32),
                   jax.ShapeDtypeStruct((m, LANES), F32),
                   jax.ShapeDtypeStruct((SUBLANES, LANES), F32)],
        scratch_shapes=[pltpu.VMEM((SUBLANES, LANES), F32)],
        compiler_params=_params(("arbitrary",)),
        name="router",
    )(h, w)


def _row_copy(src_ref, src_row, dst_ref, dst_row, sem):
    return pltpu.make_async_copy(src_ref.at[pl.ds(src_row, 1), :], dst_ref.at[pl.ds(dst_row, 1), :], sem)


def _rows_wait(src_ref, dst_ref, n_rows, sem):
    pltpu.make_async_copy(src_ref.at[pl.ds(0, n_rows), :], dst_ref.at[pl.ds(0, n_rows), :], sem).wait()


def _slots_kernel(pos_ref, src_ref):
    tb = pos_ref.shape[-1] // TOP_K
    base = pl.program_id(0) * tb

    @pl.when(pl.program_id(0) == 0)
    def _():
        def clear(p, carry):
            src_ref[p] = 0
            return carry
        lax.fori_loop(0, src_ref.shape[0], clear, 0, unroll=8)

    def body(j, carry):
        for k in range(TOP_K):
            src_ref[pos_ref[0, TOP_K * j + k]] = base + j
        return carry

    lax.fori_loop(0, tb, body, 0, unroll=4)


def _slots(pos, n_rows):
    m = pos.shape[0]
    tb = _row_tile(m, 512)
    return pl.pallas_call(
        _slots_kernel,
        grid=(m // tb,),
        in_specs=[pl.BlockSpec((None, 1, TOP_K * tb), lambda i: (i, 0, 0), memory_space=pltpu.SMEM)],
        out_specs=pl.BlockSpec(memory_space=pltpu.SMEM),
        out_shape=jax.ShapeDtypeStruct((n_rows,), I32),
        compiler_params=_params(("arbitrary",)),
        name="moe_slots",
    )(pos.reshape(m // tb, 1, TOP_K * tb))


def _gather_kernel(ns_ref, src_ref, h_ref, xs_ref, buf_ref, sem):
    n_sub = ns_ref[pl.program_id(0)]
    for s in range(EXPERT_WINDOW_SUBS):
        lo = s * EXPERT_ROWS

        @pl.when(s < n_sub)
        def _():
            def body(r2, carry):
                for p in range(2):
                    r = lo + 2 * r2 + p
                    _row_copy(h_ref, src_ref[0, r], buf_ref, r, sem.at[s]).start(priority=p)
                return carry

            lax.fori_loop(0, EXPERT_ROWS // 2, body, 0, unroll=4)

    for s in range(EXPERT_WINDOW_SUBS):
        lo = s * EXPERT_ROWS
        rows = slice(lo, lo + EXPERT_ROWS)

        @pl.when(s < n_sub)
        def _():
            _rows_wait(h_ref, buf_ref.at[rows, :], EXPERT_ROWS, sem.at[s])
            xs_ref[rows, :] = buf_ref[rows, :].astype(xs_ref.dtype)

        @pl.when(s >= n_sub)
        def _():
            xs_ref[rows, :] = jnp.zeros((EXPERT_ROWS, xs_ref.shape[1]), xs_ref.dtype)


def _gather(h32, src, win_subs):
    m, d = h32.shape
    win = EXPERT_WINDOW_SUBS * EXPERT_ROWS
    assert m >= EXPERT_ROWS
    n_win = src.shape[0] // win
    return pl.pallas_call(
        _gather_kernel,
        grid=(n_win,),
        in_specs=[pl.BlockSpec(memory_space=pltpu.SMEM),
                  pl.BlockSpec((None, 1, win), lambda b: (b, 0, 0), memory_space=pltpu.SMEM),
                  pl.BlockSpec(memory_space=pl.ANY)],
        out_specs=pl.BlockSpec((win, d), lambda b: (b, 0)),
        out_shape=jax.ShapeDtypeStruct((n_win * win, d), BF16),
        scratch_shapes=[pltpu.VMEM((win, d), F32),
                        pltpu.SemaphoreType.DMA((EXPERT_WINDOW_SUBS,))],
        compiler_params=_params(("arbitrary",)),
        name="moe_gather",
    )(win_subs, src.reshape(n_win, 1, win), h32)


def _experts_kernel(we_ref, ns_ref, nv_ref, x_ref, wg_ref, wu_ref, wd_ref, y_ref,
                    wg_bf, wu_bf, wd_bf):
    del we_ref
    b = pl.program_id(0)

    @pl.when(pl.program_id(1) == 0)
    def _():
        y_ref[...] = jnp.zeros_like(y_ref)

    def sub_block(s):
        rows = slice(s * EXPERT_ROWS, (s + 1) * EXPERT_ROWS)
        h = x_ref[rows, :]
        gate = _dot(h, wg_bf[...])
        act = (gate * _sigmoid(gate)) * _dot(h, wu_bf[...])
        y_ref[rows, :] += _dot(act.astype(BF16), wd_bf[...])

    @pl.when(b < nv_ref[0])
    def _():
        wg_bf[...] = wg_ref[...].astype(BF16)
        wu_bf[...] = wu_ref[...].astype(BF16)
        wd_bf[...] = wd_ref[...].astype(BF16)
        sub_block(0)
        for s in range(1, EXPERT_WINDOW_SUBS):
            @pl.when(s < ns_ref[b])
            def _():
                sub_block(s)


def _experts(xs, wg, wu, wd, layer, win_expert, win_subs, n_valid, tf):
    n_rows, d = xs.shape
    ff = wg.shape[-1]
    n_f = ff // tf
    win = EXPERT_WINDOW_SUBS * EXPERT_ROWS
    xmap = lambda b, f, we, ns, nv: (jnp.minimum(b, nv[0] - 1), 0)
    fidx = lambda b, f, nv: jnp.where(b < nv[0], f, n_f - 1)
    once = pl.Buffered(1)
    grid_spec = pltpu.PrefetchScalarGridSpec(
        num_scalar_prefetch=3,
        grid=(n_rows // win, n_f),
        in_specs=[pl.BlockSpec((win, d), xmap, pipeline_mode=once),
                  pl.BlockSpec((None, None, d, tf),
                               lambda b, f, we, ns, nv: (layer, we[b], 0, fidx(b, f, nv))),
                  pl.BlockSpec((None, None, d, tf),
                               lambda b, f, we, ns, nv: (layer, we[b], 0, fidx(b, f, nv))),
                  pl.BlockSpec((None, None, tf, d),
                               lambda b, f, we, ns, nv: (layer, we[b], fidx(b, f, nv), 0))],
        out_specs=pl.BlockSpec((win, d), lambda b, f, we, ns, nv: (b, 0), pipeline_mode=once),
        scratch_shapes=[pltpu.VMEM((d, tf), BF16), pltpu.VMEM((d, tf), BF16),
                        pltpu.VMEM((tf, d), BF16)],
    )
    return pl.pallas_call(
        _experts_kernel,
        grid_spec=grid_spec,
        out_shape=jax.ShapeDtypeStruct((n_rows, d), F32),
        compiler_params=_params(("arbitrary", "arbitrary")),
        name="moe_experts",
    )(win_expert, win_subs, n_valid, xs, wg, wu, wd)


def _combine_kernel(pos_ref, x_ref, gate_ref, g_ref, y_ref, o_ref, ya_ref, yb_ref, sem):
    tb = x_ref.shape[0]

    def body(j, carry):
        _row_copy(y_ref, pos_ref[0, TOP_K * j], ya_ref, j, sem).start(priority=0)
        _row_copy(y_ref, pos_ref[0, TOP_K * j + 1], yb_ref, j, sem).start(priority=1)
        return carry

    lax.fori_loop(0, tb, body, 0, unroll=4)
    _rows_wait(y_ref, ya_ref, tb, sem)
    _rows_wait(y_ref, yb_ref, tb, sem)
    gate = gate_ref[...]
    x_new = x_ref[...] + (gate[:, 0:1] * ya_ref[...] + gate[:, 1:2] * yb_ref[...])
    o_ref[...] = _rms(x_new, g_ref[...]).astype(o_ref.dtype)


def _combine(x, y, pos, gates, g, out_dtype, row0, n_rows):
    d = x.shape[1]
    tb = _row_tile(n_rows, 512)
    assert row0 % tb == 0
    blk0 = row0 // tb
    pos = pos.reshape(-1)[TOP_K * row0:TOP_K * (row0 + n_rows)].reshape(n_rows // tb, 1, TOP_K * tb)
    return pl.pallas_call(
        _combine_kernel,
        grid=(n_rows // tb,),
        in_specs=[pl.BlockSpec((None, 1, TOP_K * tb), lambda i: (i, 0, 0), memory_space=pltpu.SMEM),
                  pl.BlockSpec((tb, d), lambda i: (blk0 + i, 0)),
                  pl.BlockSpec((tb, LANES), lambda i: (blk0 + i, 0)),
                  pl.BlockSpec((1, d), lambda i: (0, 0)),
                  pl.BlockSpec(memory_space=pl.ANY)],
        out_specs=pl.BlockSpec((tb, d), lambda i: (i, 0)),
        out_shape=jax.ShapeDtypeStruct((n_rows, d), out_dtype),
        scratch_shapes=[pltpu.VMEM((tb, d), F32), pltpu.VMEM((tb, d), F32),
                        pltpu.SemaphoreType.DMA(())],
        compiler_params=_params(("arbitrary",)),
        name="moe_combine",
    )(pos, x, gates, g.reshape(1, d), y)


def _moe(h, h32, x, w_router, wg, wu, wd, layer, g, out_dtype, m_prompt):
    m, d = x.shape
    sub, wsubs = EXPERT_ROWS, EXPERT_WINDOW_SUBS
    win = sub * wsubs
    info, gates, cnt = _router(h, w_router)

    counts = cnt[0, :N_EXPERTS].astype(I32)
    n_sub = (counts + (sub - 1)) // sub
    n_win = (n_sub + (wsubs - 1)) // wsubs
    per_win = (n_sub + jnp.maximum(n_win, 1) - 1) // jnp.maximum(n_win, 1)
    win_end = jnp.cumsum(n_win)
    win_first = win_end - n_win
    n_valid = win_end[-1]
    max_windows = -(-(-(-(TOP_K * m) // sub) + N_EXPERTS) // wsubs) + N_EXPERTS
    wid = jnp.arange(max_windows, dtype=I32)
    owner = jnp.minimum(jnp.sum((wid[:, None] >= win_end[None, :]).astype(I32), axis=1),
                        N_EXPERTS - 1)
    subs_here = jnp.clip(n_sub[owner] - (wid - win_first[owner]) * per_win[owner], 0, per_win[owner])
    used = wid < n_valid
    win_expert = jnp.where(used, owner, owner[n_valid - 1])
    win_subs = jnp.where(used, subs_here, 0)

    e_tok, r_tok = info[:, :TOP_K], info[:, TOP_K:2 * TOP_K]
    rows_per_win = jnp.maximum(per_win * sub, 1)[e_tok]
    pos = (win_first[e_tok] + r_tok // rows_per_win) * win + r_tok % rows_per_win

    src = _slots(pos, max_windows * win)
    xs = _gather(h32, src, win_subs)
    y = _experts(xs, wg, wu, wd, layer, win_expert, win_subs, n_valid.reshape(1), 512)
    return (_combine(x, y, pos, gates, g, out_dtype, 0, m_prompt),
            _combine(x, y, pos, gates, g, out_dtype, m_prompt, m - m_prompt))


def _pad_ff(w, axis, mult):
    ff = w.shape[axis]
    pad = (-ff) % mult
    if pad:
        widths = [(0, 0)] * w.ndim
        widths[axis] = (0, pad)
        w = jnp.pad(w, widths)
    return w


def kernel(x_prompt, x_sample, state_hgrn, state_rglru, state_conv, hgrn_lb_logits, norm_mix, w_in,
           hgrn_gnorm, rg_conv_w, rg_conv_b, rg_wa, rg_ba, rg_wx, rg_bx, rg_a_param, w_br_a, w_br_b,
           w_out, norm_ffn, ffn_w_gate, ffn_w_up, ffn_w_down, moe_router, moe_w_gate, moe_w_up,
           moe_w_down, norm_final):
    depth = w_in.shape[0]
    batch, seq, d = x_prompt.shape
    n_seq = x_sample.shape[0]
    m_prompt = batch * seq
    m = m_prompt + n_seq

    sm = jax.nn.softmax(hgrn_lb_logits.astype(F32), axis=0)
    lbs = jnp.maximum(jnp.cumsum(sm, axis=0) - sm[0:1], 0.0)
    one_m_lbs = 1.0 - lbs

    x = jnp.concatenate([x_prompt.reshape(m_prompt, d), x_sample.reshape(n_seq, d)], axis=0)
    h = _rmsnorm(x, norm_mix[0], BF16)

    chunk = 256 if seq % 256 == 0 else seq
    rg_rows = 512 if seq % 512 == 0 else chunk
    hg_p, rg_p, cv_p, rg_s, cv_s = [], [], [], [], []
    hg_s = None
    for l in range(depth):
        proj = _in_proj(h, w_in, l)
        vrow = lambda a: a[l].reshape(1, -1)
        o_all, s_prompt = _hgrn_prompt(proj, vrow(lbs), vrow(one_m_lbs), vrow(hgrn_gnorm),
                                       batch, seq, m, chunk)
        o_all, hg_s = _hgrn_sample(proj, vrow(lbs), vrow(one_m_lbs), vrow(hgrn_gnorm),
                                   state_hgrn, l, o_all, m_prompt, hg_s)
        rg_args = (rg_conv_w[l], vrow(rg_conv_b), rg_wa[l], rg_ba[l].reshape(RG_BLOCKS, 1, RG_BW),
                   rg_wx[l], rg_bx[l].reshape(RG_BLOCKS, 1, RG_BW), vrow(rg_a_param))
        y_all, h_prompt = _rg_prompt(proj, *rg_args, batch, seq, m, rg_rows)
        y_all, h_sample = _rg_sample(proj, state_conv[l], state_rglru[l], *rg_args, y_all, m_prompt)

        dense = l % 2 == 0
        merged = _merge_out(o_all, y_all, proj, x, w_br_a[l].astype(BF16), w_br_b[l].astype(BF16),
                            w_out[l].astype(BF16), norm_ffn[l], with_h32=not dense)
        x, h2 = merged[0], merged[1]

        last = l == depth - 1
        g_next = norm_final if last else norm_mix[l + 1]
        out_dtype = F32 if last else BF16
        j = l // 2
        if dense:
            wg = _pad_ff(ffn_w_gate[j], 1, 512).astype(BF16)
            wu = _pad_ff(ffn_w_up[j], 1, 512).astype(BF16)
            wd = _pad_ff(ffn_w_down[j], 0, 512).astype(BF16)
            h, x = _ffn(h2, x, wg, wu, wd, g_next, out_dtype, 512)
        else:
            assert last
            y_prompt, y_sample = _moe(h2, merged[2], x, moe_router[j], moe_w_gate, moe_w_up,
                                      moe_w_down, j, g_next, out_dtype, m_prompt)

        xr = proj[:, 2 * HG_F + 2 * HG_I:2 * HG_F + 2 * HG_I + RG_WIDTH]
        hg_p.append(s_prompt)
        rg_p.append(h_prompt.reshape(batch, RG_WIDTH))
        cv_p.append(xr[:m_prompt].reshape(batch, seq, RG_WIDTH)[:, seq - (CONV_W - 1):])
        rg_s.append(h_sample)
        cv_s.append(jnp.concatenate([state_conv[l][:, 1:], xr[m_prompt:, None, :]], axis=1))

    y_prompt = y_prompt.reshape(batch, seq, d)
    y_sample = y_sample.reshape(n_seq, 1, d)
    return (y_prompt, y_sample, jnp.stack(hg_p), jnp.stack(rg_p), jnp.stack(cv_p),
            hg_s, jnp.stack(rg_s), jnp.stack(cv_s))
```

```python
import functools

import jax
import jax.numpy as jnp
from jax import lax
from jax.experimental import pallas as pl
from jax.experimental.pallas import tpu as pltpu

D_MODEL = 2048
HG_HEADS = 8
HG_DK = 128
HG_DV = 128
HG_F = HG_HEADS * HG_DK
HG_I = HG_HEADS * HG_DV
RG_WIDTH = 1024
RG_BLOCKS = 8
RG_BW = RG_WIDTH // RG_BLOCKS
CONV_W = 4
RG_C = 8.0
IN_COLS = 2 * HG_F + 2 * HG_I + 2 * RG_WIDTH + 2 * D_MODEL
N_EXPERTS = 8
TOP_K = 2
EPS = 1e-6

LANES = 128
SUBLANES = 8
BF16_SUBLANES = 16
VMEM_LIMIT = 56 * 1024 * 1024
EXPERT_VMEM_LIMIT = 62 * 1024 * 1024
EXPERT_ROWS = 512
EXPERT_WINDOW_SUBS = 3

BF16 = jnp.bfloat16
F32 = jnp.float32
I32 = jnp.int32


def _params(sem, vmem=VMEM_LIMIT):
    return pltpu.CompilerParams(dimension_semantics=sem, vmem_limit_bytes=vmem)


def _row_tile(m, target):
    best = None
    for t in range(BF16_SUBLANES, min(m, target) + 1, BF16_SUBLANES):
        if m % t == 0:
            best = t
    assert best is not None, (m, target)
    return best


def _dot(a, b):
    return jnp.dot(a, b, preferred_element_type=F32)


def _dot_nt(a, b):
    return lax.dot_general(a, b, (((1,), (1,)), ((), ())), preferred_element_type=F32)


def _dot_tn(a, b):
    return lax.dot_general(a, b, (((0,), (0,)), ((), ())), preferred_element_type=F32)


def _sigmoid(x):
    return 1.0 / (1.0 + jnp.exp(-x))


def _log_sigmoid(x):
    return jnp.minimum(x, 0.0) - jnp.log1p(jnp.exp(-jnp.abs(x)))


def _rms(x, g):
    ms = jnp.mean(x * x, axis=-1, keepdims=True)
    return x * lax.rsqrt(ms + EPS) * g


def _rmsnorm_kernel(x_ref, g_ref, o_ref):
    o_ref[...] = _rms(x_ref[...], g_ref[...]).astype(o_ref.dtype)


def _rmsnorm(x, g, out_dtype):
    m, d = x.shape
    tm = _row_tile(m, 640)
    return pl.pallas_call(
        _rmsnorm_kernel,
        grid=(m // tm,),
        in_specs=[pl.BlockSpec((tm, d), lambda i: (i, 0)),
                  pl.BlockSpec((1, d), lambda i: (0, 0))],
        out_specs=pl.BlockSpec((tm, d), lambda i: (i, 0)),
        out_shape=jax.ShapeDtypeStruct((m, d), out_dtype),
        compiler_params=_params(("parallel",)),
        name="rmsnorm",
    )(x, g.reshape(1, d))


def _in_proj_kernel(x_ref, w_ref, o_ref, wbf_ref):
    @pl.when(pl.program_id(1) == 0)
    def _():
        wbf_ref[...] = w_ref[...].astype(BF16)

    o_ref[...] = _dot(x_ref[...], wbf_ref[...])


def _in_proj(h, w_in, layer):
    m, d = h.shape
    n = w_in.shape[-1]
    tm = _row_tile(m, 1664)
    tn = 1024
    return pl.pallas_call(
        _in_proj_kernel,
        grid=(n // tn, m // tm),
        in_specs=[pl.BlockSpec((tm, d), lambda j, i: (i, 0)),
                  pl.BlockSpec((None, d, tn), lambda j, i: (layer, 0, j))],
        out_specs=pl.BlockSpec((tm, tn), lambda j, i: (i, j)),
        out_shape=jax.ShapeDtypeStruct((m, n), F32),
        scratch_shapes=[pltpu.VMEM((d, tn), BF16)],
        compiler_params=_params(("arbitrary", "arbitrary")),
        name="in_proj",
    )(h, w_in)


def _hgrn_gates(fz, lb, one_m_lb):
    u = jnp.exp(-jnp.abs(fz))
    r = 1.0 / (1.0 + u)
    nonneg = fz >= 0.0
    sig = jnp.where(nonneg, r, u * r)
    sig_neg = jnp.where(nonneg, u * r, r)
    f = lb + one_m_lb * sig
    return jnp.log(f), one_m_lb * sig_neg, f


def _hgrn_out(o, gate, gnorm):
    return _rms(o, gnorm) * (gate * _sigmoid(gate))


def _hgrn_prompt_kernel(q_ref, f_ref, i_ref, g_ref, lb_ref, omlb_ref, gn_ref, o_init_ref,
                        o_ref, s_out_ref, s_ref, *, chunk):
    del o_init_ref
    C = chunk
    c = pl.program_id(2)

    @pl.when(c == 0)
    def _():
        s_ref[...] = jnp.zeros_like(s_ref)

    row = lax.broadcasted_iota(I32, (C, HG_DK), 0)
    rc_xor = lax.broadcasted_iota(I32, (C, C), 0) ^ lax.broadcasted_iota(I32, (C, C), 1)

    for hh in range(HGRN_HEADS_PER_STEP):
        cols = slice(hh * HG_DK, (hh + 1) * HG_DK)
        q = q_ref[:, cols]
        v = i_ref[:, cols].astype(BF16)
        g, k, _ = _hgrn_gates(f_ref[:, cols], lb_ref[:, cols], omlb_ref[:, cols])

        a_mat = jnp.where(rc_xor == 0, _dot_nt(q.astype(BF16), k.astype(BF16)), 0.0)

        pre, suf, tot = g, jnp.zeros_like(g), g
        w = 1
        while w < C:
            upper = (row & w) != 0
            e = jnp.exp(jnp.where(upper, pre, suf))
            qw = jnp.where(upper, q * e, 0.0).astype(BF16)
            kw = jnp.where(upper, 0.0, k * e).astype(BF16)
            level = _dot_nt(qw, kw)
            a_mat = a_mat + (level if 2 * w == C else jnp.where(rc_xor < 2 * w, level, 0.0))
            up = pltpu.roll(tot, w, 0)
            dn = pltpu.roll(tot, C - w, 0)
            pre = jnp.where(upper, pre + up, pre)
            suf = jnp.where(upper, suf, suf + dn)
            tot = pre + suf
            w *= 2

        s_old = s_ref[hh]
        o = (_dot(a_mat.astype(BF16), v)
             + _dot((q * jnp.exp(pre)).astype(BF16), s_old.astype(BF16)))
        decay_rows = jnp.exp(jnp.broadcast_to(tot[0:1, :], (HG_DK, HG_DK))).T
        s_new = decay_rows * s_old + _dot_tn((k * jnp.exp(suf)).astype(BF16), v)
        s_ref[hh] = s_new

        o_ref[:, cols] = _hgrn_out(o, g_ref[:, cols], gn_ref[:, cols]).astype(o_ref.dtype)

    @pl.when(c == pl.num_programs(2) - 1)
    def _():
        s_out_ref[...] = s_ref[...]


HGRN_HEADS_PER_STEP = 4


def _hgrn_prompt(proj, lb, one_m_lb, gnorm, batch, seq, m_total, chunk):
    n_chunks = seq // chunk
    hps = HGRN_HEADS_PER_STEP
    width = hps * LANES
    groups = HG_HEADS // hps
    tok = lambda off: pl.BlockSpec((chunk, width), lambda b, h, c: (b * n_chunks + c, off + h))
    vec = pl.BlockSpec((1, width), lambda b, h, c: (0, h))
    return pl.pallas_call(
        functools.partial(_hgrn_prompt_kernel, chunk=chunk),
        grid=(batch, groups, n_chunks),
        in_specs=[tok(0), tok(groups), tok(2 * groups), tok(3 * groups), vec, vec, vec,
                  pl.BlockSpec(memory_space=pl.ANY)],
        input_output_aliases={7: 0},
        out_specs=[pl.BlockSpec((chunk, width), lambda b, h, c: (b * n_chunks + c, h)),
                   pl.BlockSpec((None, hps, HG_DK, HG_DV), lambda b, h, c: (b, h, 0, 0))],
        out_shape=[jax.ShapeDtypeStruct((m_total, HG_I), BF16),
                   jax.ShapeDtypeStruct((batch, HG_HEADS, HG_DK, HG_DV), F32)],
        scratch_shapes=[pltpu.VMEM((hps, HG_DK, HG_DV), F32)],
        compiler_params=_params(("parallel", "parallel", "arbitrary")),
        name="hgrn_prompt",
    )(proj, proj, proj, proj, lb, one_m_lb, gnorm, jnp.zeros((m_total, HG_I), BF16))


HGRN_SAMPLE_ROWS = 16


def _hgrn_sample_kernel(q_ref, f_ref, i_ref, g_ref, lb_ref, omlb_ref, gn_ref, s_ref, o_prev_ref,
                        *rest, layer, first):
    del o_prev_ref
    o_ref, s_out_ref = rest[-2:]
    n = HGRN_SAMPLE_ROWS
    q = q_ref[...]
    vi = i_ref[...]
    _, k, f = _hgrn_gates(f_ref[...], lb_ref[...], omlb_ref[...])

    def columns(x):
        pad = jnp.zeros((LANES - n, HG_DK), F32)
        return jnp.concatenate([x, pad], axis=0).T

    f_t, k_t = columns(f), columns(k)
    qb = q.astype(BF16)
    seq_id = lax.broadcasted_iota(I32, (n, HG_DV), 0)
    o = jnp.zeros((n, HG_DV), F32)
    for j in range(n):
        bc = lambda xt: jnp.broadcast_to(xt[:, j:j + 1], (HG_DK, HG_DV))
        s_new = bc(f_t) * s_ref[j] + bc(k_t) * vi[j:j + 1, :]
        if first:
            for l2 in range(s_out_ref.shape[0]):
                s_out_ref[l2, j] = s_new if l2 == layer else jnp.zeros_like(s_new)
        else:
            s_out_ref[j] = s_new
        o_j = _dot(qb[j:j + 1, :], s_new.astype(BF16))
        o = jnp.where(seq_id == j, o_j, o)
    o_ref[...] = _hgrn_out(o, g_ref[...], gn_ref[...]).astype(o_ref.dtype)


def _hgrn_sample(proj, lb, one_m_lb, gnorm, states, layer, o_all, row0, new_states):
    depth, n_seq = states.shape[:2]
    n = HGRN_SAMPLE_ROWS
    assert n_seq % n == 0 and row0 % n == 0
    blk0 = row0 // n
    first = new_states is None
    tok = lambda off: pl.BlockSpec((n, LANES), lambda j, h: (blk0 + j, off + h))
    vec = pl.BlockSpec((1, LANES), lambda j, h: (0, h))
    one_layer = pl.BlockSpec((None, n, None, HG_DK, HG_DV), lambda j, h: (layer, j, h, 0, 0))
    all_layers = pl.BlockSpec((depth, n, None, HG_DK, HG_DV), lambda j, h: (0, j, h, 0, 0))
    hbm = pl.BlockSpec(memory_space=pl.ANY)
    args = [proj, proj, proj, proj, lb, one_m_lb, gnorm, states, o_all]
    in_specs = [tok(0), tok(HG_HEADS), tok(2 * HG_HEADS), tok(3 * HG_HEADS), vec, vec, vec,
                one_layer, hbm]
    aliases = {8: 0}
    if not first:
        args.append(new_states)
        in_specs.append(hbm)
        aliases[9] = 1
    return pl.pallas_call(
        functools.partial(_hgrn_sample_kernel, layer=layer, first=first),
        grid=(n_seq // n, HG_HEADS),
        in_specs=in_specs,
        out_specs=[pl.BlockSpec((n, LANES), lambda j, h: (blk0 + j, h)),
                   all_layers if first else one_layer],
        out_shape=[jax.ShapeDtypeStruct(o_all.shape, o_all.dtype),
                   jax.ShapeDtypeStruct(states.shape, F32)],
        input_output_aliases=aliases,
        compiler_params=_params(("parallel", "parallel")),
        name="hgrn_sample",
    )(*args)


def _gelu_tanh(x):
    return 0.5 * x * (1.0 + jnp.tanh(0.7978845608028654 * (x + 0.044715 * (x * x * x))))


def _rg_gates(conv, wa_ref, ba_ref, wx_ref, bx_ref, ap_ref):
    cb = conv.astype(BF16)
    r = _sigmoid(_dot(cb, wa_ref[...].astype(BF16)) + ba_ref[...])
    ig = _sigmoid(_dot(cb, wx_ref[...].astype(BF16)) + bx_ref[...])
    log_a = RG_C * r * _log_sigmoid(ap_ref[...])
    a = jnp.exp(log_a)
    t = jnp.tanh(log_a)
    b = jnp.sqrt(-2.0 * t / (1.0 - t)) * ig * conv
    return a, b


def _rg_prompt_kernel(x_ref, gate_ref, cw_ref, cb_ref, wa_ref, ba_ref, wx_ref, bx_ref, ap_ref,
                      y_init_ref, y_ref, h_out_ref, xp_ref, h_ref, *, rows):
    del y_init_ref
    T = rows
    t = pl.program_id(2)

    @pl.when(t == 0)
    def _():
        xp_ref[0:8, :] = jnp.zeros((8, RG_BW), F32)
        h_ref[...] = jnp.zeros_like(h_ref)

    xp_ref[8:8 + T, :] = x_ref[...]
    conv = cb_ref[...]
    for j in range(CONV_W):
        conv = conv + xp_ref[5 + j:5 + j + T, :] * cw_ref[j:j + 1, :]
    xp_ref[0:8, :] = xp_ref[T:T + 8, :]

    a, b = _rg_gates(conv, wa_ref, ba_ref, wx_ref, bx_ref, ap_ref)

    row = lax.broadcasted_iota(I32, (T, RG_BW), 0)
    d = 1
    while d < T:
        ok = row >= d
        b = jnp.where(ok, a * pltpu.roll(b, d, 0) + b, b)
        a = jnp.where(ok, a * pltpu.roll(a, d, 0), a)
        d *= 2
    h = a * h_ref[0:1, :] + b
    h_last = h[T - 1:T, :]
    h_ref[...] = jnp.broadcast_to(h_last, h_ref.shape)

    y_ref[...] = (h * _gelu_tanh(gate_ref[...])).astype(y_ref.dtype)

    @pl.when(t == pl.num_programs(2) - 1)
    def _():
        h_out_ref[...] = h_last


def _rg_prompt(proj, cw, cb, wa, ba, wx, bx, ap, batch, seq, m_total, rows):
    n_t = seq // rows
    xoff = (2 * HG_F + 2 * HG_I) // LANES
    goff = xoff + RG_BLOCKS
    tok = lambda off: pl.BlockSpec((rows, LANES), lambda b, n, t: (b * n_t + t, off + n))
    vec = pl.BlockSpec((1, LANES), lambda b, n, t: (0, n))
    blkw = pl.BlockSpec((None, RG_BW, RG_BW), lambda b, n, t: (n, 0, 0))
    blkb = pl.BlockSpec((None, 1, RG_BW), lambda b, n, t: (n, 0, 0))
    return pl.pallas_call(
        functools.partial(_rg_prompt_kernel, rows=rows),
        grid=(batch, RG_BLOCKS, n_t),
        in_specs=[tok(xoff), tok(goff),
                  pl.BlockSpec((CONV_W, LANES), lambda b, n, t: (0, n)), vec,
                  blkw, blkb, blkw, blkb, vec, pl.BlockSpec(memory_space=pl.ANY)],
        input_output_aliases={9: 0},
        out_specs=[pl.BlockSpec((rows, LANES), lambda b, n, t: (b * n_t + t, n)),
                   pl.BlockSpec((None, 1, LANES), lambda b, n, t: (b, 0, n))],
        out_shape=[jax.ShapeDtypeStruct((m_total, RG_WIDTH), BF16),
                   jax.ShapeDtypeStruct((batch, 1, RG_WIDTH), F32)],
        scratch_shapes=[pltpu.VMEM((rows + 8, RG_BW), F32), pltpu.VMEM((8, RG_BW), F32)],
        compiler_params=_params(("parallel", "parallel", "arbitrary")),
        name="rg_prompt",
    )(proj, proj, cw, cb, wa, ba, wx, bx, ap, jnp.zeros((m_total, RG_WIDTH), BF16))


def _rg_sample_kernel(x_ref, gate_ref, b0_ref, b1_ref, b2_ref, h0_ref, cw_ref, cb_ref,
                      wa_ref, ba_ref, wx_ref, bx_ref, ap_ref, y_prev_ref, y_ref, h_out_ref):
    del y_prev_ref
    conv = cb_ref[...]
    for j, r in enumerate((b0_ref, b1_ref, b2_ref, x_ref)):
        conv = conv + r[...] * cw_ref[j:j + 1, :]
    a, b = _rg_gates(conv, wa_ref, ba_ref, wx_ref, bx_ref, ap_ref)
    h = a * h0_ref[...] + b
    h_out_ref[...] = h
    y_ref[...] = (h * _gelu_tanh(gate_ref[...])).astype(y_ref.dtype)


def _rg_sample(proj, conv_state, h0, cw, cb, wa, ba, wx, bx, ap, y_all, row0):
    n_seq = h0.shape[0]
    assert row0 % n_seq == 0 and n_seq % BF16_SUBLANES == 0
    blk0 = row0 // n_seq
    xoff = (2 * HG_F + 2 * HG_I) // LANES
    goff = xoff + RG_BLOCKS
    tok = lambda off: pl.BlockSpec((n_seq, LANES), lambda n: (blk0, off + n))
    buf = lambda j: pl.BlockSpec((n_seq, LANES), lambda n: (0, j * RG_BLOCKS + n))
    vec = pl.BlockSpec((1, LANES), lambda n: (0, n))
    blkw = pl.BlockSpec((None, RG_BW, RG_BW), lambda n: (n, 0, 0))
    blkb = pl.BlockSpec((None, 1, RG_BW), lambda n: (n, 0, 0))
    flat_state = conv_state.reshape(n_seq, (CONV_W - 1) * RG_WIDTH)
    return pl.pallas_call(
        _rg_sample_kernel,
        grid=(RG_BLOCKS,),
        in_specs=[tok(xoff), tok(goff), buf(0), buf(1), buf(2),
                  pl.BlockSpec((n_seq, LANES), lambda n: (0, n)),
                  pl.BlockSpec((CONV_W, LANES), lambda n: (0, n)), vec,
                  blkw, blkb, blkw, blkb, vec,
                  pl.BlockSpec(memory_space=pl.ANY)],
        out_specs=[pl.BlockSpec((n_seq, LANES), lambda n: (blk0, n)),
                   pl.BlockSpec((n_seq, LANES), lambda n: (0, n))],
        out_shape=[jax.ShapeDtypeStruct(y_all.shape, y_all.dtype),
                   jax.ShapeDtypeStruct((n_seq, RG_WIDTH), F32)],
        input_output_aliases={13: 0},
        compiler_params=_params(("parallel",)),
        name="rg_sample",
    )(proj, proj, flat_state, flat_state, flat_state, h0, cw, cb, wa, ba, wx, bx, ap, y_all)


def _merge_out_kernel(o_ref, y_ref, ma_ref, mb_ref, x_ref, wa_ref, wb_ref, wo_ref, g_ref,
                      xo_ref, h_ref, *maybe_h32_ref):
    merged = (_sigmoid(ma_ref[...]) * _dot(o_ref[...], wa_ref[...])
              + _sigmoid(mb_ref[...]) * _dot(y_ref[...], wb_ref[...]))
    x_new = x_ref[...] + _dot(merged.astype(BF16), wo_ref[...])
    xo_ref[...] = x_new
    h = _rms(x_new, g_ref[...])
    h_ref[...] = h.astype(h_ref.dtype)
    for r in maybe_h32_ref:
        r[...] = h


def _merge_out(o, y, proj, x, w_a, w_b, w_o, g, with_h32):
    m, d = x.shape
    tm = _row_tile(m, 320)
    moff = (2 * HG_F + 2 * HG_I + 2 * RG_WIDTH) // d
    row = lambda width, col: pl.BlockSpec((tm, width), lambda i: (i, col))
    res = lambda shape: pl.BlockSpec(shape, lambda i: (0, 0), pipeline_mode=pl.Buffered(1))
    n_out = 3 if with_h32 else 2
    return pl.pallas_call(
        _merge_out_kernel,
        grid=(m // tm,),
        in_specs=[row(HG_I, 0), row(RG_WIDTH, 0), row(d, moff), row(d, moff + 1), row(d, 0),
                  res(w_a.shape), res(w_b.shape), res(w_o.shape), res((1, d))],
        out_specs=[row(d, 0)] * n_out,
        out_shape=[jax.ShapeDtypeStruct((m, d), F32), jax.ShapeDtypeStruct((m, d), BF16),
                   jax.ShapeDtypeStruct((m, d), F32)][:n_out],
        compiler_params=_params(("parallel",)),
        name="merge_out",
    )(o, y, proj, proj, x, w_a, w_b, w_o, g.reshape(1, d))


def _ffn_kernel(h_ref, x_ref, wg_ref, wu_ref, wd_ref, g_ref, o_ref, xo_ref):
    f = pl.program_id(1)

    @pl.when(f == 0)
    def _():
        xo_ref[...] = jnp.zeros_like(xo_ref)

    h = h_ref[...]
    gate = _dot(h, wg_ref[...])
    act = (gate * _sigmoid(gate)) * _dot(h, wu_ref[...])
    xo_ref[...] += _dot(act.astype(BF16), wd_ref[...])

    @pl.when(f == pl.num_programs(1) - 1)
    def _():
        x_new = x_ref[...] + xo_ref[...]
        xo_ref[...] = x_new
        o_ref[...] = _rms(x_new, g_ref[...]).astype(o_ref.dtype)


def _ffn(h, x, wg, wu, wd, g, out_dtype, tf):
    m, d = x.shape
    ff = wg.shape[1]
    tm = _row_tile(m, 640)
    return pl.pallas_call(
        _ffn_kernel,
        grid=(m // tm, ff // tf),
        in_specs=[pl.BlockSpec((tm, d), lambda i, f: (i, 0)),
                  pl.BlockSpec((tm, d), lambda i, f: (i, 0)),
                  pl.BlockSpec((d, tf), lambda i, f: (0, f)),
                  pl.BlockSpec((d, tf), lambda i, f: (0, f)),
                  pl.BlockSpec((tf, d), lambda i, f: (f, 0)),
                  pl.BlockSpec((1, d), lambda i, f: (0, 0))],
        out_specs=[pl.BlockSpec((tm, d), lambda i, f: (i, 0)),
                   pl.BlockSpec((tm, d), lambda i, f: (i, 0))],
        out_shape=[jax.ShapeDtypeStruct((m, d), out_dtype), jax.ShapeDtypeStruct((m, d), F32)],
        compiler_params=_params(("parallel", "arbitrary")),
        name="ffn_dense",
    )(h, x, wg, wu, wd, g.reshape(1, d))


def _router_kernel(h_ref, w_ref, info_ref, gate_ref, cnt_ref, carry_ref):
    @pl.when(pl.program_id(0) == 0)
    def _():
        carry_ref[...] = jnp.zeros_like(carry_ref)

    logits = _dot(h_ref[...], w_ref[...])
    tm = logits.shape[0]
    lane = lax.broadcasted_iota(I32, logits.shape, 1)
    neg = jnp.float32(-jnp.inf)
    logits = jnp.where(lane < N_EXPERTS, logits, neg)
    m1 = jnp.max(logits, axis=-1, keepdims=True)
    i1 = jnp.min(jnp.where(logits == m1, lane, LANES), axis=-1, keepdims=True)
    rest = jnp.where(lane == i1, neg, logits)
    m2 = jnp.max(rest, axis=-1, keepdims=True)
    i2 = jnp.min(jnp.where(rest == m2, lane, LANES), axis=-1, keepdims=True)
    e = jnp.exp(m2 - m1)
    g1 = 1.0 / (1.0 + e)
    g2 = e / (1.0 + e)
    gate_ref[...] = jnp.where(lane == 0, g1, jnp.where(lane == 1, g2, 0.0))

    chosen = jnp.where((lane == i1) | (lane == i2), 1.0, 0.0)
    lower = jnp.where(lax.broadcasted_iota(I32, (tm, tm), 0) > lax.broadcasted_iota(I32, (tm, tm), 1),
                      1.0, 0.0).astype(BF16)
    before = _dot(lower, chosen.astype(BF16)) + carry_ref[0:1, :]
    r1 = jnp.sum(jnp.where(lane == i1, before, 0.0), axis=-1, keepdims=True).astype(I32)
    r2 = jnp.sum(jnp.where(lane == i2, before, 0.0), axis=-1, keepdims=True).astype(I32)
    info_ref[...] = jnp.where(lane == 0, i1, jnp.where(lane == 1, i2,
                              jnp.where(lane == 2, r1, jnp.where(lane == 3, r2, 0))))
    total = carry_ref[0:1, :] + jnp.sum(chosen, axis=0, keepdims=True)
    carry_ref[...] = jnp.broadcast_to(total, carry_ref.shape)
    cnt_ref[...] = jnp.broadcast_to(total, cnt_ref.shape)


def _router(h, w_router):
    m, d = h.shape
    tm = _row_tile(m, 640)
    w = jnp.pad(w_router, ((0, 0), (0, LANES - N_EXPERTS))).astype(BF16)
    return pl.pallas_call(
        _router_kernel,
        grid=(m // tm,),
        in_specs=[pl.BlockSpec((tm, d), lambda i: (i, 0)),
                  pl.BlockSpec((d, LANES), lambda i: (0, 0))],
        out_specs=[pl.BlockSpec((tm, LANES), lambda i: (i, 0)),
                   pl.BlockSpec((tm, LANES), lambda i: (i, 0)),
                   pl.BlockSpec((SUBLANES, LANES), lambda i: (0, 0))],
        out_shape=[jax.ShapeDtypeStruct((m, LANES), I32),
                   jax.ShapeDtypeStruct((m, LANES), F32),
                   jax.ShapeDtypeStruct((SUBLANES, LANES), F32)],
        scratch_shapes=[pltpu.VMEM((SUBLANES, LANES), F32)],
        compiler_params=_params(("arbitrary",)),
        name="router",
    )(h, w)


def _row_copy(src_ref, src_row, dst_ref, dst_row, sem):
    return pltpu.make_async_copy(src_ref.at[pl.ds(src_row, 1), :], dst_ref.at[pl.ds(dst_row, 1), :], sem)


def _rows_wait(src_ref, dst_ref, n_rows, sem):
    pltpu.make_async_copy(src_ref.at[pl.ds(0, n_rows), :], dst_ref.at[pl.ds(0, n_rows), :], sem).wait()


def _slots_kernel(pos_ref, src_ref):
    tb = pos_ref.shape[-1] // TOP_K
    base = pl.program_id(0) * tb

    @pl.when(pl.program_id(0) == 0)
    def _():
        def clear(p, carry):
            src_ref[p] = 0
            return carry
        lax.fori_loop(0, src_ref.shape[0], clear, 0, unroll=8)

    def body(j, carry):
        for k in range(TOP_K):
            src_ref[pos_ref[0, TOP_K * j + k]] = base + j
        return carry

    lax.fori_loop(0, tb, body, 0, unroll=4)


def _slots(pos, n_rows):
    m = pos.shape[0]
    tb = _row_tile(m, 512)
    return pl.pallas_call(
        _slots_kernel,
        grid=(m // tb,),
        in_specs=[pl.BlockSpec((None, 1, TOP_K * tb), lambda i: (i, 0, 0), memory_space=pltpu.SMEM)],
        out_specs=pl.BlockSpec(memory_space=pltpu.SMEM),
        out_shape=jax.ShapeDtypeStruct((n_rows,), I32),
        compiler_params=_params(("arbitrary",)),
        name="moe_slots",
    )(pos.reshape(m // tb, 1, TOP_K * tb))


def _gather_kernel(ns_ref, src_ref, h_ref, xs_ref, buf_ref, sem):
    n_sub = ns_ref[pl.program_id(0)]
    for s in range(EXPERT_WINDOW_SUBS):
        lo = s * EXPERT_ROWS

        @pl.when(s < n_sub)
        def _():
            def body(r2, carry):
                for p in range(2):
                    r = lo + 2 * r2 + p
                    _row_copy(h_ref, src_ref[0, r], buf_ref, r, sem.at[s]).start(priority=p)
                return carry

            lax.fori_loop(0, EXPERT_ROWS // 2, body, 0, unroll=4)

    for s in range(EXPERT_WINDOW_SUBS):
        lo = s * EXPERT_ROWS
        rows = slice(lo, lo + EXPERT_ROWS)

        @pl.when(s < n_sub)
        def _():
            _rows_wait(h_ref, buf_ref.at[rows, :], EXPERT_ROWS, sem.at[s])
            xs_ref[rows, :] = buf_ref[rows, :].astype(xs_ref.dtype)

        @pl.when(s >= n_sub)
        def _():
            xs_ref[rows, :] = jnp.zeros((EXPERT_ROWS, xs_ref.shape[1]), xs_ref.dtype)


def _gather(h32, src, win_subs):
    m, d = h32.shape
    win = EXPERT_WINDOW_SUBS * EXPERT_ROWS
    assert m >= EXPERT_ROWS
    n_win = src.shape[0] // win
    return pl.pallas_call(
        _gather_kernel,
        grid=(n_win,),
        in_specs=[pl.BlockSpec(memory_space=pltpu.SMEM),
                  pl.BlockSpec((None, 1, win), lambda b: (b, 0, 0), memory_space=pltpu.SMEM),
                  pl.BlockSpec(memory_space=pl.ANY)],
        out_specs=pl.BlockSpec((win, d), lambda b: (b, 0)),
        out_shape=jax.ShapeDtypeStruct((n_win * win, d), BF16),
        scratch_shapes=[pltpu.VMEM((win, d), F32),
                        pltpu.SemaphoreType.DMA((EXPERT_WINDOW_SUBS,))],
        compiler_params=_params(("arbitrary",)),
        name="moe_gather",
    )(win_subs, src.reshape(n_win, 1, win), h32)


def _experts_kernel(we_ref, ns_ref, nv_ref, x_ref, wg_ref, wu_ref, wd_ref, y_ref,
                    wg_bf, wu_bf, wd_bf):
    del we_ref
    b = pl.program_id(0)

    @pl.when(pl.program_id(1) == 0)
    def _():
        y_ref[...] = jnp.zeros_like(y_ref)

    def sub_block(s):
        rows = slice(s * EXPERT_ROWS, (s + 1) * EXPERT_ROWS)
        h = x_ref[rows, :]
        gate = _dot(h, wg_bf[...])
        act = (gate * _sigmoid(gate)) * _dot(h, wu_bf[...])
        y_ref[rows, :] += _dot(act.astype(BF16), wd_bf[...])

    @pl.when(b < nv_ref[0])
    def _():
        wg_bf[...] = wg_ref[...].astype(BF16)
        wu_bf[...] = wu_ref[...].astype(BF16)
        wd_bf[...] = wd_ref[...].astype(BF16)
        sub_block(0)
        for s in range(1, EXPERT_WINDOW_SUBS):
            @pl.when(s < ns_ref[b])
            def _():
                sub_block(s)


def _experts(xs, wg, wu, wd, layer, win_expert, win_subs, n_valid, tf):
    n_rows, d = xs.shape
    ff = wg.shape[-1]
    n_f = ff // tf
    win = EXPERT_WINDOW_SUBS * EXPERT_ROWS
    xmap = lambda b, f, we, ns, nv: (jnp.minimum(b, nv[0] - 1), 0)
    fidx = lambda b, f, nv: jnp.where(b < nv[0], f, n_f - 1)
    once = pl.Buffered(1)
    grid_spec = pltpu.PrefetchScalarGridSpec(
        num_scalar_prefetch=3,
        grid=(n_rows // win, n_f),
        in_specs=[pl.BlockSpec((win, d), xmap),
                  pl.BlockSpec((None, None, d, tf),
                               lambda b, f, we, ns, nv: (layer, we[b], 0, fidx(b, f, nv))),
                  pl.BlockSpec((None, None, d, tf),
                               lambda b, f, we, ns, nv: (layer, we[b], 0, fidx(b, f, nv))),
                  pl.BlockSpec((None, None, tf, d),
                               lambda b, f, we, ns, nv: (layer, we[b], fidx(b, f, nv), 0))],
        out_specs=pl.BlockSpec((win, d), lambda b, f, we, ns, nv: (b, 0), pipeline_mode=once),
        scratch_shapes=[pltpu.VMEM((d, tf), BF16), pltpu.VMEM((d, tf), BF16),
                        pltpu.VMEM((tf, d), BF16)],
    )
    return pl.pallas_call(
        _experts_kernel,
        grid_spec=grid_spec,
        out_shape=jax.ShapeDtypeStruct((n_rows, d), F32),
        compiler_params=_params(("arbitrary", "arbitrary"), vmem=EXPERT_VMEM_LIMIT),
        name="moe_experts",
    )(win_expert, win_subs, n_valid, xs, wg, wu, wd)


def _combine_kernel(pos_ref, x_ref, gate_ref, g_ref, y_ref, o_ref, ya_ref, yb_ref, sem):
    tb = x_ref.shape[0]

    def body(j, carry):
        _row_copy(y_ref, pos_ref[0, TOP_K * j], ya_ref, j, sem).start(priority=0)
        _row_copy(y_ref, pos_ref[0, TOP_K * j + 1], yb_ref, j, sem).start(priority=1)
        return carry

    lax.fori_loop(0, tb, body, 0, unroll=4)
    _rows_wait(y_ref, ya_ref, tb, sem)
    _rows_wait(y_ref, yb_ref, tb, sem)
    gate = gate_ref[...]
    x_new = x_ref[...] + (gate[:, 0:1] * ya_ref[...] + gate[:, 1:2] * yb_ref[...])
    o_ref[...] = _rms(x_new, g_ref[...]).astype(o_ref.dtype)


def _combine(x, y, pos, gates, g, out_dtype, row0, n_rows):
    d = x.shape[1]
    tb = _row_tile(n_rows, 512)
    assert row0 % tb == 0
    blk0 = row0 // tb
    pos = pos.reshape(-1)[TOP_K * row0:TOP_K * (row0 + n_rows)].reshape(n_rows // tb, 1, TOP_K * tb)
    return pl.pallas_call(
        _combine_kernel,
        grid=(n_rows // tb,),
        in_specs=[pl.BlockSpec((None, 1, TOP_K * tb), lambda i: (i, 0, 0), memory_space=pltpu.SMEM),
                  pl.BlockSpec((tb, d), lambda i: (blk0 + i, 0)),
                  pl.BlockSpec((tb, LANES), lambda i: (blk0 + i, 0)),
                  pl.BlockSpec((1, d), lambda i: (0, 0)),
                  pl.BlockSpec(memory_space=pl.ANY)],
        out_specs=pl.BlockSpec((tb, d), lambda i: (i, 0)),
        out_shape=jax.ShapeDtypeStruct((n_rows, d), out_dtype),
        scratch_shapes=[pltpu.VMEM((tb, d), F32), pltpu.VMEM((tb, d), F32),
                        pltpu.SemaphoreType.DMA(())],
        compiler_params=_params(("arbitrary",)),
        name="moe_combine",
    )(pos, x, gates, g.reshape(1, d), y)


def _moe(h, h32, x, w_router, wg, wu, wd, layer, g, out_dtype, m_prompt):
    m, d = x.shape
    sub, wsubs = EXPERT_ROWS, EXPERT_WINDOW_SUBS
    win = sub * wsubs
    info, gates, cnt = _router(h, w_router)

    counts = cnt[0, :N_EXPERTS].astype(I32)
    n_sub = (counts + (sub - 1)) // sub
    n_win = (n_sub + (wsubs - 1)) // wsubs
    per_win = (n_sub + jnp.maximum(n_win, 1) - 1) // jnp.maximum(n_win, 1)
    win_end = jnp.cumsum(n_win)
    win_first = win_end - n_win
    n_valid = win_end[-1]
    max_windows = -(-(-(-(TOP_K * m) // sub) + N_EXPERTS) // wsubs) + N_EXPERTS
    wid = jnp.arange(max_windows, dtype=I32)
    owner = jnp.minimum(jnp.sum((wid[:, None] >= win_end[None, :]).astype(I32), axis=1),
                        N_EXPERTS - 1)
    subs_here = jnp.clip(n_sub[owner] - (wid - win_first[owner]) * per_win[owner], 0, per_win[owner])
    used = wid < n_valid
    win_expert = jnp.where(used, owner, owner[n_valid - 1])
    win_subs = jnp.where(used, subs_here, 0)

    e_tok, r_tok = info[:, :TOP_K], info[:, TOP_K:2 * TOP_K]
    rows_per_win = (per_win * sub)[e_tok]
    max_win_per_expert = -(-(-(-m // sub)) // wsubs)
    win_in_expert = sum((r_tok >= k * rows_per_win).astype(I32) for k in range(1, max_win_per_expert))
    pos = (win_first[e_tok] + win_in_expert) * win + (r_tok - win_in_expert * rows_per_win)

    src = _slots(pos, max_windows * win)
    xs = _gather(h32, src, win_subs)
    y = _experts(xs, wg, wu, wd, layer, win_expert, win_subs, n_valid.reshape(1), 512)
    return (_combine(x, y, pos, gates, g, out_dtype, 0, m_prompt),
            _combine(x, y, pos, gates, g, out_dtype, m_prompt, m - m_prompt))


def _pad_ff(w, axis, mult):
    ff = w.shape[axis]
    pad = (-ff) % mult
    if pad:
        widths = [(0, 0)] * w.ndim
        widths[axis] = (0, pad)
        w = jnp.pad(w, widths)
    return w


def kernel(x_prompt, x_sample, state_hgrn, state_rglru, state_conv, hgrn_lb_logits, norm_mix, w_in,
           hgrn_gnorm, rg_conv_w, rg_conv_b, rg_wa, rg_ba, rg_wx, rg_bx, rg_a_param, w_br_a, w_br_b,
           w_out, norm_ffn, ffn_w_gate, ffn_w_up, ffn_w_down, moe_router, moe_w_gate, moe_w_up,
           moe_w_down, norm_final):
    depth = w_in.shape[0]
    batch, seq, d = x_prompt.shape
    n_seq = x_sample.shape[0]
    m_prompt = batch * seq
    m = m_prompt + n_seq

    sm = jax.nn.softmax(hgrn_lb_logits.astype(F32), axis=0)
    lbs = jnp.maximum(jnp.cumsum(sm, axis=0) - sm[0:1], 0.0)
    one_m_lbs = 1.0 - lbs

    x = jnp.concatenate([x_prompt.reshape(m_prompt, d), x_sample.reshape(n_seq, d)], axis=0)
    h = _rmsnorm(x, norm_mix[0], BF16)

    chunk = 256 if seq % 256 == 0 else seq
    rg_rows = 512 if seq % 512 == 0 else chunk
    hg_p, rg_p, cv_p, rg_s, cv_s = [], [], [], [], []
    hg_s = None
    for l in range(depth):
        proj = _in_proj(h, w_in, l)
        vrow = lambda a: a[l].reshape(1, -1)
        o_all, s_prompt = _hgrn_prompt(proj, vrow(lbs), vrow(one_m_lbs), vrow(hgrn_gnorm),
                                       batch, seq, m, chunk)
        o_all, hg_s = _hgrn_sample(proj, vrow(lbs), vrow(one_m_lbs), vrow(hgrn_gnorm),
                                   state_hgrn, l, o_all, m_prompt, hg_s)
        rg_args = (rg_conv_w[l], vrow(rg_conv_b), rg_wa[l], rg_ba[l].reshape(RG_BLOCKS, 1, RG_BW),
                   rg_wx[l], rg_bx[l].reshape(RG_BLOCKS, 1, RG_BW), vrow(rg_a_param))
        y_all, h_prompt = _rg_prompt(proj, *rg_args, batch, seq, m, rg_rows)
        y_all, h_sample = _rg_sample(proj, state_conv[l], state_rglru[l], *rg_args, y_all, m_prompt)

        dense = l % 2 == 0
        merged = _merge_out(o_all, y_all, proj, x, w_br_a[l].astype(BF16), w_br_b[l].astype(BF16),
                            w_out[l].astype(BF16), norm_ffn[l], with_h32=not dense)
        x, h2 = merged[0], merged[1]

        last = l == depth - 1
        g_next = norm_final if last else norm_mix[l + 1]
        out_dtype = F32 if last else BF16
        j = l // 2
        if dense:
            wg = _pad_ff(ffn_w_gate[j].astype(BF16), 1, 512)
            wu = _pad_ff(ffn_w_up[j].astype(BF16), 1, 512)
            wd = _pad_ff(ffn_w_down[j].astype(BF16), 0, 512)
            h, x = _ffn(h2, x, wg, wu, wd, g_next, out_dtype, 512)
        else:
            assert last
            y_prompt, y_sample = _moe(h2, merged[2], x, moe_router[j], moe_w_gate, moe_w_up,
                                      moe_w_down, j, g_next, out_dtype, m_prompt)

        xr = proj[:, 2 * HG_F + 2 * HG_I:2 * HG_F + 2 * HG_I + RG_WIDTH]
        hg_p.append(s_prompt)
        rg_p.append(h_prompt.reshape(batch, RG_WIDTH))
        cv_p.append(xr[:m_prompt].reshape(batch, seq, RG_WIDTH)[:, seq - (CONV_W - 1):])
        rg_s.append(h_sample)
        cv_s.append(jnp.concatenate([state_conv[l][:, 1:], xr[m_prompt:, None, :]], axis=1))

    y_prompt = y_prompt.reshape(batch, seq, d)
    y_sample = y_sample.reshape(n_seq, 1, d)
    return (y_prompt, y_sample, jnp.stack(hg_p), jnp.stack(rg_p), jnp.stack(cv_p),
            hg_s, jnp.stack(rg_s), jnp.stack(cv_s))
```

```python
import functools

import jax
import jax.numpy as jnp
from jax import lax
from jax.experimental import pallas as pl
from jax.experimental.pallas import tpu as pltpu

D_MODEL = 2048
HG_HEADS = 8
HG_DK = 128
HG_DV = 128
HG_F = HG_HEADS * HG_DK
HG_I = HG_HEADS * HG_DV
RG_WIDTH = 1024
RG_BLOCKS = 8
RG_BW = RG_WIDTH // RG_BLOCKS
CONV_W = 4
RG_C = 8.0
IN_COLS = 2 * HG_F + 2 * HG_I + 2 * RG_WIDTH + 2 * D_MODEL
N_EXPERTS = 8
TOP_K = 2
EPS = 1e-6

LANES = 128
SUBLANES = 8
BF16_SUBLANES = 16
VMEM_LIMIT = 56 * 1024 * 1024
EXPERT_VMEM_LIMIT = 62 * 1024 * 1024
EXPERT_ROWS = 512
EXPERT_WINDOW_SUBS = 3

BF16 = jnp.bfloat16
F32 = jnp.float32
I32 = jnp.int32


def _params(sem, vmem=VMEM_LIMIT):
    return pltpu.CompilerParams(dimension_semantics=sem, vmem_limit_bytes=vmem)


def _row_tile(m, target):
    best = None
    for t in range(BF16_SUBLANES, min(m, target) + 1, BF16_SUBLANES):
        if m % t == 0:
            best = t
    assert best is not None, (m, target)
    return best


def _dot(a, b):
    return jnp.dot(a, b, preferred_element_type=F32)


def _dot_nt(a, b):
    return lax.dot_general(a, b, (((1,), (1,)), ((), ())), preferred_element_type=F32)


def _dot_tn(a, b):
    return lax.dot_general(a, b, (((0,), (0,)), ((), ())), preferred_element_type=F32)


def _sigmoid(x):
    return 1.0 / (1.0 + jnp.exp(-x))


def _log_sigmoid(x):
    return jnp.minimum(x, 0.0) - jnp.log1p(jnp.exp(-jnp.abs(x)))


def _rms(x, g):
    ms = jnp.mean(x * x, axis=-1, keepdims=True)
    return x * lax.rsqrt(ms + EPS) * g


def _rmsnorm_kernel(x_ref, g_ref, o_ref):
    o_ref[...] = _rms(x_ref[...], g_ref[...]).astype(o_ref.dtype)


def _rmsnorm(x, g, out_dtype):
    m, d = x.shape
    tm = _row_tile(m, 640)
    return pl.pallas_call(
        _rmsnorm_kernel,
        grid=(m // tm,),
        in_specs=[pl.BlockSpec((tm, d), lambda i: (i, 0)),
                  pl.BlockSpec((1, d), lambda i: (0, 0))],
        out_specs=pl.BlockSpec((tm, d), lambda i: (i, 0)),
        out_shape=jax.ShapeDtypeStruct((m, d), out_dtype),
        compiler_params=_params(("parallel",)),
        name="rmsnorm",
    )(x, g.reshape(1, d))


def _in_proj_kernel(x_ref, w_ref, o_ref, wbf_ref):
    @pl.when(pl.program_id(1) == 0)
    def _():
        wbf_ref[...] = w_ref[...].astype(BF16)

    o_ref[...] = _dot(x_ref[...], wbf_ref[...])


def _in_proj(h, w_in, layer):
    m, d = h.shape
    n = w_in.shape[-1]
    tm = _row_tile(m, 1664)
    tn = 1024
    return pl.pallas_call(
        _in_proj_kernel,
        grid=(n // tn, m // tm),
        in_specs=[pl.BlockSpec((tm, d), lambda j, i: (i, 0)),
                  pl.BlockSpec((None, d, tn), lambda j, i: (layer, 0, j))],
        out_specs=pl.BlockSpec((tm, tn), lambda j, i: (i, j)),
        out_shape=jax.ShapeDtypeStruct((m, n), F32),
        scratch_shapes=[pltpu.VMEM((d, tn), BF16)],
        compiler_params=_params(("arbitrary", "arbitrary")),
        name="in_proj",
    )(h, w_in)


def _hgrn_gates(fz, lb, one_m_lb):
    u = jnp.exp(-jnp.abs(fz))
    r = 1.0 / (1.0 + u)
    nonneg = fz >= 0.0
    sig = jnp.where(nonneg, r, u * r)
    sig_neg = jnp.where(nonneg, u * r, r)
    f = lb + one_m_lb * sig
    return jnp.log(f), one_m_lb * sig_neg, f


def _hgrn_out(o, gate, gnorm):
    return _rms(o, gnorm) * (gate * _sigmoid(gate))


def _hgrn_prompt_kernel(q_ref, f_ref, i_ref, g_ref, lb_ref, omlb_ref, gn_ref, o_init_ref,
                        o_ref, s_out_ref, s_ref, *, chunk):
    del o_init_ref
    C = chunk
    c = pl.program_id(2)

    @pl.when(c == 0)
    def _():
        s_ref[...] = jnp.zeros_like(s_ref)

    row = lax.broadcasted_iota(I32, (C, HG_DK), 0)
    rc_xor = lax.broadcasted_iota(I32, (C, C), 0) ^ lax.broadcasted_iota(I32, (C, C), 1)

    for hh in range(HGRN_HEADS_PER_STEP):
        cols = slice(hh * HG_DK, (hh + 1) * HG_DK)
        q = q_ref[:, cols]
        v = i_ref[:, cols].astype(BF16)
        g, k, _ = _hgrn_gates(f_ref[:, cols], lb_ref[:, cols], omlb_ref[:, cols])

        a_mat = jnp.where(rc_xor == 0, _dot_nt(q.astype(BF16), k.astype(BF16)), 0.0)

        pre, suf, tot = g, jnp.zeros_like(g), g
        w = 1
        while w < C:
            upper = (row & w) != 0
            e = jnp.exp(jnp.where(upper, pre, suf))
            qw = jnp.where(upper, q * e, 0.0).astype(BF16)
            kw = jnp.where(upper, 0.0, k * e).astype(BF16)
            level = _dot_nt(qw, kw)
            a_mat = a_mat + (level if 2 * w == C else jnp.where(rc_xor < 2 * w, level, 0.0))
            up = pltpu.roll(tot, w, 0)
            dn = pltpu.roll(tot, C - w, 0)
            pre = jnp.where(upper, pre + up, pre)
            suf = jnp.where(upper, suf, suf + dn)
            tot = pre + suf
            w *= 2

        s_old = s_ref[hh]
        o = (_dot(a_mat.astype(BF16), v)
             + _dot((q * jnp.exp(pre)).astype(BF16), s_old.astype(BF16)))
        decay_rows = jnp.exp(jnp.broadcast_to(tot[0:1, :], (HG_DK, HG_DK))).T
        s_new = decay_rows * s_old + _dot_tn((k * jnp.exp(suf)).astype(BF16), v)
        s_ref[hh] = s_new

        o_ref[:, cols] = _hgrn_out(o, g_ref[:, cols], gn_ref[:, cols]).astype(o_ref.dtype)

    @pl.when(c == pl.num_programs(2) - 1)
    def _():
        s_out_ref[...] = s_ref[...]


HGRN_HEADS_PER_STEP = 4


def _hgrn_prompt(proj, lb, one_m_lb, gnorm, batch, seq, m_total, chunk):
    n_chunks = seq // chunk
    hps = HGRN_HEADS_PER_STEP
    width = hps * LANES
    groups = HG_HEADS // hps
    tok = lambda off: pl.BlockSpec((chunk, width), lambda b, h, c: (b * n_chunks + c, off + h))
    vec = pl.BlockSpec((1, width), lambda b, h, c: (0, h))
    return pl.pallas_call(
        functools.partial(_hgrn_prompt_kernel, chunk=chunk),
        grid=(batch, groups, n_chunks),
        in_specs=[tok(0), tok(groups), tok(2 * groups), tok(3 * groups), vec, vec, vec,
                  pl.BlockSpec(memory_space=pl.ANY)],
        input_output_aliases={7: 0},
        out_specs=[pl.BlockSpec((chunk, width), lambda b, h, c: (b * n_chunks + c, h)),
                   pl.BlockSpec((None, hps, HG_DK, HG_DV), lambda b, h, c: (b, h, 0, 0))],
        out_shape=[jax.ShapeDtypeStruct((m_total, HG_I), BF16),
                   jax.ShapeDtypeStruct((batch, HG_HEADS, HG_DK, HG_DV), F32)],
        scratch_shapes=[pltpu.VMEM((hps, HG_DK, HG_DV), F32)],
        compiler_params=_params(("parallel", "parallel", "arbitrary")),
        name="hgrn_prompt",
    )(proj, proj, proj, proj, lb, one_m_lb, gnorm, jnp.zeros((m_total, HG_I), BF16))


HGRN_SAMPLE_ROWS = 16


def _hgrn_sample_kernel(q_ref, f_ref, i_ref, g_ref, lb_ref, omlb_ref, gn_ref, s_ref, o_prev_ref,
                        *rest, layer, first):
    del o_prev_ref
    o_ref, s_out_ref = rest[-2:]
    n = HGRN_SAMPLE_ROWS
    q = q_ref[...]
    vi = i_ref[...]
    _, k, f = _hgrn_gates(f_ref[...], lb_ref[...], omlb_ref[...])

    def columns(x):
        pad = jnp.zeros((LANES - n, HG_DK), F32)
        return jnp.concatenate([x, pad], axis=0).T

    f_t, k_t = columns(f), columns(k)
    qb = q.astype(BF16)
    seq_id = lax.broadcasted_iota(I32, (n, HG_DV), 0)
    o = jnp.zeros((n, HG_DV), F32)
    for j in range(n):
        bc = lambda xt: jnp.broadcast_to(xt[:, j:j + 1], (HG_DK, HG_DV))
        s_new = bc(f_t) * s_ref[j] + bc(k_t) * vi[j:j + 1, :]
        if first:
            for l2 in range(s_out_ref.shape[0]):
                s_out_ref[l2, j] = s_new if l2 == layer else jnp.zeros_like(s_new)
        else:
            s_out_ref[j] = s_new
        o_j = _dot(qb[j:j + 1, :], s_new.astype(BF16))
        o = jnp.where(seq_id == j, o_j, o)
    o_ref[...] = _hgrn_out(o, g_ref[...], gn_ref[...]).astype(o_ref.dtype)


def _hgrn_sample(proj, lb, one_m_lb, gnorm, states, layer, o_all, row0, new_states):
    depth, n_seq = states.shape[:2]
    n = HGRN_SAMPLE_ROWS
    assert n_seq % n == 0 and row0 % n == 0
    blk0 = row0 // n
    first = new_states is None
    tok = lambda off: pl.BlockSpec((n, LANES), lambda j, h: (blk0 + j, off + h))
    vec = pl.BlockSpec((1, LANES), lambda j, h: (0, h))
    one_layer = pl.BlockSpec((None, n, None, HG_DK, HG_DV), lambda j, h: (layer, j, h, 0, 0))
    all_layers = pl.BlockSpec((depth, n, None, HG_DK, HG_DV), lambda j, h: (0, j, h, 0, 0))
    hbm = pl.BlockSpec(memory_space=pl.ANY)
    args = [proj, proj, proj, proj, lb, one_m_lb, gnorm, states, o_all]
    in_specs = [tok(0), tok(HG_HEADS), tok(2 * HG_HEADS), tok(3 * HG_HEADS), vec, vec, vec,
                one_layer, hbm]
    aliases = {8: 0}
    if not first:
        args.append(new_states)
        in_specs.append(hbm)
        aliases[9] = 1
    return pl.pallas_call(
        functools.partial(_hgrn_sample_kernel, layer=layer, first=first),
        grid=(n_seq // n, HG_HEADS),
        in_specs=in_specs,
        out_specs=[pl.BlockSpec((n, LANES), lambda j, h: (blk0 + j, h)),
                   all_layers if first else one_layer],
        out_shape=[jax.ShapeDtypeStruct(o_all.shape, o_all.dtype),
                   jax.ShapeDtypeStruct(states.shape, F32)],
        input_output_aliases=aliases,
        compiler_params=_params(("parallel", "parallel")),
        name="hgrn_sample",
    )(*args)


def _gelu_tanh(x):
    return 0.5 * x * (1.0 + jnp.tanh(0.7978845608028654 * (x + 0.044715 * (x * x * x))))


def _rg_gates(conv, wa, ba, wx, bx, a_param):
    cb = conv.astype(BF16)
    r = _sigmoid(_dot(cb, wa.astype(BF16)) + ba)
    ig = _sigmoid(_dot(cb, wx.astype(BF16)) + bx)
    log_a = RG_C * r * _log_sigmoid(a_param)
    a = jnp.exp(log_a)
    t = jnp.tanh(log_a)
    y = -2.0 * t
    root = y * lax.rsqrt(jnp.maximum(y, 1e-30)) * lax.rsqrt(1.0 - t)
    b = root * ig * conv
    return a, b


def _rg_prompt_kernel(x_ref, gate_ref, cw_ref, cb_ref, wa_ref, ba_ref, wx_ref, bx_ref, ap_ref,
                      y_init_ref, y_ref, h_out_ref, xp_ref, h_ref, *, rows):
    del y_init_ref
    T = rows
    t = pl.program_id(2)

    @pl.when(t == 0)
    def _():
        xp_ref[0:8, :] = jnp.zeros((8, xp_ref.shape[1]), F32)
        h_ref[...] = jnp.zeros_like(h_ref)

    xp_ref[8:8 + T, :] = x_ref[...]
    row = lax.broadcasted_iota(I32, (T, RG_BW), 0)

    for nb in range(RG_BLOCKS_PER_STEP):
        cols = slice(nb * RG_BW, (nb + 1) * RG_BW)
        xp = xp_ref[:, cols]
        conv = cb_ref[:, cols]
        for j in range(CONV_W):
            lag = CONV_W - 1 - j
            tap = xp if lag == 0 else pltpu.roll(xp, lag, 0)
            conv = conv + tap[8:8 + T, :] * cw_ref[j:j + 1, cols]

        a, b = _rg_gates(conv, wa_ref[nb], ba_ref[nb], wx_ref[nb], bx_ref[nb], ap_ref[:, cols])

        first = row == 0
        b = jnp.where(first, a * h_ref[0:1, cols] + b, b)
        a = jnp.where(first, 0.0, a)
        d = 1
        while d < T:
            b = a * pltpu.roll(b, d, 0) + b
            a = a * pltpu.roll(a, d, 0)
            d *= 2
        h = b
        h_last = h[T - 1:T, :]
        h_ref[:, cols] = jnp.broadcast_to(h_last, (h_ref.shape[0], RG_BW))
        y_ref[:, cols] = (h * _gelu_tanh(gate_ref[:, cols])).astype(y_ref.dtype)

    xp_ref[0:8, :] = xp_ref[T:T + 8, :]

    @pl.when(t == pl.num_programs(2) - 1)
    def _():
        h_out_ref[...] = h_ref[0:1, :]


RG_BLOCKS_PER_STEP = 4


def _rg_prompt(proj, cw, cb, wa, ba, wx, bx, ap, batch, seq, m_total, rows):
    n_t = seq // rows
    bps = RG_BLOCKS_PER_STEP
    width = bps * RG_BW
    groups = RG_BLOCKS // bps
    xoff = (2 * HG_F + 2 * HG_I) // width
    goff = xoff + groups
    tok = lambda off: pl.BlockSpec((rows, width), lambda b, n, t: (b * n_t + t, off + n))
    vec = pl.BlockSpec((1, width), lambda b, n, t: (0, n))
    blkw = pl.BlockSpec((bps, RG_BW, RG_BW), lambda b, n, t: (n, 0, 0))
    blkb = pl.BlockSpec((bps, 1, RG_BW), lambda b, n, t: (n, 0, 0))
    return pl.pallas_call(
        functools.partial(_rg_prompt_kernel, rows=rows),
        grid=(batch, groups, n_t),
        in_specs=[tok(xoff), tok(goff),
                  pl.BlockSpec((CONV_W, width), lambda b, n, t: (0, n)), vec,
                  blkw, blkb, blkw, blkb, vec, pl.BlockSpec(memory_space=pl.ANY)],
        input_output_aliases={9: 0},
        out_specs=[pl.BlockSpec((rows, width), lambda b, n, t: (b * n_t + t, n)),
                   pl.BlockSpec((None, 1, width), lambda b, n, t: (b, 0, n))],
        out_shape=[jax.ShapeDtypeStruct((m_total, RG_WIDTH), BF16),
                   jax.ShapeDtypeStruct((batch, 1, RG_WIDTH), F32)],
        scratch_shapes=[pltpu.VMEM((rows + 8, width), F32), pltpu.VMEM((8, width), F32)],
        compiler_params=_params(("parallel", "parallel", "arbitrary")),
        name="rg_prompt",
    )(proj, proj, cw, cb, wa, ba, wx, bx, ap, jnp.zeros((m_total, RG_WIDTH), BF16))


def _rg_sample_kernel(x_ref, gate_ref, b0_ref, b1_ref, b2_ref, h0_ref, cw_ref, cb_ref,
                      wa_ref, ba_ref, wx_ref, bx_ref, ap_ref, y_prev_ref, y_ref, h_out_ref):
    del y_prev_ref
    conv = cb_ref[...]
    for j, r in enumerate((b0_ref, b1_ref, b2_ref, x_ref)):
        conv = conv + r[...] * cw_ref[j:j + 1, :]
    a, b = _rg_gates(conv, wa_ref[...], ba_ref[...], wx_ref[...], bx_ref[...], ap_ref[...])
    h = a * h0_ref[...] + b
    h_out_ref[...] = h
    y_ref[...] = (h * _gelu_tanh(gate_ref[...])).astype(y_ref.dtype)


def _rg_sample(proj, conv_state, h0, cw, cb, wa, ba, wx, bx, ap, y_all, row0):
    n_seq = h0.shape[0]
    assert row0 % n_seq == 0 and n_seq % BF16_SUBLANES == 0
    blk0 = row0 // n_seq
    xoff = (2 * HG_F + 2 * HG_I) // LANES
    goff = xoff + RG_BLOCKS
    tok = lambda off: pl.BlockSpec((n_seq, LANES), lambda n: (blk0, off + n))
    buf = lambda j: pl.BlockSpec((n_seq, LANES), lambda n: (0, j * RG_BLOCKS + n))
    vec = pl.BlockSpec((1, LANES), lambda n: (0, n))
    blkw = pl.BlockSpec((None, RG_BW, RG_BW), lambda n: (n, 0, 0))
    blkb = pl.BlockSpec((None, 1, RG_BW), lambda n: (n, 0, 0))
    flat_state = conv_state.reshape(n_seq, (CONV_W - 1) * RG_WIDTH)
    return pl.pallas_call(
        _rg_sample_kernel,
        grid=(RG_BLOCKS,),
        in_specs=[tok(xoff), tok(goff), buf(0), buf(1), buf(2),
                  pl.BlockSpec((n_seq, LANES), lambda n: (0, n)),
                  pl.BlockSpec((CONV_W, LANES), lambda n: (0, n)), vec,
                  blkw, blkb, blkw, blkb, vec,
                  pl.BlockSpec(memory_space=pl.ANY)],
        out_specs=[pl.BlockSpec((n_seq, LANES), lambda n: (blk0, n)),
                   pl.BlockSpec((n_seq, LANES), lambda n: (0, n))],
        out_shape=[jax.ShapeDtypeStruct(y_all.shape, y_all.dtype),
                   jax.ShapeDtypeStruct((n_seq, RG_WIDTH), F32)],
        input_output_aliases={13: 0},
        compiler_params=_params(("parallel",)),
        name="rg_sample",
    )(proj, proj, flat_state, flat_state, flat_state, h0, cw, cb, wa, ba, wx, bx, ap, y_all)


def _merge_out_kernel(o_ref, y_ref, ma_ref, mb_ref, x_ref, wa_ref, wb_ref, wo_ref, g_ref,
                      xo_ref, h_ref, *maybe_h32_ref):
    merged = (_sigmoid(ma_ref[...]) * _dot(o_ref[...], wa_ref[...])
              + _sigmoid(mb_ref[...]) * _dot(y_ref[...], wb_ref[...]))
    x_new = x_ref[...] + _dot(merged.astype(BF16), wo_ref[...])
    xo_ref[...] = x_new
    h = _rms(x_new, g_ref[...])
    h_ref[...] = h.astype(h_ref.dtype)
    for r in maybe_h32_ref:
        r[...] = h


def _merge_out(o, y, proj, x, w_a, w_b, w_o, g, with_h32):
    m, d = x.shape
    tm = _row_tile(m, 320)
    moff = (2 * HG_F + 2 * HG_I + 2 * RG_WIDTH) // d
    row = lambda width, col: pl.BlockSpec((tm, width), lambda i: (i, col))
    res = lambda shape: pl.BlockSpec(shape, lambda i: (0, 0), pipeline_mode=pl.Buffered(1))
    n_out = 3 if with_h32 else 2
    return pl.pallas_call(
        _merge_out_kernel,
        grid=(m // tm,),
        in_specs=[row(HG_I, 0), row(RG_WIDTH, 0), row(d, moff), row(d, moff + 1), row(d, 0),
                  res(w_a.shape), res(w_b.shape), res(w_o.shape), res((1, d))],
        out_specs=[row(d, 0)] * n_out,
        out_shape=[jax.ShapeDtypeStruct((m, d), F32), jax.ShapeDtypeStruct((m, d), BF16),
                   jax.ShapeDtypeStruct((m, d), F32)][:n_out],
        compiler_params=_params(("parallel",)),
        name="merge_out",
    )(o, y, proj, proj, x, w_a, w_b, w_o, g.reshape(1, d))


def _ffn_kernel(h_ref, x_ref, wg_ref, wu_ref, wd_ref, g_ref, o_ref, xo_ref):
    f = pl.program_id(1)

    @pl.when(f == 0)
    def _():
        xo_ref[...] = jnp.zeros_like(xo_ref)

    h = h_ref[...]
    gate = _dot(h, wg_ref[...])
    act = (gate * _sigmoid(gate)) * _dot(h, wu_ref[...])
    xo_ref[...] += _dot(act.astype(BF16), wd_ref[...])

    @pl.when(f == pl.num_programs(1) - 1)
    def _():
        x_new = x_ref[...] + xo_ref[...]
        xo_ref[...] = x_new
        o_ref[...] = _rms(x_new, g_ref[...]).astype(o_ref.dtype)


def _ffn(h, x, wg, wu, wd, g, out_dtype, tf):
    m, d = x.shape
    ff = wg.shape[1]
    tm = _row_tile(m, 640)
    return pl.pallas_call(
        _ffn_kernel,
        grid=(m // tm, ff // tf),
        in_specs=[pl.BlockSpec((tm, d), lambda i, f: (i, 0)),
                  pl.BlockSpec((tm, d), lambda i, f: (i, 0)),
                  pl.BlockSpec((d, tf), lambda i, f: (0, f)),
                  pl.BlockSpec((d, tf), lambda i, f: (0, f)),
                  pl.BlockSpec((tf, d), lambda i, f: (f, 0)),
                  pl.BlockSpec((1, d), lambda i, f: (0, 0))],
        out_specs=[pl.BlockSpec((tm, d), lambda i, f: (i, 0)),
                   pl.BlockSpec((tm, d), lambda i, f: (i, 0))],
        out_shape=[jax.ShapeDtypeStruct((m, d), out_dtype), jax.ShapeDtypeStruct((m, d), F32)],
        compiler_params=_params(("parallel", "arbitrary")),
        name="ffn_dense",
    )(h, x, wg, wu, wd, g.reshape(1, d))


def _router_kernel(h_ref, w_ref, info_ref, gate_ref, cnt_ref, carry_ref):
    @pl.when(pl.program_id(0) == 0)
    def _():
        carry_ref[...] = jnp.zeros_like(carry_ref)

    logits = _dot(h_ref[...], w_ref[...])
    tm = logits.shape[0]
    lane = lax.broadcasted_iota(I32, logits.shape, 1)
    neg = jnp.float32(-jnp.inf)
    logits = jnp.where(lane < N_EXPERTS, logits, neg)
    m1 = jnp.max(logits, axis=-1, keepdims=True)
    i1 = jnp.min(jnp.where(logits == m1, lane, LANES), axis=-1, keepdims=True)
    rest = jnp.where(lane == i1, neg, logits)
    m2 = jnp.max(rest, axis=-1, keepdims=True)
    i2 = jnp.min(jnp.where(rest == m2, lane, LANES), axis=-1, keepdims=True)
    e = jnp.exp(m2 - m1)
    g1 = 1.0 / (1.0 + e)
    g2 = e / (1.0 + e)
    gate_ref[...] = jnp.where(lane == 0, g1, jnp.where(lane == 1, g2, 0.0))

    chosen = jnp.where((lane == i1) | (lane == i2), 1.0, 0.0)
    lower = jnp.where(lax.broadcasted_iota(I32, (tm, tm), 0) > lax.broadcasted_iota(I32, (tm, tm), 1),
                      1.0, 0.0).astype(BF16)
    before = _dot(lower, chosen.astype(BF16)) + carry_ref[0:1, :]
    r1 = jnp.sum(jnp.where(lane == i1, before, 0.0), axis=-1, keepdims=True).astype(I32)
    r2 = jnp.sum(jnp.where(lane == i2, before, 0.0), axis=-1, keepdims=True).astype(I32)
    info_ref[...] = jnp.where(lane == 0, i1, jnp.where(lane == 1, i2,
                              jnp.where(lane == 2, r1, jnp.where(lane == 3, r2, 0))))
    total = carry_ref[0:1, :] + jnp.sum(chosen, axis=0, keepdims=True)
    carry_ref[...] = jnp.broadcast_to(total, carry_ref.shape)
    cnt_ref[...] = jnp.broadcast_to(total, cnt_ref.shape)


def _router(h, w_router):
    m, d = h.shape
    tm = _row_tile(m, 640)
    w = jnp.pad(w_router, ((0, 0), (0, LANES - N_EXPERTS))).astype(BF16)
    return pl.pallas_call(
        _router_kernel,
        grid=(m // tm,),
        in_specs=[pl.BlockSpec((tm, d), lambda i: (i, 0)),
                  pl.BlockSpec((d, LANES), lambda i: (0, 0))],
        out_specs=[pl.BlockSpec((tm, LANES), lambda i: (i, 0)),
                   pl.BlockSpec((tm, LANES), lambda i: (i, 0)),
                   pl.BlockSpec((SUBLANES, LANES), lambda i: (0, 0))],
        out_shape=[jax.ShapeDtypeStruct((m, LANES), I32),
                   jax.ShapeDtypeStruct((m, LANES), F32),
                   jax.ShapeDtypeStruct((SUBLANES, LANES), F32)],
        scratch_shapes=[pltpu.VMEM((SUBLANES, LANES), F32)],
        compiler_params=_params(("arbitrary",)),
        name="router",
    )(h, w)


def _row_copy(src_ref, src_row, dst_ref, dst_row, sem):
    return pltpu.make_async_copy(src_ref.at[pl.ds(src_row, 1), :], dst_ref.at[pl.ds(dst_row, 1), :], sem)


def _rows_wait(src_ref, dst_ref, n_rows, sem):
    pltpu.make_async_copy(src_ref.at[pl.ds(0, n_rows), :], dst_ref.at[pl.ds(0, n_rows), :], sem).wait()


def _slots_kernel(pos_ref, src_ref):
    tb = pos_ref.shape[-1] // TOP_K
    base = pl.program_id(0) * tb

    @pl.when(pl.program_id(0) == 0)
    def _():
        def clear(p, carry):
            src_ref[p] = 0
            return carry
        lax.fori_loop(0, src_ref.shape[0], clear, 0, unroll=8)

    def body(j, carry):
        for k in range(TOP_K):
            src_ref[pos_ref[0, TOP_K * j + k]] = base + j
        return carry

    lax.fori_loop(0, tb, body, 0, unroll=4)


def _slots(pos, n_rows):
    m = pos.shape[0]
    tb = _row_tile(m, 512)
    return pl.pallas_call(
        _slots_kernel,
        grid=(m // tb,),
        in_specs=[pl.BlockSpec((None, 1, TOP_K * tb), lambda i: (i, 0, 0), memory_space=pltpu.SMEM)],
        out_specs=pl.BlockSpec(memory_space=pltpu.SMEM),
        out_shape=jax.ShapeDtypeStruct((n_rows,), I32),
        compiler_params=_params(("arbitrary",)),
        name="moe_slots",
    )(pos.reshape(m // tb, 1, TOP_K * tb))


def _gather_kernel(ns_ref, src_ref, h_ref, xs_ref, buf_ref, sem):
    n_sub = ns_ref[pl.program_id(0)]
    for s in range(EXPERT_WINDOW_SUBS):
        lo = s * EXPERT_ROWS

        @pl.when(s < n_sub)
        def _():
            def body(r2, carry):
                for p in range(2):
                    r = lo + 2 * r2 + p
                    _row_copy(h_ref, src_ref[0, r], buf_ref, r, sem.at[s]).start(priority=p)
                return carry

            lax.fori_loop(0, EXPERT_ROWS // 2, body, 0, unroll=4)

    for s in range(EXPERT_WINDOW_SUBS):
        lo = s * EXPERT_ROWS
        rows = slice(lo, lo + EXPERT_ROWS)

        @pl.when(s < n_sub)
        def _():
            _rows_wait(h_ref, buf_ref.at[rows, :], EXPERT_ROWS, sem.at[s])
            xs_ref[rows, :] = buf_ref[rows, :].astype(xs_ref.dtype)

        @pl.when(s >= n_sub)
        def _():
            xs_ref[rows, :] = jnp.zeros((EXPERT_ROWS, xs_ref.shape[1]), xs_ref.dtype)


def _gather(h32, src, win_subs):
    m, d = h32.shape
    win = EXPERT_WINDOW_SUBS * EXPERT_ROWS
    assert m >= EXPERT_ROWS
    n_win = src.shape[0] // win
    return pl.pallas_call(
        _gather_kernel,
        grid=(n_win,),
        in_specs=[pl.BlockSpec(memory_space=pltpu.SMEM),
                  pl.BlockSpec((None, 1, win), lambda b: (b, 0, 0), memory_space=pltpu.SMEM),
                  pl.BlockSpec(memory_space=pl.ANY)],
        out_specs=pl.BlockSpec((win, d), lambda b: (b, 0)),
        out_shape=jax.ShapeDtypeStruct((n_win * win, d), BF16),
        scratch_shapes=[pltpu.VMEM((win, d), F32),
                        pltpu.SemaphoreType.DMA((EXPERT_WINDOW_SUBS,))],
        compiler_params=_params(("arbitrary",)),
        name="moe_gather",
    )(win_subs, src.reshape(n_win, 1, win), h32)


def _experts_kernel(we_ref, ns_ref, nv_ref, x_ref, wg_ref, wu_ref, wd_ref, y_ref,
                    wg_bf, wu_bf, wd_bf):
    del we_ref
    b = pl.program_id(0)

    @pl.when(pl.program_id(1) == 0)
    def _():
        y_ref[...] = jnp.zeros_like(y_ref)

    def sub_block(s):
        rows = slice(s * EXPERT_ROWS, (s + 1) * EXPERT_ROWS)
        h = x_ref[rows, :]
        gate = _dot(h, wg_bf[...])
        act = (gate * _sigmoid(gate)) * _dot(h, wu_bf[...])
        y_ref[rows, :] += _dot(act.astype(BF16), wd_bf[...])

    @pl.when(b < nv_ref[0])
    def _():
        wg_bf[...] = wg_ref[...].astype(BF16)
        wu_bf[...] = wu_ref[...].astype(BF16)
        wd_bf[...] = wd_ref[...].astype(BF16)
        sub_block(0)
        for s in range(1, EXPERT_WINDOW_SUBS):
            @pl.when(s < ns_ref[b])
            def _():
                sub_block(s)


def _experts(xs, wg, wu, wd, layer, win_expert, win_subs, n_valid, tf):
    n_rows, d = xs.shape
    ff = wg.shape[-1]
    n_f = ff // tf
    win = EXPERT_WINDOW_SUBS * EXPERT_ROWS
    xmap = lambda b, f, we, ns, nv: (jnp.minimum(b, nv[0] - 1), 0)
    fidx = lambda b, f, nv: jnp.where(b < nv[0], f, n_f - 1)
    once = pl.Buffered(1)
    grid_spec = pltpu.PrefetchScalarGridSpec(
        num_scalar_prefetch=3,
        grid=(n_rows // win, n_f),
        in_specs=[pl.BlockSpec((win, d), xmap),
                  pl.BlockSpec((None, None, d, tf),
                               lambda b, f, we, ns, nv: (layer, we[b], 0, fidx(b, f, nv))),
                  pl.BlockSpec((None, None, d, tf),
                               lambda b, f, we, ns, nv: (layer, we[b], 0, fidx(b, f, nv))),
                  pl.BlockSpec((None, None, tf, d),
                               lambda b, f, we, ns, nv: (layer, we[b], fidx(b, f, nv), 0))],
        out_specs=pl.BlockSpec((win, d), lambda b, f, we, ns, nv: (b, 0), pipeline_mode=once),
        scratch_shapes=[pltpu.VMEM((d, tf), BF16), pltpu.VMEM((d, tf), BF16),
                        pltpu.VMEM((tf, d), BF16)],
    )
    return pl.pallas_call(
        _experts_kernel,
        grid_spec=grid_spec,
        out_shape=jax.ShapeDtypeStruct((n_rows, d), F32),
        compiler_params=_params(("arbitrary", "arbitrary"), vmem=EXPERT_VMEM_LIMIT),
        name="moe_experts",
    )(win_expert, win_subs, n_valid, xs, wg, wu, wd)


def _combine_kernel(pos_ref, x_ref, gate_ref, g_ref, y_ref, o_ref, ya_ref, yb_ref, sem):
    tb = x_ref.shape[0]

    def body(j, carry):
        _row_copy(y_ref, pos_ref[0, TOP_K * j], ya_ref, j, sem).start(priority=0)
        _row_copy(y_ref, pos_ref[0, TOP_K * j + 1], yb_ref, j, sem).start(priority=1)
        return carry

    lax.fori_loop(0, tb, body, 0, unroll=4)
    _rows_wait(y_ref, ya_ref, tb, sem)
    _rows_wait(y_ref, yb_ref, tb, sem)
    gate = gate_ref[...]
    x_new = x_ref[...] + (gate[:, 0:1] * ya_ref[...] + gate[:, 1:2] * yb_ref[...])
    o_ref[...] = _rms(x_new, g_ref[...]).astype(o_ref.dtype)


def _combine(x, y, pos, gates, g, out_dtype, row0, n_rows):
    d = x.shape[1]
    tb = _row_tile(n_rows, 512)
    assert row0 % tb == 0
    blk0 = row0 // tb
    pos = pos.reshape(-1)[TOP_K * row0:TOP_K * (row0 + n_rows)].reshape(n_rows // tb, 1, TOP_K * tb)
    return pl.pallas_call(
        _combine_kernel,
        grid=(n_rows // tb,),
        in_specs=[pl.BlockSpec((None, 1, TOP_K * tb), lambda i: (i, 0, 0), memory_space=pltpu.SMEM),
                  pl.BlockSpec((tb, d), lambda i: (blk0 + i, 0)),
                  pl.BlockSpec((tb, LANES), lambda i: (blk0 + i, 0)),
                  pl.BlockSpec((1, d), lambda i: (0, 0)),
                  pl.BlockSpec(memory_space=pl.ANY)],
        out_specs=pl.BlockSpec((tb, d), lambda i: (i, 0)),
        out_shape=jax.ShapeDtypeStruct((n_rows, d), out_dtype),
        scratch_shapes=[pltpu.VMEM((tb, d), F32), pltpu.VMEM((tb, d), F32),
                        pltpu.SemaphoreType.DMA(())],
        compiler_params=_params(("arbitrary",)),
        name="moe_combine",
    )(pos, x, gates, g.reshape(1, d), y)


def _moe(h, h32, x, w_router, wg, wu, wd, layer, g, out_dtype, m_prompt):
    m, d = x.shape
    sub, wsubs = EXPERT_ROWS, EXPERT_WINDOW_SUBS
    win = sub * wsubs
    info, gates, cnt = _router(h, w_router)

    counts = cnt[0, :N_EXPERTS].astype(I32)
    n_sub = (counts + (sub - 1)) // sub
    n_win = (n_sub + (wsubs - 1)) // wsubs
    per_win = (n_sub + jnp.maximum(n_win, 1) - 1) // jnp.maximum(n_win, 1)
    win_end = jnp.cumsum(n_win)
    win_first = win_end - n_win
    n_valid = win_end[-1]
    max_windows = -(-(-(-(TOP_K * m) // sub) + N_EXPERTS) // wsubs) + N_EXPERTS
    wid = jnp.arange(max_windows, dtype=I32)
    owner = jnp.minimum(jnp.sum((wid[:, None] >= win_end[None, :]).astype(I32), axis=1),
                        N_EXPERTS - 1)
    subs_here = jnp.clip(n_sub[owner] - (wid - win_first[owner]) * per_win[owner], 0, per_win[owner])
    used = wid < n_valid
    win_expert = jnp.where(used, owner, owner[n_valid - 1])
    win_subs = jnp.where(used, subs_here, 0)

    e_tok, r_tok = info[:, :TOP_K], info[:, TOP_K:2 * TOP_K]
    rows_per_win = (per_win * sub)[e_tok]
    max_win_per_expert = -(-(-(-m // sub)) // wsubs)
    win_in_expert = sum((r_tok >= k * rows_per_win).astype(I32) for k in range(1, max_win_per_expert))
    pos = (win_first[e_tok] + win_in_expert) * win + (r_tok - win_in_expert * rows_per_win)

    src = _slots(pos, max_windows * win)
    xs = _gather(h32, src, win_subs)
    y = _experts(xs, wg, wu, wd, layer, win_expert, win_subs, n_valid.reshape(1), 512)
    return (_combine(x, y, pos, gates, g, out_dtype, 0, m_prompt),
            _combine(x, y, pos, gates, g, out_dtype, m_prompt, m - m_prompt))


def _pad_ff(w, axis, mult):
    ff = w.shape[axis]
    pad = (-ff) % mult
    if pad:
        widths = [(0, 0)] * w.ndim
        widths[axis] = (0, pad)
        w = jnp.pad(w, widths)
    return w


def kernel(x_prompt, x_sample, state_hgrn, state_rglru, state_conv, hgrn_lb_logits, norm_mix, w_in,
           hgrn_gnorm, rg_conv_w, rg_conv_b, rg_wa, rg_ba, rg_wx, rg_bx, rg_a_param, w_br_a, w_br_b,
           w_out, norm_ffn, ffn_w_gate, ffn_w_up, ffn_w_down, moe_router, moe_w_gate, moe_w_up,
           moe_w_down, norm_final):
    depth = w_in.shape[0]
    batch, seq, d = x_prompt.shape
    n_seq = x_sample.shape[0]
    m_prompt = batch * seq
    m = m_prompt + n_seq

    sm = jax.nn.softmax(hgrn_lb_logits.astype(F32), axis=0)
    lbs = jnp.maximum(jnp.cumsum(sm, axis=0) - sm[0:1], 0.0)
    one_m_lbs = 1.0 - lbs

    x = jnp.concatenate([x_prompt.reshape(m_prompt, d), x_sample.reshape(n_seq, d)], axis=0)
    h = _rmsnorm(x, norm_mix[0], BF16)

    chunk = 256 if seq % 256 == 0 else seq
    rg_rows = 512 if seq % 512 == 0 else chunk
    hg_p, rg_p, cv_p, rg_s, cv_s = [], [], [], [], []
    hg_s = None
    for l in range(depth):
        proj = _in_proj(h, w_in, l)
        vrow = lambda a: a[l].reshape(1, -1)
        o_all, s_prompt = _hgrn_prompt(proj, vrow(lbs), vrow(one_m_lbs), vrow(hgrn_gnorm),
                                       batch, seq, m, chunk)
        o_all, hg_s = _hgrn_sample(proj, vrow(lbs), vrow(one_m_lbs), vrow(hgrn_gnorm),
                                   state_hgrn, l, o_all, m_prompt, hg_s)
        rg_args = (rg_conv_w[l], vrow(rg_conv_b), rg_wa[l], rg_ba[l].reshape(RG_BLOCKS, 1, RG_BW),
                   rg_wx[l], rg_bx[l].reshape(RG_BLOCKS, 1, RG_BW), vrow(rg_a_param))
        y_all, h_prompt = _rg_prompt(proj, *rg_args, batch, seq, m, rg_rows)
        y_all, h_sample = _rg_sample(proj, state_conv[l], state_rglru[l], *rg_args, y_all, m_prompt)

        dense = l % 2 == 0
        merged = _merge_out(o_all, y_all, proj, x, w_br_a[l].astype(BF16), w_br_b[l].astype(BF16),
                            w_out[l].astype(BF16), norm_ffn[l], with_h32=not dense)
        x, h2 = merged[0], merged[1]

        last = l == depth - 1
        g_next = norm_final if last else norm_mix[l + 1]
        out_dtype = F32 if last else BF16
        j = l // 2
        if dense:
            wg = _pad_ff(ffn_w_gate[j].astype(BF16), 1, 512)
            wu = _pad_ff(ffn_w_up[j].astype(BF16), 1, 512)
            wd = _pad_ff(ffn_w_down[j].astype(BF16), 0, 512)
            h, x = _ffn(h2, x, wg, wu, wd, g_next, out_dtype, 512)
        else:
            assert last
            y_prompt, y_sample = _moe(h2, merged[2], x, moe_router[j], moe_w_gate, moe_w_up,
                                      moe_w_down, j, g_next, out_dtype, m_prompt)

        xr_cols = slice(2 * HG_F + 2 * HG_I, 2 * HG_F + 2 * HG_I + RG_WIDTH)
        tail = proj[:m_prompt].reshape(batch, seq, IN_COLS)[:, seq - (CONV_W - 1):, xr_cols]
        hg_p.append(s_prompt)
        rg_p.append(h_prompt.reshape(batch, RG_WIDTH))
        cv_p.append(tail)
        rg_s.append(h_sample)
        cv_s.append(jnp.concatenate([state_conv[l][:, 1:], proj[m_prompt:, None, xr_cols]], axis=1))

    y_prompt = y_prompt.reshape(batch, seq, d)
    y_sample = y_sample.reshape(n_seq, 1, d)
    return (y_prompt, y_sample, jnp.stack(hg_p), jnp.stack(rg_p), jnp.stack(cv_p),
            hg_s, jnp.stack(rg_s), jnp.stack(cv_s))
```

```python
import functools

import jax
import jax.numpy as jnp
from jax import lax
from jax.experimental import pallas as pl
from jax.experimental.pallas import tpu as pltpu

D_MODEL = 2048
HG_HEADS = 8
HG_DK = 128
HG_DV = 128
HG_F = HG_HEADS * HG_DK
HG_I = HG_HEADS * HG_DV
RG_WIDTH = 1024
RG_BLOCKS = 8
RG_BW = RG_WIDTH // RG_BLOCKS
CONV_W = 4
RG_C = 8.0
IN_COLS = 2 * HG_F + 2 * HG_I + 2 * RG_WIDTH + 2 * D_MODEL
N_EXPERTS = 8
TOP_K = 2
EPS = 1e-6

LANES = 128
SUBLANES = 8
BF16_SUBLANES = 16
VMEM_LIMIT = 56 * 1024 * 1024
EXPERT_VMEM_LIMIT = 62 * 1024 * 1024
EXPERT_ROWS = 512
EXPERT_WINDOW_SUBS = 3

BF16 = jnp.bfloat16
F32 = jnp.float32
I32 = jnp.int32


def _params(sem, vmem=VMEM_LIMIT):
    return pltpu.CompilerParams(dimension_semantics=sem, vmem_limit_bytes=vmem)


def _row_tile(m, target):
    best = None
    for t in range(BF16_SUBLANES, min(m, target) + 1, BF16_SUBLANES):
        if m % t == 0:
            best = t
    assert best is not None, (m, target)
    return best


def _dot(a, b):
    return jnp.dot(a, b, preferred_element_type=F32)


def _dot_nt(a, b):
    return lax.dot_general(a, b, (((1,), (1,)), ((), ())), preferred_element_type=F32)


def _dot_tn(a, b):
    return lax.dot_general(a, b, (((0,), (0,)), ((), ())), preferred_element_type=F32)


def _sigmoid(x):
    return 1.0 / (1.0 + jnp.exp(-x))


def _log_sigmoid(x):
    return jnp.minimum(x, 0.0) - jnp.log1p(jnp.exp(-jnp.abs(x)))


def _rms(x, g):
    ms = jnp.mean(x * x, axis=-1, keepdims=True)
    return x * lax.rsqrt(ms + EPS) * g


def _rmsnorm_kernel(x_ref, g_ref, o_ref):
    o_ref[...] = _rms(x_ref[...], g_ref[...]).astype(o_ref.dtype)


def _rmsnorm(x, g, out_dtype):
    m, d = x.shape
    tm = _row_tile(m, 640)
    return pl.pallas_call(
        _rmsnorm_kernel,
        grid=(m // tm,),
        in_specs=[pl.BlockSpec((tm, d), lambda i: (i, 0)),
                  pl.BlockSpec((1, d), lambda i: (0, 0))],
        out_specs=pl.BlockSpec((tm, d), lambda i: (i, 0)),
        out_shape=jax.ShapeDtypeStruct((m, d), out_dtype),
        compiler_params=_params(("parallel",)),
        name="rmsnorm",
    )(x, g.reshape(1, d))


def _in_proj_kernel(x_ref, w_ref, o_ref, wbf_ref):
    @pl.when(pl.program_id(1) == 0)
    def _():
        wbf_ref[...] = w_ref[...].astype(BF16)

    o_ref[...] = _dot(x_ref[...], wbf_ref[...])


def _in_proj(h, w_in, layer):
    m, d = h.shape
    n = w_in.shape[-1]
    tm = _row_tile(m, 1664)
    tn = 1024
    return pl.pallas_call(
        _in_proj_kernel,
        grid=(n // tn, m // tm),
        in_specs=[pl.BlockSpec((tm, d), lambda j, i: (i, 0)),
                  pl.BlockSpec((None, d, tn), lambda j, i: (layer, 0, j))],
        out_specs=pl.BlockSpec((tm, tn), lambda j, i: (i, j)),
        out_shape=jax.ShapeDtypeStruct((m, n), F32),
        scratch_shapes=[pltpu.VMEM((d, tn), BF16)],
        compiler_params=_params(("arbitrary", "arbitrary")),
        name="in_proj",
    )(h, w_in)


def _hgrn_gates(fz, lb, one_m_lb):
    u = jnp.exp(-jnp.abs(fz))
    r = 1.0 / (1.0 + u)
    nonneg = fz >= 0.0
    sig = jnp.where(nonneg, r, u * r)
    sig_neg = jnp.where(nonneg, u * r, r)
    f = lb + one_m_lb * sig
    return jnp.log(f), one_m_lb * sig_neg, f


def _hgrn_out(o, gate, gnorm):
    return _rms(o, gnorm) * (gate * _sigmoid(gate))


def _hgrn_prompt_kernel(q_ref, f_ref, i_ref, g_ref, lb_ref, omlb_ref, gn_ref, o_init_ref,
                        o_ref, s_out_ref, s_ref, *, chunk):
    del o_init_ref
    C = chunk
    c = pl.program_id(2)

    @pl.when(c == 0)
    def _():
        s_ref[...] = jnp.zeros_like(s_ref)

    row = lax.broadcasted_iota(I32, (C, HG_DK), 0)
    rc_xor = lax.broadcasted_iota(I32, (C, C), 0) ^ lax.broadcasted_iota(I32, (C, C), 1)

    for hh in range(HGRN_HEADS_PER_STEP):
        cols = slice(hh * HG_DK, (hh + 1) * HG_DK)
        q = q_ref[:, cols]
        v = i_ref[:, cols].astype(BF16)
        g, k, _ = _hgrn_gates(f_ref[:, cols], lb_ref[:, cols], omlb_ref[:, cols])

        a_mat = jnp.where(rc_xor == 0, _dot_nt(q.astype(BF16), k.astype(BF16)), 0.0)

        pre, suf, tot = g, jnp.zeros_like(g), g
        w = 1
        while w < C:
            upper = (row & w) != 0
            e = jnp.exp(jnp.where(upper, pre, suf))
            qw = jnp.where(upper, q * e, 0.0).astype(BF16)
            kw = jnp.where(upper, 0.0, k * e).astype(BF16)
            level = _dot_nt(qw, kw)
            a_mat = a_mat + (level if 2 * w == C else jnp.where(rc_xor < 2 * w, level, 0.0))
            up = pltpu.roll(tot, w, 0)
            dn = pltpu.roll(tot, C - w, 0)
            pre = jnp.where(upper, pre + up, pre)
            suf = jnp.where(upper, suf, suf + dn)
            tot = pre + suf
            w *= 2

        s_old = s_ref[hh]
        o = (_dot(a_mat.astype(BF16), v)
             + _dot((q * jnp.exp(pre)).astype(BF16), s_old.astype(BF16)))
        decay_rows = jnp.exp(jnp.broadcast_to(tot[0:1, :], (HG_DK, HG_DK))).T
        s_new = decay_rows * s_old + _dot_tn((k * jnp.exp(suf)).astype(BF16), v)
        s_ref[hh] = s_new

        o_ref[:, cols] = _hgrn_out(o, g_ref[:, cols], gn_ref[:, cols]).astype(o_ref.dtype)

    @pl.when(c == pl.num_programs(2) - 1)
    def _():
        s_out_ref[...] = s_ref[...]


HGRN_HEADS_PER_STEP = 4


def _hgrn_prompt(proj, lb, one_m_lb, gnorm, batch, seq, m_total, chunk):
    n_chunks = seq // chunk
    hps = HGRN_HEADS_PER_STEP
    width = hps * LANES
    groups = HG_HEADS // hps
    tok = lambda off: pl.BlockSpec((chunk, width), lambda b, h, c: (b * n_chunks + c, off + h))
    vec = pl.BlockSpec((1, width), lambda b, h, c: (0, h))
    return pl.pallas_call(
        functools.partial(_hgrn_prompt_kernel, chunk=chunk),
        grid=(batch, groups, n_chunks),
        in_specs=[tok(0), tok(groups), tok(2 * groups), tok(3 * groups), vec, vec, vec,
                  pl.BlockSpec(memory_space=pl.ANY)],
        input_output_aliases={7: 0},
        out_specs=[pl.BlockSpec((chunk, width), lambda b, h, c: (b * n_chunks + c, h)),
                   pl.BlockSpec((None, hps, HG_DK, HG_DV), lambda b, h, c: (b, h, 0, 0))],
        out_shape=[jax.ShapeDtypeStruct((m_total, HG_I), BF16),
                   jax.ShapeDtypeStruct((batch, HG_HEADS, HG_DK, HG_DV), F32)],
        scratch_shapes=[pltpu.VMEM((hps, HG_DK, HG_DV), F32)],
        compiler_params=_params(("parallel", "parallel", "arbitrary")),
        name="hgrn_prompt",
    )(proj, proj, proj, proj, lb, one_m_lb, gnorm, jnp.zeros((m_total, HG_I), BF16))


HGRN_SAMPLE_ROWS = 16


def _hgrn_sample_kernel(q_ref, f_ref, i_ref, g_ref, lb_ref, omlb_ref, gn_ref, s_ref, o_prev_ref,
                        *rest, layer, first):
    del o_prev_ref
    o_ref, s_out_ref = rest[-2:]
    n = HGRN_SAMPLE_ROWS
    q = q_ref[...]
    vi = i_ref[...]
    _, k, f = _hgrn_gates(f_ref[...], lb_ref[...], omlb_ref[...])

    def columns(x):
        pad = jnp.zeros((LANES - n, HG_DK), F32)
        return jnp.concatenate([x, pad], axis=0).T

    f_t, k_t = columns(f), columns(k)
    qb = q.astype(BF16)
    seq_id = lax.broadcasted_iota(I32, (n, HG_DV), 0)
    o = jnp.zeros((n, HG_DV), F32)
    for j in range(n):
        bc = lambda xt: jnp.broadcast_to(xt[:, j:j + 1], (HG_DK, HG_DV))
        s_new = bc(f_t) * s_ref[j] + bc(k_t) * vi[j:j + 1, :]
        if first:
            for l2 in range(s_out_ref.shape[0]):
                s_out_ref[l2, j] = s_new if l2 == layer else jnp.zeros_like(s_new)
        else:
            s_out_ref[j] = s_new
        o_j = _dot(qb[j:j + 1, :], s_new.astype(BF16))
        o = jnp.where(seq_id == j, o_j, o)
    o_ref[...] = _hgrn_out(o, g_ref[...], gn_ref[...]).astype(o_ref.dtype)


def _hgrn_sample(proj, lb, one_m_lb, gnorm, states, layer, o_all, row0, new_states):
    depth, n_seq = states.shape[:2]
    n = HGRN_SAMPLE_ROWS
    assert n_seq % n == 0 and row0 % n == 0
    blk0 = row0 // n
    first = new_states is None
    tok = lambda off: pl.BlockSpec((n, LANES), lambda j, h: (blk0 + j, off + h))
    vec = pl.BlockSpec((1, LANES), lambda j, h: (0, h))
    one_layer = pl.BlockSpec((None, n, None, HG_DK, HG_DV), lambda j, h: (layer, j, h, 0, 0))
    all_layers = pl.BlockSpec((depth, n, None, HG_DK, HG_DV), lambda j, h: (0, j, h, 0, 0))
    hbm = pl.BlockSpec(memory_space=pl.ANY)
    args = [proj, proj, proj, proj, lb, one_m_lb, gnorm, states, o_all]
    in_specs = [tok(0), tok(HG_HEADS), tok(2 * HG_HEADS), tok(3 * HG_HEADS), vec, vec, vec,
                one_layer, hbm]
    aliases = {8: 0}
    if not first:
        args.append(new_states)
        in_specs.append(hbm)
        aliases[9] = 1
    return pl.pallas_call(
        functools.partial(_hgrn_sample_kernel, layer=layer, first=first),
        grid=(n_seq // n, HG_HEADS),
        in_specs=in_specs,
        out_specs=[pl.BlockSpec((n, LANES), lambda j, h: (blk0 + j, h)),
                   all_layers if first else one_layer],
        out_shape=[jax.ShapeDtypeStruct(o_all.shape, o_all.dtype),
                   jax.ShapeDtypeStruct(states.shape, F32)],
        input_output_aliases=aliases,
        compiler_params=_params(("parallel", "parallel")),
        name="hgrn_sample",
    )(*args)


def _gelu_tanh(x):
    return 0.5 * x * (1.0 + jnp.tanh(0.7978845608028654 * (x + 0.044715 * (x * x * x))))


def _rg_gates(conv, wa, ba, wx, bx, a_param):
    cb = conv.astype(BF16)
    r = _sigmoid(_dot(cb, wa.astype(BF16)) + ba)
    ig = _sigmoid(_dot(cb, wx.astype(BF16)) + bx)
    log_a = RG_C * r * _log_sigmoid(a_param)
    a = jnp.exp(log_a)
    t = jnp.tanh(log_a)
    y = -2.0 * t
    root = y * lax.rsqrt(jnp.maximum(y, 1e-30)) * lax.rsqrt(1.0 - t)
    b = root * ig * conv
    return a, b


def _rg_prompt_kernel(x_ref, gate_ref, cw_ref, cb_ref, wa_ref, ba_ref, wx_ref, bx_ref, ap_ref,
                      y_init_ref, y_ref, h_out_ref, xp_ref, h_ref, *, rows):
    del y_init_ref
    T = rows
    t = pl.program_id(2)

    @pl.when(t == 0)
    def _():
        xp_ref[0:8, :] = jnp.zeros((8, xp_ref.shape[1]), F32)
        h_ref[...] = jnp.zeros_like(h_ref)

    xp_ref[8:8 + T, :] = x_ref[...]
    row = lax.broadcasted_iota(I32, (T, RG_BW), 0)

    for nb in range(RG_BLOCKS_PER_STEP):
        cols = slice(nb * RG_BW, (nb + 1) * RG_BW)
        xp = xp_ref[:, cols]
        conv = cb_ref[:, cols]
        for j in range(CONV_W):
            lag = CONV_W - 1 - j
            tap = xp if lag == 0 else pltpu.roll(xp, lag, 0)
            conv = conv + tap[8:8 + T, :] * cw_ref[j:j + 1, cols]

        a, b = _rg_gates(conv, wa_ref[nb], ba_ref[nb], wx_ref[nb], bx_ref[nb], ap_ref[:, cols])

        first = row == 0
        b = jnp.where(first, a * h_ref[0:1, cols] + b, b)
        a = jnp.where(first, 0.0, a)
        d = 1
        while d < T:
            b = a * pltpu.roll(b, d, 0) + b
            a = a * pltpu.roll(a, d, 0)
            d *= 2
        h = b
        h_last = h[T - 1:T, :]
        h_ref[:, cols] = jnp.broadcast_to(h_last, (h_ref.shape[0], RG_BW))
        y_ref[:, cols] = (h * _gelu_tanh(gate_ref[:, cols])).astype(y_ref.dtype)

    xp_ref[0:8, :] = xp_ref[T:T + 8, :]

    @pl.when(t == pl.num_programs(2) - 1)
    def _():
        h_out_ref[...] = h_ref[0:1, :]


RG_BLOCKS_PER_STEP = 4


def _rg_prompt(proj, cw, cb, wa, ba, wx, bx, ap, batch, seq, m_total, rows):
    n_t = seq // rows
    bps = RG_BLOCKS_PER_STEP
    width = bps * RG_BW
    groups = RG_BLOCKS // bps
    xoff = (2 * HG_F + 2 * HG_I) // width
    goff = xoff + groups
    tok = lambda off: pl.BlockSpec((rows, width), lambda b, n, t: (b * n_t + t, off + n))
    vec = pl.BlockSpec((1, width), lambda b, n, t: (0, n))
    blkw = pl.BlockSpec((bps, RG_BW, RG_BW), lambda b, n, t: (n, 0, 0))
    blkb = pl.BlockSpec((bps, 1, RG_BW), lambda b, n, t: (n, 0, 0))
    return pl.pallas_call(
        functools.partial(_rg_prompt_kernel, rows=rows),
        grid=(batch, groups, n_t),
        in_specs=[tok(xoff), tok(goff),
                  pl.BlockSpec((CONV_W, width), lambda b, n, t: (0, n)), vec,
                  blkw, blkb, blkw, blkb, vec, pl.BlockSpec(memory_space=pl.ANY)],
        input_output_aliases={9: 0},
        out_specs=[pl.BlockSpec((rows, width), lambda b, n, t: (b * n_t + t, n)),
                   pl.BlockSpec((None, 1, width), lambda b, n, t: (b, 0, n))],
        out_shape=[jax.ShapeDtypeStruct((m_total, RG_WIDTH), BF16),
                   jax.ShapeDtypeStruct((batch, 1, RG_WIDTH), F32)],
        scratch_shapes=[pltpu.VMEM((rows + 8, width), F32), pltpu.VMEM((8, width), F32)],
        compiler_params=_params(("parallel", "parallel", "arbitrary")),
        name="rg_prompt",
    )(proj, proj, cw, cb, wa, ba, wx, bx, ap, jnp.zeros((m_total, RG_WIDTH), BF16))


def _rg_sample_kernel(x_ref, gate_ref, b0_ref, b1_ref, b2_ref, h0_ref, cw_ref, cb_ref,
                      wa_ref, ba_ref, wx_ref, bx_ref, ap_ref, y_prev_ref, y_ref, h_out_ref):
    del y_prev_ref
    conv = cb_ref[...]
    for j, r in enumerate((b0_ref, b1_ref, b2_ref, x_ref)):
        conv = conv + r[...] * cw_ref[j:j + 1, :]
    a, b = _rg_gates(conv, wa_ref[...], ba_ref[...], wx_ref[...], bx_ref[...], ap_ref[...])
    h = a * h0_ref[...] + b
    h_out_ref[...] = h
    y_ref[...] = (h * _gelu_tanh(gate_ref[...])).astype(y_ref.dtype)


def _rg_sample(proj, conv_state, h0, cw, cb, wa, ba, wx, bx, ap, y_all, row0):
    n_seq = h0.shape[0]
    assert row0 % n_seq == 0 and n_seq % BF16_SUBLANES == 0
    blk0 = row0 // n_seq
    xoff = (2 * HG_F + 2 * HG_I) // LANES
    goff = xoff + RG_BLOCKS
    tok = lambda off: pl.BlockSpec((n_seq, LANES), lambda n: (blk0, off + n))
    buf = lambda j: pl.BlockSpec((n_seq, LANES), lambda n: (0, j * RG_BLOCKS + n))
    vec = pl.BlockSpec((1, LANES), lambda n: (0, n))
    blkw = pl.BlockSpec((None, RG_BW, RG_BW), lambda n: (n, 0, 0))
    blkb = pl.BlockSpec((None, 1, RG_BW), lambda n: (n, 0, 0))
    flat_state = conv_state.reshape(n_seq, (CONV_W - 1) * RG_WIDTH)
    return pl.pallas_call(
        _rg_sample_kernel,
        grid=(RG_BLOCKS,),
        in_specs=[tok(xoff), tok(goff), buf(0), buf(1), buf(2),
                  pl.BlockSpec((n_seq, LANES), lambda n: (0, n)),
                  pl.BlockSpec((CONV_W, LANES), lambda n: (0, n)), vec,
                  blkw, blkb, blkw, blkb, vec,
                  pl.BlockSpec(memory_space=pl.ANY)],
        out_specs=[pl.BlockSpec((n_seq, LANES), lambda n: (blk0, n)),
                   pl.BlockSpec((n_seq, LANES), lambda n: (0, n))],
        out_shape=[jax.ShapeDtypeStruct(y_all.shape, y_all.dtype),
                   jax.ShapeDtypeStruct((n_seq, RG_WIDTH), F32)],
        input_output_aliases={13: 0},
        compiler_params=_params(("parallel",)),
        name="rg_sample",
    )(proj, proj, flat_state, flat_state, flat_state, h0, cw, cb, wa, ba, wx, bx, ap, y_all)


def _merge_out_kernel(o_ref, y_ref, ma_ref, mb_ref, x_ref, wa_ref, wb_ref, wo_ref, g_ref,
                      xo_ref, h_ref, *maybe_h32_ref):
    merged = (_sigmoid(ma_ref[...]) * _dot(o_ref[...], wa_ref[...])
              + _sigmoid(mb_ref[...]) * _dot(y_ref[...], wb_ref[...]))
    x_new = x_ref[...] + _dot(merged.astype(BF16), wo_ref[...])
    xo_ref[...] = x_new
    h = _rms(x_new, g_ref[...])
    h_ref[...] = h.astype(h_ref.dtype)
    for r in maybe_h32_ref:
        r[...] = h


def _merge_out(o, y, proj, x, w_a, w_b, w_o, g, with_h32):
    m, d = x.shape
    tm = _row_tile(m, 320)
    moff = (2 * HG_F + 2 * HG_I + 2 * RG_WIDTH) // d
    row = lambda width, col: pl.BlockSpec((tm, width), lambda i: (i, col))
    res = lambda shape: pl.BlockSpec(shape, lambda i: (0, 0), pipeline_mode=pl.Buffered(1))
    n_out = 3 if with_h32 else 2
    return pl.pallas_call(
        _merge_out_kernel,
        grid=(m // tm,),
        in_specs=[row(HG_I, 0), row(RG_WIDTH, 0), row(d, moff), row(d, moff + 1), row(d, 0),
                  res(w_a.shape), res(w_b.shape), res(w_o.shape), res((1, d))],
        out_specs=[row(d, 0)] * n_out,
        out_shape=[jax.ShapeDtypeStruct((m, d), F32), jax.ShapeDtypeStruct((m, d), BF16),
                   jax.ShapeDtypeStruct((m, d), F32)][:n_out],
        compiler_params=_params(("parallel",)),
        name="merge_out",
    )(o, y, proj, proj, x, w_a, w_b, w_o, g.reshape(1, d))


def _ffn_kernel(h_ref, x_ref, wg_ref, wu_ref, wd_ref, g_ref, o_ref, xo_ref):
    f = pl.program_id(1)

    @pl.when(f == 0)
    def _():
        xo_ref[...] = jnp.zeros_like(xo_ref)

    h = h_ref[...]
    gate = _dot(h, wg_ref[...])
    act = (gate * _sigmoid(gate)) * _dot(h, wu_ref[...])
    xo_ref[...] += _dot(act.astype(BF16), wd_ref[...])

    @pl.when(f == pl.num_programs(1) - 1)
    def _():
        x_new = x_ref[...] + xo_ref[...]
        xo_ref[...] = x_new
        o_ref[...] = _rms(x_new, g_ref[...]).astype(o_ref.dtype)


def _ffn(h, x, wg, wu, wd, g, out_dtype, tf):
    m, d = x.shape
    ff = wg.shape[1]
    tm = _row_tile(m, 640)
    return pl.pallas_call(
        _ffn_kernel,
        grid=(m // tm, ff // tf),
        in_specs=[pl.BlockSpec((tm, d), lambda i, f: (i, 0)),
                  pl.BlockSpec((tm, d), lambda i, f: (i, 0)),
                  pl.BlockSpec((d, tf), lambda i, f: (0, f)),
                  pl.BlockSpec((d, tf), lambda i, f: (0, f)),
                  pl.BlockSpec((tf, d), lambda i, f: (f, 0)),
                  pl.BlockSpec((1, d), lambda i, f: (0, 0))],
        out_specs=[pl.BlockSpec((tm, d), lambda i, f: (i, 0)),
                   pl.BlockSpec((tm, d), lambda i, f: (i, 0))],
        out_shape=[jax.ShapeDtypeStruct((m, d), out_dtype), jax.ShapeDtypeStruct((m, d), F32)],
        compiler_params=_params(("parallel", "arbitrary")),
        name="ffn_dense",
    )(h, x, wg, wu, wd, g.reshape(1, d))


def _router_kernel(h_ref, w_ref, info_ref, gate_ref, cnt_ref, carry_ref):
    @pl.when(pl.program_id(0) == 0)
    def _():
        carry_ref[...] = jnp.zeros_like(carry_ref)

    logits = _dot(h_ref[...], w_ref[...])
    tm = logits.shape[0]
    lane = lax.broadcasted_iota(I32, logits.shape, 1)
    neg = jnp.float32(-jnp.inf)
    logits = jnp.where(lane < N_EXPERTS, logits, neg)
    m1 = jnp.max(logits, axis=-1, keepdims=True)
    i1 = jnp.min(jnp.where(logits == m1, lane, LANES), axis=-1, keepdims=True)
    rest = jnp.where(lane == i1, neg, logits)
    m2 = jnp.max(rest, axis=-1, keepdims=True)
    i2 = jnp.min(jnp.where(rest == m2, lane, LANES), axis=-1, keepdims=True)
    e = jnp.exp(m2 - m1)
    g1 = 1.0 / (1.0 + e)
    g2 = e / (1.0 + e)
    gate_ref[...] = jnp.where(lane == 0, g1, jnp.where(lane == 1, g2, 0.0))

    chosen = jnp.where((lane == i1) | (lane == i2), 1.0, 0.0)
    lower = jnp.where(lax.broadcasted_iota(I32, (tm, tm), 0) > lax.broadcasted_iota(I32, (tm, tm), 1),
                      1.0, 0.0).astype(BF16)
    before = _dot(lower, chosen.astype(BF16)) + carry_ref[0:1, :]
    r1 = jnp.sum(jnp.where(lane == i1, before, 0.0), axis=-1, keepdims=True).astype(I32)
    r2 = jnp.sum(jnp.where(lane == i2, before, 0.0), axis=-1, keepdims=True).astype(I32)
    info_ref[...] = jnp.where(lane == 0, i1, jnp.where(lane == 1, i2,
                              jnp.where(lane == 2, r1, jnp.where(lane == 3, r2, 0))))
    total = carry_ref[0:1, :] + jnp.sum(chosen, axis=0, keepdims=True)
    carry_ref[...] = jnp.broadcast_to(total, carry_ref.shape)
    cnt_ref[...] = jnp.broadcast_to(total, cnt_ref.shape)


def _router(h, w_router):
    m, d = h.shape
    tm = _row_tile(m, 640)
    w = jnp.pad(w_router, ((0, 0), (0, LANES - N_EXPERTS))).astype(BF16)
    return pl.pallas_call(
        _router_kernel,
        grid=(m // tm,),
        in_specs=[pl.BlockSpec((tm, d), lambda i: (i, 0)),
                  pl.BlockSpec((d, LANES), lambda i: (0, 0))],
        out_specs=[pl.BlockSpec((tm, LANES), lambda i: (i, 0)),
                   pl.BlockSpec((tm, LANES), lambda i: (i, 0)),
                   pl.BlockSpec((SUBLANES, LANES), lambda i: (0, 0))],
        out_shape=[jax.ShapeDtypeStruct((m, LANES), I32),
                   jax.ShapeDtypeStruct((m, LANES), F32),
                   jax.ShapeDtypeStruct((SUBLANES, LANES), F32)],
        scratch_shapes=[pltpu.VMEM((SUBLANES, LANES), F32)],
        compiler_params=_params(("arbitrary",)),
        name="router",
    )(h, w)


def _row_copy(src_ref, src_row, dst_ref, dst_row, sem):
    return pltpu.make_async_copy(src_ref.at[pl.ds(src_row, 1), :], dst_ref.at[pl.ds(dst_row, 1), :], sem)


def _rows_wait(src_ref, dst_ref, n_rows, sem):
    pltpu.make_async_copy(src_ref.at[pl.ds(0, n_rows), :], dst_ref.at[pl.ds(0, n_rows), :], sem).wait()


def _slots_kernel(pos_ref, src_ref, *, pad_mask):
    tb = pos_ref.shape[-1] // TOP_K
    base = pl.program_id(0) * tb

    @pl.when(pl.program_id(0) == 0)
    def _():
        def clear(p, carry):
            src_ref[p] = p & pad_mask
            return carry
        lax.fori_loop(0, src_ref.shape[0], clear, 0, unroll=8)

    def body(j, carry):
        for k in range(TOP_K):
            src_ref[pos_ref[0, TOP_K * j + k]] = base + j
        return carry

    lax.fori_loop(0, tb, body, 0, unroll=4)


def _slots(pos, n_rows):
    m = pos.shape[0]
    tb = _row_tile(m, 512)
    pad_mask = (1 << (m.bit_length() - 1)) - 1
    return pl.pallas_call(
        functools.partial(_slots_kernel, pad_mask=pad_mask),
        grid=(m // tb,),
        in_specs=[pl.BlockSpec((None, 1, TOP_K * tb), lambda i: (i, 0, 0), memory_space=pltpu.SMEM)],
        out_specs=pl.BlockSpec(memory_space=pltpu.SMEM),
        out_shape=jax.ShapeDtypeStruct((n_rows,), I32),
        compiler_params=_params(("arbitrary",)),
        name="moe_slots",
    )(pos.reshape(m // tb, 1, TOP_K * tb))


def _gather_kernel(ns_ref, src_ref, h_ref, xs_ref, buf_ref, sem):
    n_sub = ns_ref[pl.program_id(0)]
    for s in range(EXPERT_WINDOW_SUBS):
        lo = s * EXPERT_ROWS

        @pl.when(s < n_sub)
        def _():
            def body(r2, carry):
                for p in range(2):
                    r = lo + 2 * r2 + p
                    _row_copy(h_ref, src_ref[0, r], buf_ref, r, sem.at[s]).start(priority=p)
                return carry

            lax.fori_loop(0, EXPERT_ROWS // 2, body, 0, unroll=4)

    for s in range(EXPERT_WINDOW_SUBS):
        lo = s * EXPERT_ROWS
        rows = slice(lo, lo + EXPERT_ROWS)

        @pl.when(s < n_sub)
        def _():
            _rows_wait(h_ref, buf_ref.at[rows, :], EXPERT_ROWS, sem.at[s])
            xs_ref[rows, :] = buf_ref[rows, :].astype(xs_ref.dtype)

        @pl.when(s >= n_sub)
        def _():
            xs_ref[rows, :] = jnp.zeros((EXPERT_ROWS, xs_ref.shape[1]), xs_ref.dtype)


def _gather(h32, src, win_subs):
    m, d = h32.shape
    win = EXPERT_WINDOW_SUBS * EXPERT_ROWS
    assert m >= EXPERT_ROWS
    n_win = src.shape[0] // win
    return pl.pallas_call(
        _gather_kernel,
        grid=(n_win,),
        in_specs=[pl.BlockSpec(memory_space=pltpu.SMEM),
                  pl.BlockSpec((None, 1, win), lambda b: (b, 0, 0), memory_space=pltpu.SMEM),
                  pl.BlockSpec(memory_space=pl.ANY)],
        out_specs=pl.BlockSpec((win, d), lambda b: (b, 0)),
        out_shape=jax.ShapeDtypeStruct((n_win * win, d), BF16),
        scratch_shapes=[pltpu.VMEM((win, d), F32),
                        pltpu.SemaphoreType.DMA((EXPERT_WINDOW_SUBS,))],
        compiler_params=_params(("arbitrary",)),
        name="moe_gather",
    )(win_subs, src.reshape(n_win, 1, win), h32)


def _experts_kernel(we_ref, ns_ref, nv_ref, x_ref, wg_ref, wu_ref, wd_ref, y_ref,
                    wg_bf, wu_bf, wd_bf):
    del we_ref
    b = pl.program_id(0)

    @pl.when(pl.program_id(1) == 0)
    def _():
        y_ref[...] = jnp.zeros_like(y_ref)

    def sub_block(s):
        rows = slice(s * EXPERT_ROWS, (s + 1) * EXPERT_ROWS)
        h = x_ref[rows, :]
        gate = _dot(h, wg_bf[...])
        act = (gate * _sigmoid(gate)) * _dot(h, wu_bf[...])
        y_ref[rows, :] += _dot(act.astype(BF16), wd_bf[...])

    @pl.when(b < nv_ref[0])
    def _():
        wg_bf[...] = wg_ref[...].astype(BF16)
        wu_bf[...] = wu_ref[...].astype(BF16)
        wd_bf[...] = wd_ref[...].astype(BF16)
        sub_block(0)
        for s in range(1, EXPERT_WINDOW_SUBS):
            @pl.when(s < ns_ref[b])
            def _():
                sub_block(s)


def _experts(xs, wg, wu, wd, layer, win_expert, win_subs, n_valid, tf):
    n_rows, d = xs.shape
    ff = wg.shape[-1]
    n_f = ff // tf
    win = EXPERT_WINDOW_SUBS * EXPERT_ROWS
    xmap = lambda b, f, we, ns, nv: (jnp.minimum(b, nv[0] - 1), 0)
    fidx = lambda b, f, nv: jnp.where(b < nv[0], f, n_f - 1)
    once = pl.Buffered(1)
    grid_spec = pltpu.PrefetchScalarGridSpec(
        num_scalar_prefetch=3,
        grid=(n_rows // win, n_f),
        in_specs=[pl.BlockSpec((win, d), xmap),
                  pl.BlockSpec((None, None, d, tf),
                               lambda b, f, we, ns, nv: (layer, we[b], 0, fidx(b, f, nv))),
                  pl.BlockSpec((None, None, d, tf),
                               lambda b, f, we, ns, nv: (layer, we[b], 0, fidx(b, f, nv))),
                  pl.BlockSpec((None, None, tf, d),
                               lambda b, f, we, ns, nv: (layer, we[b], fidx(b, f, nv), 0))],
        out_specs=pl.BlockSpec((win, d), lambda b, f, we, ns, nv: (b, 0), pipeline_mode=once),
        scratch_shapes=[pltpu.VMEM((d, tf), BF16), pltpu.VMEM((d, tf), BF16),
                        pltpu.VMEM((tf, d), BF16)],
    )
    return pl.pallas_call(
        _experts_kernel,
        grid_spec=grid_spec,
        out_shape=jax.ShapeDtypeStruct((n_rows, d), F32),
        compiler_params=_params(("arbitrary", "arbitrary"), vmem=EXPERT_VMEM_LIMIT),
        name="moe_experts",
    )(win_expert, win_subs, n_valid, xs, wg, wu, wd)


def _combine_kernel(pos_ref, x_ref, gate_ref, g_ref, y_ref, o_ref, ya_ref, yb_ref, sem):
    tb = x_ref.shape[0]

    def body(j, carry):
        _row_copy(y_ref, pos_ref[0, TOP_K * j], ya_ref, j, sem).start(priority=0)
        _row_copy(y_ref, pos_ref[0, TOP_K * j + 1], yb_ref, j, sem).start(priority=1)
        return carry

    lax.fori_loop(0, tb, body, 0, unroll=4)
    _rows_wait(y_ref, ya_ref, tb, sem)
    _rows_wait(y_ref, yb_ref, tb, sem)
    gate = gate_ref[...]
    x_new = x_ref[...] + (gate[:, 0:1] * ya_ref[...] + gate[:, 1:2] * yb_ref[...])
    o_ref[...] = _rms(x_new, g_ref[...]).astype(o_ref.dtype)


def _combine(x, y, pos, gates, g, out_dtype, row0, n_rows):
    d = x.shape[1]
    tb = _row_tile(n_rows, 512)
    assert row0 % tb == 0
    blk0 = row0 // tb
    pos = pos.reshape(-1)[TOP_K * row0:TOP_K * (row0 + n_rows)].reshape(n_rows // tb, 1, TOP_K * tb)
    return pl.pallas_call(
        _combine_kernel,
        grid=(n_rows // tb,),
        in_specs=[pl.BlockSpec((None, 1, TOP_K * tb), lambda i: (i, 0, 0), memory_space=pltpu.SMEM),
                  pl.BlockSpec((tb, d), lambda i: (blk0 + i, 0)),
                  pl.BlockSpec((tb, LANES), lambda i: (blk0 + i, 0)),
                  pl.BlockSpec((1, d), lambda i: (0, 0)),
                  pl.BlockSpec(memory_space=pl.ANY)],
        out_specs=pl.BlockSpec((tb, d), lambda i: (i, 0)),
        out_shape=jax.ShapeDtypeStruct((n_rows, d), out_dtype),
        scratch_shapes=[pltpu.VMEM((tb, d), F32), pltpu.VMEM((tb, d), F32),
                        pltpu.SemaphoreType.DMA(())],
        compiler_params=_params(("arbitrary",)),
        name="moe_combine",
    )(pos, x, gates, g.reshape(1, d), y)


def _moe(h, h32, x, w_router, wg, wu, wd, layer, g, out_dtype, m_prompt):
    m, d = x.shape
    sub, wsubs = EXPERT_ROWS, EXPERT_WINDOW_SUBS
    win = sub * wsubs
    info, gates, cnt = _router(h, w_router)

    counts = cnt[0, :N_EXPERTS].astype(I32)
    n_sub = (counts + (sub - 1)) // sub
    n_win = (n_sub + (wsubs - 1)) // wsubs
    per_win = (n_sub + jnp.maximum(n_win, 1) - 1) // jnp.maximum(n_win, 1)
    win_end = jnp.cumsum(n_win)
    win_first = win_end - n_win
    n_valid = win_end[-1]
    max_windows = -(-(-(-(TOP_K * m) // sub) + N_EXPERTS) // wsubs) + N_EXPERTS
    wid = jnp.arange(max_windows, dtype=I32)
    owner = jnp.minimum(jnp.sum((wid[:, None] >= win_end[None, :]).astype(I32), axis=1),
                        N_EXPERTS - 1)
    subs_here = jnp.clip(n_sub[owner] - (wid - win_first[owner]) * per_win[owner], 0, per_win[owner])
    used = wid < n_valid
    win_expert = jnp.where(used, owner, owner[n_valid - 1])
    win_subs = jnp.where(used, subs_here, 0)

    e_tok, r_tok = info[:, :TOP_K], info[:, TOP_K:2 * TOP_K]
    rows_per_win = (per_win * sub)[e_tok]
    max_win_per_expert = -(-(-(-m // sub)) // wsubs)
    win_in_expert = sum((r_tok >= k * rows_per_win).astype(I32) for k in range(1, max_win_per_expert))
    pos = (win_first[e_tok] + win_in_expert) * win + (r_tok - win_in_expert * rows_per_win)

    src = _slots(pos, max_windows * win)
    xs = _gather(h32, src, win_subs)
    y = _experts(xs, wg, wu, wd, layer, win_expert, win_subs, n_valid.reshape(1), 512)
    return (_combine(x, y, pos, gates, g, out_dtype, 0, m_prompt),
            _combine(x, y, pos, gates, g, out_dtype, m_prompt, m - m_prompt))


def _pad_ff(w, axis, mult):
    ff = w.shape[axis]
    pad = (-ff) % mult
    if pad:
        widths = [(0, 0)] * w.ndim
        widths[axis] = (0, pad)
        w = jnp.pad(w, widths)
    return w


def kernel(x_prompt, x_sample, state_hgrn, state_rglru, state_conv, hgrn_lb_logits, norm_mix, w_in,
           hgrn_gnorm, rg_conv_w, rg_conv_b, rg_wa, rg_ba, rg_wx, rg_bx, rg_a_param, w_br_a, w_br_b,
           w_out, norm_ffn, ffn_w_gate, ffn_w_up, ffn_w_down, moe_router, moe_w_gate, moe_w_up,
           moe_w_down, norm_final):
    depth = w_in.shape[0]
    batch, seq, d = x_prompt.shape
    n_seq = x_sample.shape[0]
    m_prompt = batch * seq
    m = m_prompt + n_seq

    sm = jax.nn.softmax(hgrn_lb_logits.astype(F32), axis=0)
    lbs = jnp.maximum(jnp.cumsum(sm, axis=0) - sm[0:1], 0.0)
    one_m_lbs = 1.0 - lbs

    x = jnp.concatenate([x_prompt.reshape(m_prompt, d), x_sample.reshape(n_seq, d)], axis=0)
    h = _rmsnorm(x, norm_mix[0], BF16)

    chunk = 256 if seq % 256 == 0 else seq
    rg_rows = 512 if seq % 512 == 0 else chunk
    hg_p, rg_p, cv_p, rg_s, cv_s = [], [], [], [], []
    hg_s = None
    for l in range(depth):
        proj = _in_proj(h, w_in, l)
        vrow = lambda a: a[l].reshape(1, -1)
        o_all, s_prompt = _hgrn_prompt(proj, vrow(lbs), vrow(one_m_lbs), vrow(hgrn_gnorm),
                                       batch, seq, m, chunk)
        o_all, hg_s = _hgrn_sample(proj, vrow(lbs), vrow(one_m_lbs), vrow(hgrn_gnorm),
                                   state_hgrn, l, o_all, m_prompt, hg_s)
        rg_args = (rg_conv_w[l], vrow(rg_conv_b), rg_wa[l], rg_ba[l].reshape(RG_BLOCKS, 1, RG_BW),
                   rg_wx[l], rg_bx[l].reshape(RG_BLOCKS, 1, RG_BW), vrow(rg_a_param))
        y_all, h_prompt = _rg_prompt(proj, *rg_args, batch, seq, m, rg_rows)
        y_all, h_sample = _rg_sample(proj, state_conv[l], state_rglru[l], *rg_args, y_all, m_prompt)

        dense = l % 2 == 0
        merged = _merge_out(o_all, y_all, proj, x, w_br_a[l].astype(BF16), w_br_b[l].astype(BF16),
                            w_out[l].astype(BF16), norm_ffn[l], with_h32=not dense)
        x, h2 = merged[0], merged[1]

        last = l == depth - 1
        g_next = norm_final if last else norm_mix[l + 1]
        out_dtype = F32 if last else BF16
        j = l // 2
        if dense:
            wg = _pad_ff(ffn_w_gate[j].astype(BF16), 1, 512)
            wu = _pad_ff(ffn_w_up[j].astype(BF16), 1, 512)
            wd = _pad_ff(ffn_w_down[j].astype(BF16), 0, 512)
            h, x = _ffn(h2, x, wg, wu, wd, g_next, out_dtype, 512)
        else:
            assert last
            y_prompt, y_sample = _moe(h2, merged[2], x, moe_router[j], moe_w_gate, moe_w_up,
                                      moe_w_down, j, g_next, out_dtype, m_prompt)

        xr_cols = slice(2 * HG_F + 2 * HG_I, 2 * HG_F + 2 * HG_I + RG_WIDTH)
        tail = jnp.stack([proj[(b + 1) * seq - (CONV_W - 1):(b + 1) * seq, xr_cols]
                          for b in range(batch)])
        hg_p.append(s_prompt)
        rg_p.append(h_prompt.reshape(batch, RG_WIDTH))
        cv_p.append(tail)
        rg_s.append(h_sample)
        cv_s.append(jnp.concatenate([state_conv[l][:, 1:], proj[m_prompt:, None, xr_cols]], axis=1))

    y_prompt = y_prompt.reshape(batch, seq, d)
    y_sample = y_sample.reshape(n_seq, 1, d)
    return (y_prompt, y_sample, jnp.stack(hg_p), jnp.stack(rg_p), jnp.stack(cv_p),
            hg_s, jnp.stack(rg_s), jnp.stack(cv_s))
```

```python
import functools

import jax
import jax.numpy as jnp
from jax import lax
from jax.experimental import pallas as pl
from jax.experimental.pallas import tpu as pltpu

D_MODEL = 2048
HG_HEADS = 8
HG_DK = 128
HG_DV = 128
HG_F = HG_HEADS * HG_DK
HG_I = HG_HEADS * HG_DV
RG_WIDTH = 1024
RG_BLOCKS = 8
RG_BW = RG_WIDTH // RG_BLOCKS
CONV_W = 4
RG_C = 8.0
IN_COLS = 2 * HG_F + 2 * HG_I + 2 * RG_WIDTH + 2 * D_MODEL
N_EXPERTS = 8
TOP_K = 2
EPS = 1e-6

LANES = 128
SUBLANES = 8
BF16_SUBLANES = 16
VMEM_LIMIT = 56 * 1024 * 1024
EXPERT_VMEM_LIMIT = 62 * 1024 * 1024
HGRN_HEADS_PER_STEP = 4
HGRN_SAMPLE_ROWS = 16
HGRN_SAMPLE_HEADS = 2
RG_BLOCKS_PER_STEP = 4
EXPERT_ROWS = 512
EXPERT_WINDOW_SUBS = 3

BF16 = jnp.bfloat16
F32 = jnp.float32
I32 = jnp.int32


def _params(sem, vmem=VMEM_LIMIT):
    return pltpu.CompilerParams(dimension_semantics=sem, vmem_limit_bytes=vmem)


def _row_tile(m, target):
    best = None
    for t in range(BF16_SUBLANES, min(m, target) + 1, BF16_SUBLANES):
        if m % t == 0:
            best = t
    assert best is not None, (m, target)
    return best


def _dot(a, b):
    return jnp.dot(a, b, preferred_element_type=F32)


def _dot_nt(a, b):
    return lax.dot_general(a, b, (((1,), (1,)), ((), ())), preferred_element_type=F32)


def _dot_tn(a, b):
    return lax.dot_general(a, b, (((0,), (0,)), ((), ())), preferred_element_type=F32)


def _sigmoid(x):
    return 1.0 / (1.0 + jnp.exp(-x))


def _log_sigmoid(x):
    return jnp.minimum(x, 0.0) - jnp.log1p(jnp.exp(-jnp.abs(x)))


def _rms(x, g):
    ms = jnp.mean(x * x, axis=-1, keepdims=True)
    return x * lax.rsqrt(ms + EPS) * g


def _rmsnorm_kernel(x_ref, g_ref, o_ref):
    o_ref[...] = _rms(x_ref[...], g_ref[...]).astype(o_ref.dtype)


def _rmsnorm(x, g, out_dtype):
    m, d = x.shape
    tm = _row_tile(m, 640)
    return pl.pallas_call(
        _rmsnorm_kernel,
        grid=(m // tm,),
        in_specs=[pl.BlockSpec((tm, d), lambda i: (i, 0)),
                  pl.BlockSpec((1, d), lambda i: (0, 0))],
        out_specs=pl.BlockSpec((tm, d), lambda i: (i, 0)),
        out_shape=jax.ShapeDtypeStruct((m, d), out_dtype),
        compiler_params=_params(("parallel",)),
        name="rmsnorm",
    )(x, g.reshape(1, d))


def _in_proj_kernel(x_ref, w_ref, o_ref, wbf_ref):
    @pl.when(pl.program_id(1) == 0)
    def _():
        wbf_ref[...] = w_ref[...].astype(BF16)

    o_ref[...] = _dot(x_ref[...], wbf_ref[...])


def _in_proj(h, w_in, layer):
    m, d = h.shape
    n = w_in.shape[-1]
    tm = _row_tile(m, 1664)
    tn = 1024
    return pl.pallas_call(
        _in_proj_kernel,
        grid=(n // tn, m // tm),
        in_specs=[pl.BlockSpec((tm, d), lambda j, i: (i, 0)),
                  pl.BlockSpec((None, d, tn), lambda j, i: (layer, 0, j))],
        out_specs=pl.BlockSpec((tm, tn), lambda j, i: (i, j)),
        out_shape=jax.ShapeDtypeStruct((m, n), F32),
        scratch_shapes=[pltpu.VMEM((d, tn), BF16)],
        compiler_params=_params(("arbitrary", "arbitrary")),
        name="in_proj",
    )(h, w_in)


def _hgrn_gates(fz, lb, one_m_lb):
    u = jnp.exp(-jnp.abs(fz))
    r = 1.0 / (1.0 + u)
    nonneg = fz >= 0.0
    sig = jnp.where(nonneg, r, u * r)
    sig_neg = jnp.where(nonneg, u * r, r)
    f = lb + one_m_lb * sig
    return jnp.log(f), one_m_lb * sig_neg, f


def _hgrn_out(o, gate, gnorm):
    return _rms(o, gnorm) * (gate * _sigmoid(gate))


def _hgrn_prompt_kernel(q_ref, f_ref, i_ref, g_ref, lb_ref, omlb_ref, gn_ref, o_init_ref,
                        o_ref, s_out_ref, s_ref, *, chunk):
    del o_init_ref
    C = chunk
    c = pl.program_id(2)

    @pl.when(c == 0)
    def _():
        s_ref[...] = jnp.zeros_like(s_ref)

    row = lax.broadcasted_iota(I32, (C, HG_DK), 0)
    rc_xor = lax.broadcasted_iota(I32, (C, C), 0) ^ lax.broadcasted_iota(I32, (C, C), 1)

    for hh in range(HGRN_HEADS_PER_STEP):
        cols = slice(hh * HG_DK, (hh + 1) * HG_DK)
        q = q_ref[:, cols]
        v = i_ref[:, cols].astype(BF16)
        g, k, _ = _hgrn_gates(f_ref[:, cols], lb_ref[:, cols], omlb_ref[:, cols])

        a_mat = jnp.where(rc_xor == 0, _dot_nt(q.astype(BF16), k.astype(BF16)), 0.0)

        pre, suf, tot = g, jnp.zeros_like(g), g
        w = 1
        while w < C:
            upper = (row & w) != 0
            e = jnp.exp(jnp.where(upper, pre, suf))
            qw = jnp.where(upper, q * e, 0.0).astype(BF16)
            kw = jnp.where(upper, 0.0, k * e).astype(BF16)
            level = _dot_nt(qw, kw)
            a_mat = a_mat + (level if 2 * w == C else jnp.where(rc_xor < 2 * w, level, 0.0))
            up = pltpu.roll(tot, w, 0)
            dn = pltpu.roll(tot, C - w, 0)
            pre = jnp.where(upper, pre + up, pre)
            suf = jnp.where(upper, suf, suf + dn)
            tot = pre + suf
            w *= 2

        s_old = s_ref[hh]
        o = (_dot(a_mat.astype(BF16), v)
             + _dot((q * jnp.exp(pre)).astype(BF16), s_old.astype(BF16)))
        decay_rows = jnp.exp(jnp.broadcast_to(tot[0:1, :], (HG_DK, HG_DK))).T
        s_new = decay_rows * s_old + _dot_tn((k * jnp.exp(suf)).astype(BF16), v)
        s_ref[hh] = s_new

        o_ref[:, cols] = _hgrn_out(o, g_ref[:, cols], gn_ref[:, cols]).astype(o_ref.dtype)

    @pl.when(c == pl.num_programs(2) - 1)
    def _():
        s_out_ref[...] = s_ref[...]


def _hgrn_prompt(proj, lb, one_m_lb, gnorm, batch, seq, m_total, chunk):
    n_chunks = seq // chunk
    hps = HGRN_HEADS_PER_STEP
    width = hps * LANES
    groups = HG_HEADS // hps
    tok = lambda off: pl.BlockSpec((chunk, width), lambda b, h, c: (b * n_chunks + c, off + h))
    vec = pl.BlockSpec((1, width), lambda b, h, c: (0, h))
    return pl.pallas_call(
        functools.partial(_hgrn_prompt_kernel, chunk=chunk),
        grid=(batch, groups, n_chunks),
        in_specs=[tok(0), tok(groups), tok(2 * groups), tok(3 * groups), vec, vec, vec,
                  pl.BlockSpec(memory_space=pl.ANY)],
        input_output_aliases={7: 0},
        out_specs=[pl.BlockSpec((chunk, width), lambda b, h, c: (b * n_chunks + c, h)),
                   pl.BlockSpec((None, hps, HG_DK, HG_DV), lambda b, h, c: (b, h, 0, 0))],
        out_shape=[jax.ShapeDtypeStruct((m_total, HG_I), BF16),
                   jax.ShapeDtypeStruct((batch, HG_HEADS, HG_DK, HG_DV), F32)],
        scratch_shapes=[pltpu.VMEM((hps, HG_DK, HG_DV), F32)],
        compiler_params=_params(("parallel", "parallel", "arbitrary")),
        name="hgrn_prompt",
    )(proj, proj, proj, proj, lb, one_m_lb, gnorm, jnp.zeros((m_total, HG_I), BF16))


def _hgrn_sample_kernel(q_ref, f_ref, i_ref, g_ref, lb_ref, omlb_ref, gn_ref, s_ref, o_prev_ref,
                        *rest, layer, first):
    del o_prev_ref
    o_ref, s_out_ref = rest[-2:]
    n = HGRN_SAMPLE_ROWS
    seq_id = lax.broadcasted_iota(I32, (n, HG_DV), 0)

    def columns(x):
        pad = jnp.zeros((LANES - n, HG_DK), F32)
        return jnp.concatenate([x, pad], axis=0).T

    for hh in range(HGRN_SAMPLE_HEADS):
        cols = slice(hh * HG_DK, (hh + 1) * HG_DK)
        vi = i_ref[:, cols]
        _, k, f = _hgrn_gates(f_ref[:, cols], lb_ref[:, cols], omlb_ref[:, cols])
        f_t, k_t = columns(f), columns(k)
        qb = q_ref[:, cols].astype(BF16)
        o = jnp.zeros((n, HG_DV), F32)
        for j in range(n):
            bc = lambda xt: jnp.broadcast_to(xt[:, j:j + 1], (HG_DK, HG_DV))
            s_new = bc(f_t) * s_ref[j, hh] + bc(k_t) * vi[j:j + 1, :]
            if first:
                for l2 in range(s_out_ref.shape[0]):
                    s_out_ref[l2, j, hh] = s_new if l2 == layer else jnp.zeros_like(s_new)
            else:
                s_out_ref[j, hh] = s_new
            o_j = _dot(qb[j:j + 1, :], s_new.astype(BF16))
            o = jnp.where(seq_id == j, o_j, o)
        o_ref[:, cols] = _hgrn_out(o, g_ref[:, cols], gn_ref[:, cols]).astype(o_ref.dtype)


def _hgrn_sample(proj, lb, one_m_lb, gnorm, states, layer, o_all, row0, new_states):
    depth, n_seq = states.shape[:2]
    n = HGRN_SAMPLE_ROWS
    assert n_seq % n == 0 and row0 % n == 0
    blk0 = row0 // n
    first = new_states is None
    hps = HGRN_SAMPLE_HEADS
    width = hps * LANES
    groups = HG_HEADS // hps
    tok = lambda off: pl.BlockSpec((n, width), lambda j, h: (blk0 + j, off + h))
    vec = pl.BlockSpec((1, width), lambda j, h: (0, h))
    one_layer = pl.BlockSpec((None, n, hps, HG_DK, HG_DV), lambda j, h: (layer, j, h, 0, 0))
    all_layers = pl.BlockSpec((depth, n, hps, HG_DK, HG_DV), lambda j, h: (0, j, h, 0, 0))
    hbm = pl.BlockSpec(memory_space=pl.ANY)
    args = [proj, proj, proj, proj, lb, one_m_lb, gnorm, states, o_all]
    in_specs = [tok(0), tok(groups), tok(2 * groups), tok(3 * groups), vec, vec, vec,
                one_layer, hbm]
    aliases = {8: 0}
    if not first:
        args.append(new_states)
        in_specs.append(hbm)
        aliases[9] = 1
    return pl.pallas_call(
        functools.partial(_hgrn_sample_kernel, layer=layer, first=first),
        grid=(n_seq // n, groups),
        in_specs=in_specs,
        out_specs=[pl.BlockSpec((n, width), lambda j, h: (blk0 + j, h)),
                   all_layers if first else one_layer],
        out_shape=[jax.ShapeDtypeStruct(o_all.shape, o_all.dtype),
                   jax.ShapeDtypeStruct(states.shape, F32)],
        input_output_aliases=aliases,
        compiler_params=_params(("parallel", "parallel")),
        name="hgrn_sample",
    )(*args)


def _gelu_tanh(x):
    sqrt_2_over_pi = 0.7978845608028654
    return 0.5 * x * (1.0 + jnp.tanh(sqrt_2_over_pi * (x + 0.044715 * (x * x * x))))


def _rg_gates(conv, wa, ba, wx, bx, a_param):
    cb = conv.astype(BF16)
    r = _sigmoid(_dot(cb, wa.astype(BF16)) + ba)
    ig = _sigmoid(_dot(cb, wx.astype(BF16)) + bx)
    log_a = RG_C * r * _log_sigmoid(a_param)
    a = jnp.exp(log_a)
    t = jnp.tanh(log_a)
    y = -2.0 * t
    root = y * lax.rsqrt(jnp.maximum(y, 1e-30)) * lax.rsqrt(1.0 - t)
    b = root * ig * conv
    return a, b


def _rg_prompt_kernel(x_ref, gate_ref, cw_ref, cb_ref, wa_ref, ba_ref, wx_ref, bx_ref, ap_ref,
                      y_init_ref, y_ref, h_out_ref, xp_ref, h_ref, *, rows):
    del y_init_ref
    T = rows
    t = pl.program_id(2)

    @pl.when(t == 0)
    def _():
        xp_ref[0:SUBLANES, :] = jnp.zeros((SUBLANES, xp_ref.shape[1]), F32)
        h_ref[...] = jnp.zeros_like(h_ref)

    xp_ref[SUBLANES:SUBLANES + T, :] = x_ref[...]
    row = lax.broadcasted_iota(I32, (T, RG_BW), 0)

    for nb in range(RG_BLOCKS_PER_STEP):
        cols = slice(nb * RG_BW, (nb + 1) * RG_BW)
        xp = xp_ref[:, cols]
        conv = cb_ref[:, cols]
        for j in range(CONV_W):
            lag = CONV_W - 1 - j
            tap = xp if lag == 0 else pltpu.roll(xp, lag, 0)
            conv = conv + tap[SUBLANES:SUBLANES + T, :] * cw_ref[j:j + 1, cols]

        a, b = _rg_gates(conv, wa_ref[nb], ba_ref[nb], wx_ref[nb], bx_ref[nb], ap_ref[:, cols])

        first = row == 0
        b = jnp.where(first, a * h_ref[0:1, cols] + b, b)
        a = jnp.where(first, 0.0, a)
        d = 1
        while d < T:
            b = a * pltpu.roll(b, d, 0) + b
            a = a * pltpu.roll(a, d, 0)
            d *= 2
        h = b
        h_last = h[T - 1:T, :]
        h_ref[:, cols] = jnp.broadcast_to(h_last, (h_ref.shape[0], RG_BW))
        y_ref[:, cols] = (h * _gelu_tanh(gate_ref[:, cols])).astype(y_ref.dtype)

    xp_ref[0:SUBLANES, :] = xp_ref[T:T + SUBLANES, :]

    @pl.when(t == pl.num_programs(2) - 1)
    def _():
        h_out_ref[...] = h_ref[0:1, :]


def _rg_prompt(proj, cw, cb, wa, ba, wx, bx, ap, batch, seq, m_total, rows):
    n_t = seq // rows
    bps = RG_BLOCKS_PER_STEP
    width = bps * RG_BW
    groups = RG_BLOCKS // bps
    xoff = (2 * HG_F + 2 * HG_I) // width
    goff = xoff + groups
    tok = lambda off: pl.BlockSpec((rows, width), lambda b, n, t: (b * n_t + t, off + n))
    vec = pl.BlockSpec((1, width), lambda b, n, t: (0, n))
    blkw = pl.BlockSpec((bps, RG_BW, RG_BW), lambda b, n, t: (n, 0, 0))
    blkb = pl.BlockSpec((bps, 1, RG_BW), lambda b, n, t: (n, 0, 0))
    return pl.pallas_call(
        functools.partial(_rg_prompt_kernel, rows=rows),
        grid=(batch, groups, n_t),
        in_specs=[tok(xoff), tok(goff),
                  pl.BlockSpec((CONV_W, width), lambda b, n, t: (0, n)), vec,
                  blkw, blkb, blkw, blkb, vec, pl.BlockSpec(memory_space=pl.ANY)],
        input_output_aliases={9: 0},
        out_specs=[pl.BlockSpec((rows, width), lambda b, n, t: (b * n_t + t, n)),
                   pl.BlockSpec((None, 1, width), lambda b, n, t: (b, 0, n))],
        out_shape=[jax.ShapeDtypeStruct((m_total, RG_WIDTH), BF16),
                   jax.ShapeDtypeStruct((batch, 1, RG_WIDTH), F32)],
        scratch_shapes=[pltpu.VMEM((rows + SUBLANES, width), F32),
                        pltpu.VMEM((SUBLANES, width), F32)],
        compiler_params=_params(("parallel", "parallel", "arbitrary")),
        name="rg_prompt",
    )(proj, proj, cw, cb, wa, ba, wx, bx, ap, jnp.zeros((m_total, RG_WIDTH), BF16))


def _rg_sample_kernel(x_ref, gate_ref, b0_ref, b1_ref, b2_ref, h0_ref, cw_ref, cb_ref,
                      wa_ref, ba_ref, wx_ref, bx_ref, ap_ref, y_prev_ref, y_ref, h_out_ref):
    del y_prev_ref
    conv = cb_ref[...]
    for j, r in enumerate((b0_ref, b1_ref, b2_ref, x_ref)):
        conv = conv + r[...] * cw_ref[j:j + 1, :]
    a, b = _rg_gates(conv, wa_ref[...], ba_ref[...], wx_ref[...], bx_ref[...], ap_ref[...])
    h = a * h0_ref[...] + b
    h_out_ref[...] = h
    y_ref[...] = (h * _gelu_tanh(gate_ref[...])).astype(y_ref.dtype)


def _rg_sample(proj, conv_state, h0, cw, cb, wa, ba, wx, bx, ap, y_all, row0):
    n_seq = h0.shape[0]
    assert row0 % n_seq == 0 and n_seq % BF16_SUBLANES == 0
    blk0 = row0 // n_seq
    xoff = (2 * HG_F + 2 * HG_I) // LANES
    goff = xoff + RG_BLOCKS
    tok = lambda off: pl.BlockSpec((n_seq, LANES), lambda n: (blk0, off + n))
    buf = lambda j: pl.BlockSpec((n_seq, LANES), lambda n: (0, j * RG_BLOCKS + n))
    vec = pl.BlockSpec((1, LANES), lambda n: (0, n))
    blkw = pl.BlockSpec((None, RG_BW, RG_BW), lambda n: (n, 0, 0))
    blkb = pl.BlockSpec((None, 1, RG_BW), lambda n: (n, 0, 0))
    flat_state = conv_state.reshape(n_seq, (CONV_W - 1) * RG_WIDTH)
    return pl.pallas_call(
        _rg_sample_kernel,
        grid=(RG_BLOCKS,),
        in_specs=[tok(xoff), tok(goff), buf(0), buf(1), buf(2),
                  pl.BlockSpec((n_seq, LANES), lambda n: (0, n)),
                  pl.BlockSpec((CONV_W, LANES), lambda n: (0, n)), vec,
                  blkw, blkb, blkw, blkb, vec,
                  pl.BlockSpec(memory_space=pl.ANY)],
        out_specs=[pl.BlockSpec((n_seq, LANES), lambda n: (blk0, n)),
                   pl.BlockSpec((n_seq, LANES), lambda n: (0, n))],
        out_shape=[jax.ShapeDtypeStruct(y_all.shape, y_all.dtype),
                   jax.ShapeDtypeStruct((n_seq, RG_WIDTH), F32)],
        input_output_aliases={13: 0},
        compiler_params=_params(("parallel",)),
        name="rg_sample",
    )(proj, proj, flat_state, flat_state, flat_state, h0, cw, cb, wa, ba, wx, bx, ap, y_all)


def _merge_out_kernel(o_ref, y_ref, ma_ref, mb_ref, x_ref, wa_ref, wb_ref, wo_ref, g_ref,
                      xo_ref, h_ref, *maybe_h32_ref):
    merged = (_sigmoid(ma_ref[...]) * _dot(o_ref[...], wa_ref[...])
              + _sigmoid(mb_ref[...]) * _dot(y_ref[...], wb_ref[...]))
    x_new = x_ref[...] + _dot(merged.astype(BF16), wo_ref[...])
    xo_ref[...] = x_new
    h = _rms(x_new, g_ref[...])
    h_ref[...] = h.astype(h_ref.dtype)
    for r in maybe_h32_ref:
        r[...] = h


def _merge_out(o, y, proj, x, w_a, w_b, w_o, g, with_h32):
    m, d = x.shape
    tm = _row_tile(m, 320)
    moff = (2 * HG_F + 2 * HG_I + 2 * RG_WIDTH) // d
    row = lambda width, col: pl.BlockSpec((tm, width), lambda i: (i, col))
    res = lambda shape: pl.BlockSpec(shape, lambda i: (0, 0), pipeline_mode=pl.Buffered(1))
    n_out = 3 if with_h32 else 2
    return pl.pallas_call(
        _merge_out_kernel,
        grid=(m // tm,),
        in_specs=[row(HG_I, 0), row(RG_WIDTH, 0), row(d, moff), row(d, moff + 1), row(d, 0),
                  res(w_a.shape), res(w_b.shape), res(w_o.shape), res((1, d))],
        out_specs=[row(d, 0)] * n_out,
        out_shape=[jax.ShapeDtypeStruct((m, d), F32), jax.ShapeDtypeStruct((m, d), BF16),
                   jax.ShapeDtypeStruct((m, d), F32)][:n_out],
        compiler_params=_params(("parallel",)),
        name="merge_out",
    )(o, y, proj, proj, x, w_a, w_b, w_o, g.reshape(1, d))


def _ffn_kernel(h_ref, x_ref, wg_ref, wu_ref, wd_ref, g_ref, o_ref, xo_ref):
    f = pl.program_id(1)

    @pl.when(f == 0)
    def _():
        xo_ref[...] = jnp.zeros_like(xo_ref)

    h = h_ref[...]
    gate = _dot(h, wg_ref[...])
    act = (gate * _sigmoid(gate)) * _dot(h, wu_ref[...])
    xo_ref[...] += _dot(act.astype(BF16), wd_ref[...])

    @pl.when(f == pl.num_programs(1) - 1)
    def _():
        x_new = x_ref[...] + xo_ref[...]
        xo_ref[...] = x_new
        o_ref[...] = _rms(x_new, g_ref[...]).astype(o_ref.dtype)


def _ffn(h, x, wg, wu, wd, g, out_dtype, tf):
    m, d = x.shape
    ff = wg.shape[1]
    tm = _row_tile(m, 640)
    return pl.pallas_call(
        _ffn_kernel,
        grid=(m // tm, ff // tf),
        in_specs=[pl.BlockSpec((tm, d), lambda i, f: (i, 0)),
                  pl.BlockSpec((tm, d), lambda i, f: (i, 0)),
                  pl.BlockSpec((d, tf), lambda i, f: (0, f)),
                  pl.BlockSpec((d, tf), lambda i, f: (0, f)),
                  pl.BlockSpec((tf, d), lambda i, f: (f, 0)),
                  pl.BlockSpec((1, d), lambda i, f: (0, 0))],
        out_specs=[pl.BlockSpec((tm, d), lambda i, f: (i, 0)),
                   pl.BlockSpec((tm, d), lambda i, f: (i, 0))],
        out_shape=[jax.ShapeDtypeStruct((m, d), out_dtype), jax.ShapeDtypeStruct((m, d), F32)],
        compiler_params=_params(("parallel", "arbitrary")),
        name="ffn_dense",
    )(h, x, wg, wu, wd, g.reshape(1, d))


def _router_kernel(h_ref, w_ref, info_ref, gate_ref, cnt_ref, carry_ref):
    @pl.when(pl.program_id(0) == 0)
    def _():
        carry_ref[...] = jnp.zeros_like(carry_ref)

    logits = _dot(h_ref[...], w_ref[...])
    tm = logits.shape[0]
    lane = lax.broadcasted_iota(I32, logits.shape, 1)
    neg = jnp.float32(-jnp.inf)
    logits = jnp.where(lane < N_EXPERTS, logits, neg)
    m1 = jnp.max(logits, axis=-1, keepdims=True)
    i1 = jnp.min(jnp.where(logits == m1, lane, LANES), axis=-1, keepdims=True)
    rest = jnp.where(lane == i1, neg, logits)
    m2 = jnp.max(rest, axis=-1, keepdims=True)
    i2 = jnp.min(jnp.where(rest == m2, lane, LANES), axis=-1, keepdims=True)
    e = jnp.exp(m2 - m1)
    g1 = 1.0 / (1.0 + e)
    g2 = e / (1.0 + e)
    gate_ref[...] = jnp.where(lane == 0, g1, jnp.where(lane == 1, g2, 0.0))

    chosen = jnp.where((lane == i1) | (lane == i2), 1.0, 0.0)
    lower = jnp.where(lax.broadcasted_iota(I32, (tm, tm), 0) > lax.broadcasted_iota(I32, (tm, tm), 1),
                      1.0, 0.0).astype(BF16)
    before = _dot(lower, chosen.astype(BF16)) + carry_ref[0:1, :]
    r1 = jnp.sum(jnp.where(lane == i1, before, 0.0), axis=-1, keepdims=True).astype(I32)
    r2 = jnp.sum(jnp.where(lane == i2, before, 0.0), axis=-1, keepdims=True).astype(I32)
    info_ref[...] = jnp.where(lane == 0, i1, jnp.where(lane == 1, i2,
                              jnp.where(lane == 2, r1, jnp.where(lane == 3, r2, 0))))
    total = carry_ref[0:1, :] + jnp.sum(chosen, axis=0, keepdims=True)
    carry_ref[...] = jnp.broadcast_to(total, carry_ref.shape)
    cnt_ref[...] = jnp.broadcast_to(total, cnt_ref.shape)


def _router(h, w_router):
    m, d = h.shape
    tm = _row_tile(m, 640)
    w = jnp.pad(w_router, ((0, 0), (0, LANES - N_EXPERTS))).astype(BF16)
    return pl.pallas_call(
        _router_kernel,
        grid=(m // tm,),
        in_specs=[pl.BlockSpec((tm, d), lambda i: (i, 0)),
                  pl.BlockSpec((d, LANES), lambda i: (0, 0))],
        out_specs=[pl.BlockSpec((tm, LANES), lambda i: (i, 0)),
                   pl.BlockSpec((tm, LANES), lambda i: (i, 0)),
                   pl.BlockSpec((SUBLANES, LANES), lambda i: (0, 0))],
        out_shape=[jax.ShapeDtypeStruct((m, LANES), I32),
                   jax.ShapeDtypeStruct((m, LANES), F32),
                   jax.ShapeDtypeStruct((SUBLANES, LANES), F32)],
        scratch_shapes=[pltpu.VMEM((SUBLANES, LANES), F32)],
        compiler_params=_params(("arbitrary",)),
        name="router",
    )(h, w)


def _row_copy(src_ref, src_row, dst_ref, dst_row, sem):
    return pltpu.make_async_copy(src_ref.at[pl.ds(src_row, 1), :], dst_ref.at[pl.ds(dst_row, 1), :], sem)


def _rows_wait(src_ref, dst_ref, n_rows, sem):
    pltpu.make_async_copy(src_ref.at[pl.ds(0, n_rows), :], dst_ref.at[pl.ds(0, n_rows), :], sem).wait()


def _slots_kernel(pos_ref, src_ref, *, pad_mask):
    tb = pos_ref.shape[-1] // TOP_K
    base = pl.program_id(0) * tb

    @pl.when(pl.program_id(0) == 0)
    def _():
        def clear(p, carry):
            src_ref[p] = p & pad_mask
            return carry
        lax.fori_loop(0, src_ref.shape[0], clear, 0, unroll=8)

    def body(j, carry):
        for k in range(TOP_K):
            src_ref[pos_ref[0, TOP_K * j + k]] = base + j
        return carry

    lax.fori_loop(0, tb, body, 0, unroll=4)


def _slots(pos, n_rows):
    m = pos.shape[0]
    tb = _row_tile(m, 512)
    pad_mask = (1 << (m.bit_length() - 1)) - 1
    return pl.pallas_call(
        functools.partial(_slots_kernel, pad_mask=pad_mask),
        grid=(m // tb,),
        in_specs=[pl.BlockSpec((None, 1, TOP_K * tb), lambda i: (i, 0, 0), memory_space=pltpu.SMEM)],
        out_specs=pl.BlockSpec(memory_space=pltpu.SMEM),
        out_shape=jax.ShapeDtypeStruct((n_rows,), I32),
        compiler_params=_params(("arbitrary",)),
        name="moe_slots",
    )(pos.reshape(m // tb, 1, TOP_K * tb))


def _gather_kernel(ns_ref, src_ref, h_ref, xs_ref, buf_ref, sem):
    n_sub = ns_ref[pl.program_id(0)]
    for s in range(EXPERT_WINDOW_SUBS):
        lo = s * EXPERT_ROWS

        @pl.when(s < n_sub)
        def _():
            def body(r2, carry):
                for p in range(2):
                    r = lo + 2 * r2 + p
                    _row_copy(h_ref, src_ref[0, r], buf_ref, r, sem.at[s]).start(priority=p)
                return carry

            lax.fori_loop(0, EXPERT_ROWS // 2, body, 0, unroll=4)

    for s in range(EXPERT_WINDOW_SUBS):
        lo = s * EXPERT_ROWS
        rows = slice(lo, lo + EXPERT_ROWS)

        @pl.when(s < n_sub)
        def _():
            _rows_wait(h_ref, buf_ref.at[rows, :], EXPERT_ROWS, sem.at[s])
            xs_ref[rows, :] = buf_ref[rows, :].astype(xs_ref.dtype)

        @pl.when(s >= n_sub)
        def _():
            xs_ref[rows, :] = jnp.zeros((EXPERT_ROWS, xs_ref.shape[1]), xs_ref.dtype)


def _gather(h32, src, win_subs):
    m, d = h32.shape
    win = EXPERT_WINDOW_SUBS * EXPERT_ROWS
    assert m >= EXPERT_ROWS
    n_win = src.shape[0] // win
    return pl.pallas_call(
        _gather_kernel,
        grid=(n_win,),
        in_specs=[pl.BlockSpec(memory_space=pltpu.SMEM),
                  pl.BlockSpec((None, 1, win), lambda b: (b, 0, 0), memory_space=pltpu.SMEM),
                  pl.BlockSpec(memory_space=pl.ANY)],
        out_specs=pl.BlockSpec((win, d), lambda b: (b, 0)),
        out_shape=jax.ShapeDtypeStruct((n_win * win, d), BF16),
        scratch_shapes=[pltpu.VMEM((win, d), F32),
                        pltpu.SemaphoreType.DMA((EXPERT_WINDOW_SUBS,))],
        compiler_params=_params(("arbitrary",)),
        name="moe_gather",
    )(win_subs, src.reshape(n_win, 1, win), h32)


def _experts_kernel(we_ref, ns_ref, nv_ref, x_ref, wg_ref, wu_ref, wd_ref, y_ref,
                    wg_bf, wu_bf, wd_bf):
    del we_ref
    b = pl.program_id(0)

    @pl.when(pl.program_id(1) == 0)
    def _():
        y_ref[...] = jnp.zeros_like(y_ref)

    def sub_block(s):
        rows = slice(s * EXPERT_ROWS, (s + 1) * EXPERT_ROWS)
        h = x_ref[rows, :]
        gate = _dot(h, wg_bf[...])
        act = (gate * _sigmoid(gate)) * _dot(h, wu_bf[...])
        y_ref[rows, :] += _dot(act.astype(BF16), wd_bf[...])

    @pl.when(b < nv_ref[0])
    def _():
        wg_bf[...] = wg_ref[...].astype(BF16)
        wu_bf[...] = wu_ref[...].astype(BF16)
        wd_bf[...] = wd_ref[...].astype(BF16)
        sub_block(0)
        for s in range(1, EXPERT_WINDOW_SUBS):
            @pl.when(s < ns_ref[b])
            def _():
                sub_block(s)


def _experts(xs, wg, wu, wd, layer, win_expert, win_subs, n_valid, tf):
    n_rows, d = xs.shape
    ff = wg.shape[-1]
    n_f = ff // tf
    win = EXPERT_WINDOW_SUBS * EXPERT_ROWS
    xmap = lambda b, f, we, ns, nv: (jnp.minimum(b, nv[0] - 1), 0)
    fidx = lambda b, f, nv: jnp.where(b < nv[0], f, n_f - 1)
    once = pl.Buffered(1)
    grid_spec = pltpu.PrefetchScalarGridSpec(
        num_scalar_prefetch=3,
        grid=(n_rows // win, n_f),
        in_specs=[pl.BlockSpec((win, d), xmap),
                  pl.BlockSpec((None, None, d, tf),
                               lambda b, f, we, ns, nv: (layer, we[b], 0, fidx(b, f, nv))),
                  pl.BlockSpec((None, None, d, tf),
                               lambda b, f, we, ns, nv: (layer, we[b], 0, fidx(b, f, nv))),
                  pl.BlockSpec((None, None, tf, d),
                               lambda b, f, we, ns, nv: (layer, we[b], fidx(b, f, nv), 0))],
        out_specs=pl.BlockSpec((win, d), lambda b, f, we, ns, nv: (b, 0), pipeline_mode=once),
        scratch_shapes=[pltpu.VMEM((d, tf), BF16), pltpu.VMEM((d, tf), BF16),
                        pltpu.VMEM((tf, d), BF16)],
    )
    return pl.pallas_call(
        _experts_kernel,
        grid_spec=grid_spec,
        out_shape=jax.ShapeDtypeStruct((n_rows, d), F32),
        compiler_params=_params(("arbitrary", "arbitrary"), vmem=EXPERT_VMEM_LIMIT),
        name="moe_experts",
    )(win_expert, win_subs, n_valid, xs, wg, wu, wd)


def _combine_kernel(pos_ref, x_ref, gate_ref, g_ref, y_ref, o_ref, ya_ref, yb_ref, sem):
    tb = x_ref.shape[0]

    def body(j, carry):
        _row_copy(y_ref, pos_ref[0, TOP_K * j], ya_ref, j, sem).start(priority=0)
        _row_copy(y_ref, pos_ref[0, TOP_K * j + 1], yb_ref, j, sem).start(priority=1)
        return carry

    lax.fori_loop(0, tb, body, 0, unroll=4)
    _rows_wait(y_ref, ya_ref, tb, sem)
    _rows_wait(y_ref, yb_ref, tb, sem)
    gate = gate_ref[...]
    x_new = x_ref[...] + (gate[:, 0:1] * ya_ref[...] + gate[:, 1:2] * yb_ref[...])
    o_ref[...] = _rms(x_new, g_ref[...]).astype(o_ref.dtype)


def _combine(x, y, pos, gates, g, out_dtype, row0, n_rows):
    d = x.shape[1]
    tb = _row_tile(n_rows, 512)
    assert row0 % tb == 0
    blk0 = row0 // tb
    pos = pos.reshape(-1)[TOP_K * row0:TOP_K * (row0 + n_rows)].reshape(n_rows // tb, 1, TOP_K * tb)
    return pl.pallas_call(
        _combine_kernel,
        grid=(n_rows // tb,),
        in_specs=[pl.BlockSpec((None, 1, TOP_K * tb), lambda i: (i, 0, 0), memory_space=pltpu.SMEM),
                  pl.BlockSpec((tb, d), lambda i: (blk0 + i, 0)),
                  pl.BlockSpec((tb, LANES), lambda i: (blk0 + i, 0)),
                  pl.BlockSpec((1, d), lambda i: (0, 0)),
                  pl.BlockSpec(memory_space=pl.ANY)],
        out_specs=pl.BlockSpec((tb, d), lambda i: (i, 0)),
        out_shape=jax.ShapeDtypeStruct((n_rows, d), out_dtype),
        scratch_shapes=[pltpu.VMEM((tb, d), F32), pltpu.VMEM((tb, d), F32),
                        pltpu.SemaphoreType.DMA(())],
        compiler_params=_params(("arbitrary",)),
        name="moe_combine",
    )(pos, x, gates, g.reshape(1, d), y)


def _moe(h, h32, x, w_router, wg, wu, wd, layer, g, out_dtype, m_prompt):
    m, d = x.shape
    sub, wsubs = EXPERT_ROWS, EXPERT_WINDOW_SUBS
    win = sub * wsubs
    info, gates, cnt = _router(h, w_router)

    counts = cnt[0, :N_EXPERTS].astype(I32)
    n_sub = (counts + (sub - 1)) // sub
    n_win = (n_sub + (wsubs - 1)) // wsubs
    per_win = (n_sub + jnp.maximum(n_win, 1) - 1) // jnp.maximum(n_win, 1)
    win_end = jnp.cumsum(n_win)
    win_first = win_end - n_win
    n_valid = win_end[-1]
    max_windows = -(-(-(-(TOP_K * m) // sub) + N_EXPERTS) // wsubs) + N_EXPERTS
    wid = jnp.arange(max_windows, dtype=I32)
    owner = jnp.minimum(jnp.sum((wid[:, None] >= win_end[None, :]).astype(I32), axis=1),
                        N_EXPERTS - 1)
    subs_here = jnp.clip(n_sub[owner] - (wid - win_first[owner]) * per_win[owner], 0, per_win[owner])
    used = wid < n_valid
    win_expert = jnp.where(used, owner, owner[n_valid - 1])
    win_subs = jnp.where(used, subs_here, 0)

    e_tok, r_tok = info[:, :TOP_K], info[:, TOP_K:2 * TOP_K]
    rows_per_win = (per_win * sub)[e_tok]
    max_win_per_expert = -(-(-(-m // sub)) // wsubs)
    win_in_expert = sum((r_tok >= k * rows_per_win).astype(I32) for k in range(1, max_win_per_expert))
    pos = (win_first[e_tok] + win_in_expert) * win + (r_tok - win_in_expert * rows_per_win)

    src = _slots(pos, max_windows * win)
    xs = _gather(h32, src, win_subs)
    y = _experts(xs, wg, wu, wd, layer, win_expert, win_subs, n_valid.reshape(1), 512)
    return (_combine(x, y, pos, gates, g, out_dtype, 0, m_prompt),
            _combine(x, y, pos, gates, g, out_dtype, m_prompt, m - m_prompt))


def _pad_ff(w, axis, mult):
    ff = w.shape[axis]
    pad = (-ff) % mult
    if pad:
        widths = [(0, 0)] * w.ndim
        widths[axis] = (0, pad)
        w = jnp.pad(w, widths)
    return w


def kernel(x_prompt, x_sample, state_hgrn, state_rglru, state_conv, hgrn_lb_logits, norm_mix, w_in,
           hgrn_gnorm, rg_conv_w, rg_conv_b, rg_wa, rg_ba, rg_wx, rg_bx, rg_a_param, w_br_a, w_br_b,
           w_out, norm_ffn, ffn_w_gate, ffn_w_up, ffn_w_down, moe_router, moe_w_gate, moe_w_up,
           moe_w_down, norm_final):
    depth = w_in.shape[0]
    batch, seq, d = x_prompt.shape
    n_seq = x_sample.shape[0]
    m_prompt = batch * seq
    m = m_prompt + n_seq

    sm = jax.nn.softmax(hgrn_lb_logits.astype(F32), axis=0)
    lbs = jnp.maximum(jnp.cumsum(sm, axis=0) - sm[0:1], 0.0)
    one_m_lbs = 1.0 - lbs

    x = jnp.concatenate([x_prompt.reshape(m_prompt, d), x_sample.reshape(n_seq, d)], axis=0)
    h = _rmsnorm(x, norm_mix[0], BF16)

    chunk = 256 if seq % 256 == 0 else seq
    rg_rows = 512 if seq % 512 == 0 else chunk
    hg_p, rg_p, cv_p, rg_s, cv_s = [], [], [], [], []
    hg_s = None
    for l in range(depth):
        proj = _in_proj(h, w_in, l)
        vrow = lambda a: a[l].reshape(1, -1)
        o_all, s_prompt = _hgrn_prompt(proj, vrow(lbs), vrow(one_m_lbs), vrow(hgrn_gnorm),
                                       batch, seq, m, chunk)
        o_all, hg_s = _hgrn_sample(proj, vrow(lbs), vrow(one_m_lbs), vrow(hgrn_gnorm),
                                   state_hgrn, l, o_all, m_prompt, hg_s)
        rg_args = (rg_conv_w[l], vrow(rg_conv_b), rg_wa[l], rg_ba[l].reshape(RG_BLOCKS, 1, RG_BW),
                   rg_wx[l], rg_bx[l].reshape(RG_BLOCKS, 1, RG_BW), vrow(rg_a_param))
        y_all, h_prompt = _rg_prompt(proj, *rg_args, batch, seq, m, rg_rows)
        y_all, h_sample = _rg_sample(proj, state_conv[l], state_rglru[l], *rg_args, y_all, m_prompt)

        dense = l % 2 == 0
        merged = _merge_out(o_all, y_all, proj, x, w_br_a[l].astype(BF16), w_br_b[l].astype(BF16),
                            w_out[l].astype(BF16), norm_ffn[l], with_h32=not dense)
        x, h2 = merged[0], merged[1]

        last = l == depth - 1
        g_next = norm_final if last else norm_mix[l + 1]
        out_dtype = F32 if last else BF16
        j = l // 2
        if dense:
            wg = _pad_ff(ffn_w_gate[j].astype(BF16), 1, 512)
            wu = _pad_ff(ffn_w_up[j].astype(BF16), 1, 512)
            wd = _pad_ff(ffn_w_down[j].astype(BF16), 0, 512)
            h, x = _ffn(h2, x, wg, wu, wd, g_next, out_dtype, 512)
        else:
            assert last
            y_prompt, y_sample = _moe(h2, merged[2], x, moe_router[j], moe_w_gate, moe_w_up,
                                      moe_w_down, j, g_next, out_dtype, m_prompt)

        xr_cols = slice(2 * HG_F + 2 * HG_I, 2 * HG_F + 2 * HG_I + RG_WIDTH)
        tail = jnp.stack([proj[(b + 1) * seq - (CONV_W - 1):(b + 1) * seq, xr_cols]
                          for b in range(batch)])
        hg_p.append(s_prompt)
        rg_p.append(h_prompt.reshape(batch, RG_WIDTH))
        cv_p.append(tail)
        rg_s.append(h_sample)
        cv_s.append(jnp.concatenate([state_conv[l][:, 1:], proj[m_prompt:, None, xr_cols]], axis=1))

    y_prompt = y_prompt.reshape(batch, seq, d)
    y_sample = y_sample.reshape(n_seq, 1, d)
    return (y_prompt, y_sample, jnp.stack(hg_p), jnp.stack(rg_p), jnp.stack(cv_p),
            hg_s, jnp.stack(rg_s), jnp.stack(cv_s))
```

```python
import functools

import jax
import jax.numpy as jnp
from jax import lax
from jax.experimental import pallas as pl
from jax.experimental.pallas import tpu as pltpu

D_MODEL = 2048
HG_HEADS = 8
HG_DK = 128
HG_DV = 128
HG_F = HG_HEADS * HG_DK
HG_I = HG_HEADS * HG_DV
RG_WIDTH = 1024
RG_BLOCKS = 8
RG_BW = RG_WIDTH // RG_BLOCKS
CONV_W = 4
RG_C = 8.0
IN_COLS = 2 * HG_F + 2 * HG_I + 2 * RG_WIDTH + 2 * D_MODEL
N_EXPERTS = 8
TOP_K = 2
EPS = 1e-6

LANES = 128
SUBLANES = 8
BF16_SUBLANES = 16
VMEM_LIMIT = 56 * 1024 * 1024
EXPERT_VMEM_LIMIT = 62 * 1024 * 1024
HGRN_HEADS_PER_STEP = 4
HGRN_SAMPLE_ROWS = 16
HGRN_SAMPLE_HEADS = 2
RG_BLOCKS_PER_STEP = 4
EXPERT_ROWS = 512
EXPERT_WINDOW_SUBS = 3

BF16 = jnp.bfloat16
F32 = jnp.float32
I32 = jnp.int32


def _params(sem, vmem=VMEM_LIMIT):
    return pltpu.CompilerParams(dimension_semantics=sem, vmem_limit_bytes=vmem)


def _row_tile(m, target):
    best = None
    for t in range(BF16_SUBLANES, min(m, target) + 1, BF16_SUBLANES):
        if m % t == 0:
            best = t
    assert best is not None, (m, target)
    return best


def _dot(a, b):
    return jnp.dot(a, b, preferred_element_type=F32)


def _dot_nt(a, b):
    return lax.dot_general(a, b, (((1,), (1,)), ((), ())), preferred_element_type=F32)


def _dot_tn(a, b):
    return lax.dot_general(a, b, (((0,), (0,)), ((), ())), preferred_element_type=F32)


def _sigmoid(x):
    return 1.0 / (1.0 + jnp.exp(-x))


def _log_sigmoid(x):
    return jnp.minimum(x, 0.0) - jnp.log1p(jnp.exp(-jnp.abs(x)))


def _rms(x, g):
    ms = jnp.mean(x * x, axis=-1, keepdims=True)
    return x * lax.rsqrt(ms + EPS) * g


def _rmsnorm_kernel(x_ref, g_ref, o_ref):
    o_ref[...] = _rms(x_ref[...], g_ref[...]).astype(o_ref.dtype)


def _rmsnorm(x, g, out_dtype):
    m, d = x.shape
    tm = _row_tile(m, 640)
    return pl.pallas_call(
        _rmsnorm_kernel,
        grid=(m // tm,),
        in_specs=[pl.BlockSpec((tm, d), lambda i: (i, 0)),
                  pl.BlockSpec((1, d), lambda i: (0, 0))],
        out_specs=pl.BlockSpec((tm, d), lambda i: (i, 0)),
        out_shape=jax.ShapeDtypeStruct((m, d), out_dtype),
        compiler_params=_params(("parallel",)),
        name="rmsnorm",
    )(x, g.reshape(1, d))


def _in_proj_kernel(x_ref, w_ref, o_ref, wbf_ref):
    @pl.when(pl.program_id(1) == 0)
    def _():
        wbf_ref[...] = w_ref[...].astype(BF16)

    o_ref[...] = _dot(x_ref[...], wbf_ref[...])


def _in_proj(h, w_in, layer):
    m, d = h.shape
    n = w_in.shape[-1]
    tm = _row_tile(m, 1664)
    tn = 1024
    return pl.pallas_call(
        _in_proj_kernel,
        grid=(n // tn, m // tm),
        in_specs=[pl.BlockSpec((tm, d), lambda j, i: (i, 0)),
                  pl.BlockSpec((None, d, tn), lambda j, i: (layer, 0, j))],
        out_specs=pl.BlockSpec((tm, tn), lambda j, i: (i, j)),
        out_shape=jax.ShapeDtypeStruct((m, n), F32),
        scratch_shapes=[pltpu.VMEM((d, tn), BF16)],
        compiler_params=_params(("arbitrary", "arbitrary")),
        name="in_proj",
    )(h, w_in)


def _hgrn_gates(fz, lb, one_m_lb):
    u = jnp.exp(-jnp.abs(fz))
    r = 1.0 / (1.0 + u)
    nonneg = fz >= 0.0
    sig = jnp.where(nonneg, r, u * r)
    sig_neg = jnp.where(nonneg, u * r, r)
    f = lb + one_m_lb * sig
    return jnp.log(f), one_m_lb * sig_neg, f


def _hgrn_out(o, gate, gnorm):
    return _rms(o, gnorm) * (gate * _sigmoid(gate))


def _hgrn_prompt_kernel(q_ref, f_ref, i_ref, g_ref, lb_ref, omlb_ref, gn_ref, o_init_ref,
                        o_ref, s_out_ref, s_ref, *, chunk):
    del o_init_ref
    C = chunk
    c = pl.program_id(2)

    @pl.when(c == 0)
    def _():
        s_ref[...] = jnp.zeros_like(s_ref)

    row = lax.broadcasted_iota(I32, (C, HG_DK), 0)
    rc_xor = lax.broadcasted_iota(I32, (C, C), 0) ^ lax.broadcasted_iota(I32, (C, C), 1)

    for hh in range(HGRN_HEADS_PER_STEP):
        cols = slice(hh * HG_DK, (hh + 1) * HG_DK)
        q = q_ref[:, cols]
        v = i_ref[:, cols].astype(BF16)
        g, k, _ = _hgrn_gates(f_ref[:, cols], lb_ref[:, cols], omlb_ref[:, cols])

        a_mat = jnp.where(rc_xor == 0, _dot_nt(q.astype(BF16), k.astype(BF16)), 0.0)

        pre, suf, tot = g, jnp.zeros_like(g), g
        w = 1
        while w < C:
            upper = (row & w) != 0
            e = jnp.exp(jnp.where(upper, pre, suf))
            qw = jnp.where(upper, q * e, 0.0).astype(BF16)
            kw = jnp.where(upper, 0.0, k * e).astype(BF16)
            level = _dot_nt(qw, kw)
            a_mat = a_mat + (level if 2 * w == C else jnp.where(rc_xor < 2 * w, level, 0.0))
            up = pltpu.roll(tot, w, 0)
            dn = pltpu.roll(tot, C - w, 0)
            pre = jnp.where(upper, pre + up, pre)
            suf = jnp.where(upper, suf, suf + dn)
            tot = pre + suf
            w *= 2

        s_old = s_ref[hh]
        o = (_dot(a_mat.astype(BF16), v)
             + _dot((q * jnp.exp(pre)).astype(BF16), s_old.astype(BF16)))
        decay_rows = jnp.exp(jnp.broadcast_to(tot[0:1, :], (HG_DK, HG_DK))).T
        s_new = decay_rows * s_old + _dot_tn((k * jnp.exp(suf)).astype(BF16), v)
        s_ref[hh] = s_new

        o_ref[:, cols] = _hgrn_out(o, g_ref[:, cols], gn_ref[:, cols]).astype(o_ref.dtype)

    @pl.when(c == pl.num_programs(2) - 1)
    def _():
        s_out_ref[...] = s_ref[...]


def _hgrn_prompt(proj, lb, one_m_lb, gnorm, batch, seq, m_total, chunk):
    n_chunks = seq // chunk
    hps = HGRN_HEADS_PER_STEP
    width = hps * LANES
    groups = HG_HEADS // hps
    tok = lambda off: pl.BlockSpec((chunk, width), lambda b, h, c: (b * n_chunks + c, off + h))
    vec = pl.BlockSpec((1, width), lambda b, h, c: (0, h))
    return pl.pallas_call(
        functools.partial(_hgrn_prompt_kernel, chunk=chunk),
        grid=(batch, groups, n_chunks),
        in_specs=[tok(0), tok(groups), tok(2 * groups), tok(3 * groups), vec, vec, vec,
                  pl.BlockSpec(memory_space=pl.ANY)],
        input_output_aliases={7: 0},
        out_specs=[pl.BlockSpec((chunk, width), lambda b, h, c: (b * n_chunks + c, h)),
                   pl.BlockSpec((None, hps, HG_DK, HG_DV), lambda b, h, c: (b, h, 0, 0))],
        out_shape=[jax.ShapeDtypeStruct((m_total, HG_I), BF16),
                   jax.ShapeDtypeStruct((batch, HG_HEADS, HG_DK, HG_DV), F32)],
        scratch_shapes=[pltpu.VMEM((hps, HG_DK, HG_DV), F32)],
        compiler_params=_params(("parallel", "parallel", "arbitrary")),
        name="hgrn_prompt",
    )(proj, proj, proj, proj, lb, one_m_lb, gnorm, jnp.zeros((m_total, HG_I), BF16))


def _hgrn_sample_kernel(q_ref, f_ref, i_ref, g_ref, lb_ref, omlb_ref, gn_ref, s_ref, o_prev_ref,
                        *rest, layer, first):
    del o_prev_ref
    o_ref, s_out_ref = rest[-2:]
    n = HGRN_SAMPLE_ROWS
    seq_id = lax.broadcasted_iota(I32, (n, HG_DV), 0)

    def columns(x):
        pad = jnp.zeros((LANES - n, HG_DK), F32)
        return jnp.concatenate([x, pad], axis=0).T

    for hh in range(HGRN_SAMPLE_HEADS):
        cols = slice(hh * HG_DK, (hh + 1) * HG_DK)
        vi = i_ref[:, cols]
        _, k, f = _hgrn_gates(f_ref[:, cols], lb_ref[:, cols], omlb_ref[:, cols])
        f_t, k_t = columns(f), columns(k)
        qb = q_ref[:, cols].astype(BF16)
        o = jnp.zeros((n, HG_DV), F32)
        for j in range(n):
            bc = lambda xt: jnp.broadcast_to(xt[:, j:j + 1], (HG_DK, HG_DV))
            s_new = bc(f_t) * s_ref[j, hh] + bc(k_t) * vi[j:j + 1, :]
            if first:
                for l2 in range(s_out_ref.shape[0]):
                    s_out_ref[l2, j, hh] = s_new if l2 == layer else jnp.zeros_like(s_new)
            else:
                s_out_ref[j, hh] = s_new
            o_j = _dot(qb[j:j + 1, :], s_new.astype(BF16))
            o = jnp.where(seq_id == j, o_j, o)
        o_ref[:, cols] = _hgrn_out(o, g_ref[:, cols], gn_ref[:, cols]).astype(o_ref.dtype)


def _hgrn_sample(proj, lb, one_m_lb, gnorm, states, layer, o_all, row0, new_states):
    depth, n_seq = states.shape[:2]
    n = HGRN_SAMPLE_ROWS
    assert n_seq % n == 0 and row0 % n == 0
    blk0 = row0 // n
    first = new_states is None
    hps = HGRN_SAMPLE_HEADS
    width = hps * LANES
    groups = HG_HEADS // hps
    tok = lambda off: pl.BlockSpec((n, width), lambda j, h: (blk0 + j, off + h))
    vec = pl.BlockSpec((1, width), lambda j, h: (0, h))
    one_layer = pl.BlockSpec((None, n, hps, HG_DK, HG_DV), lambda j, h: (layer, j, h, 0, 0))
    all_layers = pl.BlockSpec((depth, n, hps, HG_DK, HG_DV), lambda j, h: (0, j, h, 0, 0))
    hbm = pl.BlockSpec(memory_space=pl.ANY)
    args = [proj, proj, proj, proj, lb, one_m_lb, gnorm, states, o_all]
    in_specs = [tok(0), tok(groups), tok(2 * groups), tok(3 * groups), vec, vec, vec,
                one_layer, hbm]
    aliases = {8: 0}
    if not first:
        args.append(new_states)
        in_specs.append(hbm)
        aliases[9] = 1
    return pl.pallas_call(
        functools.partial(_hgrn_sample_kernel, layer=layer, first=first),
        grid=(n_seq // n, groups),
        in_specs=in_specs,
        out_specs=[pl.BlockSpec((n, width), lambda j, h: (blk0 + j, h)),
                   all_layers if first else one_layer],
        out_shape=[jax.ShapeDtypeStruct(o_all.shape, o_all.dtype),
                   jax.ShapeDtypeStruct(states.shape, F32)],
        input_output_aliases=aliases,
        compiler_params=_params(("parallel", "parallel")),
        name="hgrn_sample",
    )(*args)


def _gelu_tanh(x):
    sqrt_2_over_pi = 0.7978845608028654
    return 0.5 * x * (1.0 + jnp.tanh(sqrt_2_over_pi * (x + 0.044715 * (x * x * x))))


def _rg_gates(conv, wa, ba, wx, bx, a_param):
    cb = conv.astype(BF16)
    r = _sigmoid(_dot(cb, wa.astype(BF16)) + ba)
    ig = _sigmoid(_dot(cb, wx.astype(BF16)) + bx)
    log_a = RG_C * r * _log_sigmoid(a_param)
    a = jnp.exp(log_a)
    t = jnp.tanh(log_a)
    y = -2.0 * t
    root = y * lax.rsqrt(jnp.maximum(y, 1e-30)) * lax.rsqrt(1.0 - t)
    b = root * ig * conv
    return a, b


def _rg_prompt_kernel(x_ref, gate_ref, cw_ref, cb_ref, wa_ref, ba_ref, wx_ref, bx_ref, ap_ref,
                      y_init_ref, y_ref, h_out_ref, xp_ref, h_ref, *, rows):
    del y_init_ref
    T = rows
    t = pl.program_id(2)

    @pl.when(t == 0)
    def _():
        xp_ref[0:SUBLANES, :] = jnp.zeros((SUBLANES, xp_ref.shape[1]), F32)
        h_ref[...] = jnp.zeros_like(h_ref)

    xp_ref[SUBLANES:SUBLANES + T, :] = x_ref[...]
    row = lax.broadcasted_iota(I32, (T, RG_BW), 0)

    for nb in range(RG_BLOCKS_PER_STEP):
        cols = slice(nb * RG_BW, (nb + 1) * RG_BW)
        xp = xp_ref[:, cols]
        conv = cb_ref[:, cols]
        for j in range(CONV_W):
            lag = CONV_W - 1 - j
            tap = xp if lag == 0 else pltpu.roll(xp, lag, 0)
            conv = conv + tap[SUBLANES:SUBLANES + T, :] * cw_ref[j:j + 1, cols]

        a, b = _rg_gates(conv, wa_ref[nb], ba_ref[nb], wx_ref[nb], bx_ref[nb], ap_ref[:, cols])

        first = row == 0
        b = jnp.where(first, a * h_ref[0:1, cols] + b, b)
        a = jnp.where(first, 0.0, a)
        d = 1
        while d < T:
            b = a * pltpu.roll(b, d, 0) + b
            a = a * pltpu.roll(a, d, 0)
            d *= 2
        h = b
        h_last = h[T - 1:T, :]
        h_ref[:, cols] = jnp.broadcast_to(h_last, (h_ref.shape[0], RG_BW))
        y_ref[:, cols] = (h * _gelu_tanh(gate_ref[:, cols])).astype(y_ref.dtype)

    xp_ref[0:SUBLANES, :] = xp_ref[T:T + SUBLANES, :]

    @pl.when(t == pl.num_programs(2) - 1)
    def _():
        h_out_ref[...] = h_ref[0:1, :]


def _rg_prompt(proj, cw, cb, wa, ba, wx, bx, ap, batch, seq, m_total, rows):
    n_t = seq // rows
    bps = RG_BLOCKS_PER_STEP
    width = bps * RG_BW
    groups = RG_BLOCKS // bps
    xoff = (2 * HG_F + 2 * HG_I) // width
    goff = xoff + groups
    tok = lambda off: pl.BlockSpec((rows, width), lambda b, n, t: (b * n_t + t, off + n))
    vec = pl.BlockSpec((1, width), lambda b, n, t: (0, n))
    blkw = pl.BlockSpec((bps, RG_BW, RG_BW), lambda b, n, t: (n, 0, 0))
    blkb = pl.BlockSpec((bps, 1, RG_BW), lambda b, n, t: (n, 0, 0))
    return pl.pallas_call(
        functools.partial(_rg_prompt_kernel, rows=rows),
        grid=(batch, groups, n_t),
        in_specs=[tok(xoff), tok(goff),
                  pl.BlockSpec((CONV_W, width), lambda b, n, t: (0, n)), vec,
                  blkw, blkb, blkw, blkb, vec, pl.BlockSpec(memory_space=pl.ANY)],
        input_output_aliases={9: 0},
        out_specs=[pl.BlockSpec((rows, width), lambda b, n, t: (b * n_t + t, n)),
                   pl.BlockSpec((None, 1, width), lambda b, n, t: (b, 0, n))],
        out_shape=[jax.ShapeDtypeStruct((m_total, RG_WIDTH), BF16),
                   jax.ShapeDtypeStruct((batch, 1, RG_WIDTH), F32)],
        scratch_shapes=[pltpu.VMEM((rows + SUBLANES, width), F32),
                        pltpu.VMEM((SUBLANES, width), F32)],
        compiler_params=_params(("parallel", "parallel", "arbitrary")),
        name="rg_prompt",
    )(proj, proj, cw, cb, wa, ba, wx, bx, ap, jnp.zeros((m_total, RG_WIDTH), BF16))


def _rg_sample_kernel(x_ref, gate_ref, b0_ref, b1_ref, b2_ref, h0_ref, cw_ref, cb_ref,
                      wa_ref, ba_ref, wx_ref, bx_ref, ap_ref, y_prev_ref, y_ref, h_out_ref):
    del y_prev_ref
    conv = cb_ref[...]
    for j, r in enumerate((b0_ref, b1_ref, b2_ref, x_ref)):
        conv = conv + r[...] * cw_ref[j:j + 1, :]
    a, b = _rg_gates(conv, wa_ref[...], ba_ref[...], wx_ref[...], bx_ref[...], ap_ref[...])
    h = a * h0_ref[...] + b
    h_out_ref[...] = h
    y_ref[...] = (h * _gelu_tanh(gate_ref[...])).astype(y_ref.dtype)


def _rg_sample(proj, conv_state, h0, cw, cb, wa, ba, wx, bx, ap, y_all, row0):
    n_seq = h0.shape[0]
    assert row0 % n_seq == 0 and n_seq % BF16_SUBLANES == 0
    blk0 = row0 // n_seq
    xoff = (2 * HG_F + 2 * HG_I) // LANES
    goff = xoff + RG_BLOCKS
    tok = lambda off: pl.BlockSpec((n_seq, LANES), lambda n: (blk0, off + n))
    buf = lambda j: pl.BlockSpec((n_seq, LANES), lambda n: (0, j * RG_BLOCKS + n))
    vec = pl.BlockSpec((1, LANES), lambda n: (0, n))
    blkw = pl.BlockSpec((None, RG_BW, RG_BW), lambda n: (n, 0, 0))
    blkb = pl.BlockSpec((None, 1, RG_BW), lambda n: (n, 0, 0))
    flat_state = conv_state.reshape(n_seq, (CONV_W - 1) * RG_WIDTH)
    return pl.pallas_call(
        _rg_sample_kernel,
        grid=(RG_BLOCKS,),
        in_specs=[tok(xoff), tok(goff), buf(0), buf(1), buf(2),
                  pl.BlockSpec((n_seq, LANES), lambda n: (0, n)),
                  pl.BlockSpec((CONV_W, LANES), lambda n: (0, n)), vec,
                  blkw, blkb, blkw, blkb, vec,
                  pl.BlockSpec(memory_space=pl.ANY)],
        out_specs=[pl.BlockSpec((n_seq, LANES), lambda n: (blk0, n)),
                   pl.BlockSpec((n_seq, LANES), lambda n: (0, n))],
        out_shape=[jax.ShapeDtypeStruct(y_all.shape, y_all.dtype),
                   jax.ShapeDtypeStruct((n_seq, RG_WIDTH), F32)],
        input_output_aliases={13: 0},
        compiler_params=_params(("parallel",)),
        name="rg_sample",
    )(proj, proj, flat_state, flat_state, flat_state, h0, cw, cb, wa, ba, wx, bx, ap, y_all)


def _merge_out_kernel(o_ref, y_ref, ma_ref, mb_ref, x_ref, wa_ref, wb_ref, wo_ref, g_ref,
                      xo_ref, h_ref, *maybe_h32_ref):
    merged = (_sigmoid(ma_ref[...]) * _dot(o_ref[...], wa_ref[...])
              + _sigmoid(mb_ref[...]) * _dot(y_ref[...], wb_ref[...]))
    x_new = x_ref[...] + _dot(merged.astype(BF16), wo_ref[...])
    xo_ref[...] = x_new
    h = _rms(x_new, g_ref[...])
    h_ref[...] = h.astype(h_ref.dtype)
    for r in maybe_h32_ref:
        r[...] = h


def _merge_out(o, y, proj, x, w_a, w_b, w_o, g, with_h32):
    m, d = x.shape
    tm = _row_tile(m, 320)
    moff = (2 * HG_F + 2 * HG_I + 2 * RG_WIDTH) // d
    row = lambda width, col: pl.BlockSpec((tm, width), lambda i: (i, col))
    res = lambda shape: pl.BlockSpec(shape, lambda i: (0, 0), pipeline_mode=pl.Buffered(1))
    n_out = 3 if with_h32 else 2
    return pl.pallas_call(
        _merge_out_kernel,
        grid=(m // tm,),
        in_specs=[row(HG_I, 0), row(RG_WIDTH, 0), row(d, moff), row(d, moff + 1), row(d, 0),
                  res(w_a.shape), res(w_b.shape), res(w_o.shape), res((1, d))],
        out_specs=[row(d, 0)] * n_out,
        out_shape=[jax.ShapeDtypeStruct((m, d), F32), jax.ShapeDtypeStruct((m, d), BF16),
                   jax.ShapeDtypeStruct((m, d), F32)][:n_out],
        compiler_params=_params(("parallel",)),
        name="merge_out",
    )(o, y, proj, proj, x, w_a, w_b, w_o, g.reshape(1, d))


def _ffn_kernel(h_ref, x_ref, wg_ref, wu_ref, wd_ref, g_ref, o_ref, xo_ref):
    f = pl.program_id(1)

    @pl.when(f == 0)
    def _():
        xo_ref[...] = jnp.zeros_like(xo_ref)

    h = h_ref[...]
    gate = _dot(h, wg_ref[...])
    act = (gate * _sigmoid(gate)) * _dot(h, wu_ref[...])
    xo_ref[...] += _dot(act.astype(BF16), wd_ref[...])

    @pl.when(f == pl.num_programs(1) - 1)
    def _():
        x_new = x_ref[...] + xo_ref[...]
        xo_ref[...] = x_new
        o_ref[...] = _rms(x_new, g_ref[...]).astype(o_ref.dtype)


def _ffn(h, x, wg, wu, wd, g, out_dtype, tf):
    m, d = x.shape
    ff = wg.shape[1]
    tm = _row_tile(m, 640)
    return pl.pallas_call(
        _ffn_kernel,
        grid=(m // tm, ff // tf),
        in_specs=[pl.BlockSpec((tm, d), lambda i, f: (i, 0)),
                  pl.BlockSpec((tm, d), lambda i, f: (i, 0)),
                  pl.BlockSpec((d, tf), lambda i, f: (0, f)),
                  pl.BlockSpec((d, tf), lambda i, f: (0, f)),
                  pl.BlockSpec((tf, d), lambda i, f: (f, 0)),
                  pl.BlockSpec((1, d), lambda i, f: (0, 0))],
        out_specs=[pl.BlockSpec((tm, d), lambda i, f: (i, 0)),
                   pl.BlockSpec((tm, d), lambda i, f: (i, 0))],
        out_shape=[jax.ShapeDtypeStruct((m, d), out_dtype), jax.ShapeDtypeStruct((m, d), F32)],
        compiler_params=_params(("parallel", "arbitrary")),
        name="ffn_dense",
    )(h, x, wg, wu, wd, g.reshape(1, d))


def _router_kernel(h_ref, w_ref, info_ref, gate_ref, cnt_ref, carry_ref):
    @pl.when(pl.program_id(0) == 0)
    def _():
        carry_ref[...] = jnp.zeros_like(carry_ref)

    logits = _dot(h_ref[...], w_ref[...])
    tm = logits.shape[0]
    lane = lax.broadcasted_iota(I32, logits.shape, 1)
    neg = jnp.float32(-jnp.inf)
    logits = jnp.where(lane < N_EXPERTS, logits, neg)
    m1 = jnp.max(logits, axis=-1, keepdims=True)
    i1 = jnp.min(jnp.where(logits == m1, lane, LANES), axis=-1, keepdims=True)
    rest = jnp.where(lane == i1, neg, logits)
    m2 = jnp.max(rest, axis=-1, keepdims=True)
    i2 = jnp.min(jnp.where(rest == m2, lane, LANES), axis=-1, keepdims=True)
    e = jnp.exp(m2 - m1)
    g1 = 1.0 / (1.0 + e)
    g2 = e / (1.0 + e)
    gate_ref[...] = jnp.where(lane == 0, g1, jnp.where(lane == 1, g2, 0.0))

    chosen = jnp.where((lane == i1) | (lane == i2), 1.0, 0.0)
    lower = jnp.where(lax.broadcasted_iota(I32, (tm, tm), 0) > lax.broadcasted_iota(I32, (tm, tm), 1),
                      1.0, 0.0).astype(BF16)
    before = _dot(lower, chosen.astype(BF16)) + carry_ref[0:1, :]
    r1 = jnp.sum(jnp.where(lane == i1, before, 0.0), axis=-1, keepdims=True).astype(I32)
    r2 = jnp.sum(jnp.where(lane == i2, before, 0.0), axis=-1, keepdims=True).astype(I32)
    info_ref[...] = jnp.where(lane == 0, i1, jnp.where(lane == 1, i2,
                              jnp.where(lane == 2, r1, jnp.where(lane == 3, r2, 0))))
    total = carry_ref[0:1, :] + jnp.sum(chosen, axis=0, keepdims=True)
    carry_ref[...] = jnp.broadcast_to(total, carry_ref.shape)
    cnt_ref[...] = jnp.broadcast_to(total, cnt_ref.shape)


def _router(h, w_router):
    m, d = h.shape
    tm = _row_tile(m, 640)
    w = jnp.pad(w_router, ((0, 0), (0, LANES - N_EXPERTS))).astype(BF16)
    return pl.pallas_call(
        _router_kernel,
        grid=(m // tm,),
        in_specs=[pl.BlockSpec((tm, d), lambda i: (i, 0)),
                  pl.BlockSpec((d, LANES), lambda i: (0, 0))],
        out_specs=[pl.BlockSpec((tm, LANES), lambda i: (i, 0)),
                   pl.BlockSpec((tm, LANES), lambda i: (i, 0)),
                   pl.BlockSpec((SUBLANES, LANES), lambda i: (0, 0))],
        out_shape=[jax.ShapeDtypeStruct((m, LANES), I32),
                   jax.ShapeDtypeStruct((m, LANES), F32),
                   jax.ShapeDtypeStruct((SUBLANES, LANES), F32)],
        scratch_shapes=[pltpu.VMEM((SUBLANES, LANES), F32)],
        compiler_params=_params(("arbitrary",)),
        name="router",
    )(h, w)


def _row_copy(src_ref, src_row, dst_ref, dst_row, sem):
    return pltpu.make_async_copy(src_ref.at[pl.ds(src_row, 1), :], dst_ref.at[pl.ds(dst_row, 1), :], sem)


def _rows_wait(src_ref, dst_ref, n_rows, sem):
    pltpu.make_async_copy(src_ref.at[pl.ds(0, n_rows), :], dst_ref.at[pl.ds(0, n_rows), :], sem).wait()


def _slots_kernel(pos_ref, src_ref, *, pad_mask):
    tb = pos_ref.shape[-1] // TOP_K
    base = pl.program_id(0) * tb

    @pl.when(pl.program_id(0) == 0)
    def _():
        def clear(p, carry):
            src_ref[p] = p & pad_mask
            return carry
        lax.fori_loop(0, src_ref.shape[0], clear, 0, unroll=8)

    def body(j, carry):
        for k in range(TOP_K):
            src_ref[pos_ref[0, TOP_K * j + k]] = base + j
        return carry

    lax.fori_loop(0, tb, body, 0, unroll=4)


def _slots(pos, n_rows):
    m = pos.shape[0]
    tb = _row_tile(m, 512)
    pad_mask = (1 << (m.bit_length() - 1)) - 1
    return pl.pallas_call(
        functools.partial(_slots_kernel, pad_mask=pad_mask),
        grid=(m // tb,),
        in_specs=[pl.BlockSpec((None, 1, TOP_K * tb), lambda i: (i, 0, 0), memory_space=pltpu.SMEM)],
        out_specs=pl.BlockSpec(memory_space=pltpu.SMEM),
        out_shape=jax.ShapeDtypeStruct((n_rows,), I32),
        compiler_params=_params(("arbitrary",)),
        name="moe_slots",
    )(pos.reshape(m // tb, 1, TOP_K * tb))


def _gather_kernel(ns_ref, src_ref, h_ref, xs_ref, buf_ref, sem):
    n_sub = ns_ref[pl.program_id(0)]
    for s in range(EXPERT_WINDOW_SUBS):
        lo = s * EXPERT_ROWS

        @pl.when(s < n_sub)
        def _():
            def body(r2, carry):
                for p in range(2):
                    r = lo + 2 * r2 + p
                    _row_copy(h_ref, src_ref[0, r], buf_ref, r, sem.at[s]).start(priority=p)
                return carry

            lax.fori_loop(0, EXPERT_ROWS // 2, body, 0, unroll=4)

    for s in range(EXPERT_WINDOW_SUBS):
        lo = s * EXPERT_ROWS
        rows = slice(lo, lo + EXPERT_ROWS)

        @pl.when(s < n_sub)
        def _():
            _rows_wait(h_ref, buf_ref.at[rows, :], EXPERT_ROWS, sem.at[s])
            xs_ref[rows, :] = buf_ref[rows, :].astype(xs_ref.dtype)

        @pl.when(s >= n_sub)
        def _():
            xs_ref[rows, :] = jnp.zeros((EXPERT_ROWS, xs_ref.shape[1]), xs_ref.dtype)


def _gather(h32, src, win_subs):
    m, d = h32.shape
    win = EXPERT_WINDOW_SUBS * EXPERT_ROWS
    assert m >= EXPERT_ROWS
    n_win = src.shape[0] // win
    return pl.pallas_call(
        _gather_kernel,
        grid=(n_win,),
        in_specs=[pl.BlockSpec(memory_space=pltpu.SMEM),
                  pl.BlockSpec((None, 1, win), lambda b: (b, 0, 0), memory_space=pltpu.SMEM),
                  pl.BlockSpec(memory_space=pl.ANY)],
        out_specs=pl.BlockSpec((win, d), lambda b: (b, 0)),
        out_shape=jax.ShapeDtypeStruct((n_win * win, d), BF16),
        scratch_shapes=[pltpu.VMEM((win, d), F32),
                        pltpu.SemaphoreType.DMA((EXPERT_WINDOW_SUBS,))],
        compiler_params=_params(("arbitrary",)),
        name="moe_gather",
    )(win_subs, src.reshape(n_win, 1, win), h32)


def _experts_kernel(we_ref, ns_ref, nv_ref, x_ref, wg_ref, wu_ref, wd_ref, y_ref,
                    wg_bf, wu_bf, wd_bf):
    del we_ref
    b = pl.program_id(0)

    @pl.when(pl.program_id(1) == 0)
    def _():
        y_ref[...] = jnp.zeros_like(y_ref)

    def sub_block(s):
        rows = slice(s * EXPERT_ROWS, (s + 1) * EXPERT_ROWS)
        h = x_ref[rows, :]
        gate = _dot(h, wg_bf[...])
        act = (gate * _sigmoid(gate)) * _dot(h, wu_bf[...])
        y_ref[rows, :] += _dot(act.astype(BF16), wd_bf[...])

    @pl.when(b < nv_ref[0])
    def _():
        wg_bf[...] = wg_ref[...].astype(BF16)
        wu_bf[...] = wu_ref[...].astype(BF16)
        wd_bf[...] = wd_ref[...].astype(BF16)
        sub_block(0)
        for s in range(1, EXPERT_WINDOW_SUBS):
            @pl.when(s < ns_ref[b])
            def _():
                sub_block(s)


def _experts(xs, wg, wu, wd, layer, win_expert, win_subs, n_valid, tf):
    n_rows, d = xs.shape
    ff = wg.shape[-1]
    n_f = ff // tf
    win = EXPERT_WINDOW_SUBS * EXPERT_ROWS
    xmap = lambda b, f, we, ns, nv: (jnp.minimum(b, nv[0] - 1), 0)
    fidx = lambda b, f, nv: jnp.where(b < nv[0], f, n_f - 1)
    once = pl.Buffered(1)
    grid_spec = pltpu.PrefetchScalarGridSpec(
        num_scalar_prefetch=3,
        grid=(n_rows // win, n_f),
        in_specs=[pl.BlockSpec((win, d), xmap),
                  pl.BlockSpec((None, None, d, tf),
                               lambda b, f, we, ns, nv: (layer, we[b], 0, fidx(b, f, nv))),
                  pl.BlockSpec((None, None, d, tf),
                               lambda b, f, we, ns, nv: (layer, we[b], 0, fidx(b, f, nv))),
                  pl.BlockSpec((None, None, tf, d),
                               lambda b, f, we, ns, nv: (layer, we[b], fidx(b, f, nv), 0))],
        out_specs=pl.BlockSpec((win, d), lambda b, f, we, ns, nv: (b, 0), pipeline_mode=once),
        scratch_shapes=[pltpu.VMEM((d, tf), BF16), pltpu.VMEM((d, tf), BF16),
                        pltpu.VMEM((tf, d), BF16)],
    )
    return pl.pallas_call(
        _experts_kernel,
        grid_spec=grid_spec,
        out_shape=jax.ShapeDtypeStruct((n_rows, d), F32),
        compiler_params=_params(("arbitrary", "arbitrary"), vmem=EXPERT_VMEM_LIMIT),
        name="moe_experts",
    )(win_expert, win_subs, n_valid, xs, wg, wu, wd)


def _combine_kernel(pos_ref, x_ref, gate_ref, g_ref, y_ref, o_ref, ya_ref, yb_ref, sem):
    tb = x_ref.shape[0]

    def body(j, carry):
        _row_copy(y_ref, pos_ref[0, TOP_K * j], ya_ref, j, sem).start(priority=0)
        _row_copy(y_ref, pos_ref[0, TOP_K * j + 1], yb_ref, j, sem).start(priority=1)
        return carry

    lax.fori_loop(0, tb, body, 0, unroll=4)
    _rows_wait(y_ref, ya_ref, tb, sem)
    _rows_wait(y_ref, yb_ref, tb, sem)
    gate = gate_ref[...]
    x_new = x_ref[...] + (gate[:, 0:1] * ya_ref[...] + gate[:, 1:2] * yb_ref[...])
    o_ref[...] = _rms(x_new, g_ref[...]).astype(o_ref.dtype)


def _combine(x, y, pos, gates, g, out_dtype, row0, n_rows):
    d = x.shape[1]
    tb = _row_tile(n_rows, 512)
    assert row0 % tb == 0
    blk0 = row0 // tb
    pos = pos.reshape(-1)[TOP_K * row0:TOP_K * (row0 + n_rows)].reshape(n_rows // tb, 1, TOP_K * tb)
    return pl.pallas_call(
        _combine_kernel,
        grid=(n_rows // tb,),
        in_specs=[pl.BlockSpec((None, 1, TOP_K * tb), lambda i: (i, 0, 0), memory_space=pltpu.SMEM),
                  pl.BlockSpec((tb, d), lambda i: (blk0 + i, 0)),
                  pl.BlockSpec((tb, LANES), lambda i: (blk0 + i, 0)),
                  pl.BlockSpec((1, d), lambda i: (0, 0)),
                  pl.BlockSpec(memory_space=pl.ANY)],
        out_specs=pl.BlockSpec((tb, d), lambda i: (i, 0)),
        out_shape=jax.ShapeDtypeStruct((n_rows, d), out_dtype),
        scratch_shapes=[pltpu.VMEM((tb, d), F32), pltpu.VMEM((tb, d), F32),
                        pltpu.SemaphoreType.DMA(())],
        compiler_params=_params(("arbitrary",)),
        name="moe_combine",
    )(pos, x, gates, g.reshape(1, d), y)


def _moe(h, h32, x, w_router, wg, wu, wd, layer, g, out_dtype, m_prompt):
    m, d = x.shape
    sub, wsubs = EXPERT_ROWS, EXPERT_WINDOW_SUBS
    win = sub * wsubs
    info, gates, cnt = _router(h, w_router)

    counts = cnt[0, :N_EXPERTS].astype(I32)
    n_sub = (counts + (sub - 1)) // sub
    n_win = (n_sub + (wsubs - 1)) // wsubs
    per_win = (n_sub + jnp.maximum(n_win, 1) - 1) // jnp.maximum(n_win, 1)
    win_end = jnp.cumsum(n_win)
    win_first = win_end - n_win
    n_valid = win_end[-1]
    max_subs = (TOP_K * m + N_EXPERTS * (sub - 1)) // sub
    max_windows = (max_subs + N_EXPERTS * (wsubs - 1)) // wsubs
    wid = jnp.arange(max_windows, dtype=I32)
    owner = jnp.minimum(jnp.sum((wid[:, None] >= win_end[None, :]).astype(I32), axis=1),
                        N_EXPERTS - 1)
    subs_here = jnp.clip(n_sub[owner] - (wid - win_first[owner]) * per_win[owner], 0, per_win[owner])
    used = wid < n_valid
    win_expert = jnp.where(used, owner, owner[n_valid - 1])
    win_subs = jnp.where(used, subs_here, 0)

    e_tok, r_tok = info[:, :TOP_K], info[:, TOP_K:2 * TOP_K]
    rows_per_win = (per_win * sub)[e_tok]
    max_win_per_expert = -(-(-(-m // sub)) // wsubs)
    win_in_expert = sum((r_tok >= k * rows_per_win).astype(I32) for k in range(1, max_win_per_expert))
    pos = (win_first[e_tok] + win_in_expert) * win + (r_tok - win_in_expert * rows_per_win)

    src = _slots(pos, max_windows * win)
    xs = _gather(h32, src, win_subs)
    y = _experts(xs, wg, wu, wd, layer, win_expert, win_subs, n_valid.reshape(1), 512)
    return (_combine(x, y, pos, gates, g, out_dtype, 0, m_prompt),
            _combine(x, y, pos, gates, g, out_dtype, m_prompt, m - m_prompt))


def _bf16_padded(w, axis, mult):
    shape = list(w.shape)
    shape[axis] += (-shape[axis]) % mult
    return lax.dynamic_update_slice(jnp.zeros(shape, BF16), w.astype(BF16), (0,) * w.ndim)


def kernel(x_prompt, x_sample, state_hgrn, state_rglru, state_conv, hgrn_lb_logits, norm_mix, w_in,
           hgrn_gnorm, rg_conv_w, rg_conv_b, rg_wa, rg_ba, rg_wx, rg_bx, rg_a_param, w_br_a, w_br_b,
           w_out, norm_ffn, ffn_w_gate, ffn_w_up, ffn_w_down, moe_router, moe_w_gate, moe_w_up,
           moe_w_down, norm_final):
    depth = w_in.shape[0]
    batch, seq, d = x_prompt.shape
    n_seq = x_sample.shape[0]
    m_prompt = batch * seq
    m = m_prompt + n_seq

    sm = jax.nn.softmax(hgrn_lb_logits.astype(F32), axis=0)
    lbs = jnp.maximum(jnp.cumsum(sm, axis=0) - sm[0:1], 0.0)
    one_m_lbs = 1.0 - lbs

    x = jnp.concatenate([x_prompt.reshape(m_prompt, d), x_sample.reshape(n_seq, d)], axis=0)
    h = _rmsnorm(x, norm_mix[0], BF16)

    chunk = 256 if seq % 256 == 0 else seq
    rg_rows = 512 if seq % 512 == 0 else chunk
    hg_p, rg_p, cv_p, rg_s, cv_s = [], [], [], [], []
    hg_s = None
    for l in range(depth):
        proj = _in_proj(h, w_in, l)
        vrow = lambda a: a[l].reshape(1, -1)
        o_all, s_prompt = _hgrn_prompt(proj, vrow(lbs), vrow(one_m_lbs), vrow(hgrn_gnorm),
                                       batch, seq, m, chunk)
        o_all, hg_s = _hgrn_sample(proj, vrow(lbs), vrow(one_m_lbs), vrow(hgrn_gnorm),
                                   state_hgrn, l, o_all, m_prompt, hg_s)
        rg_args = (rg_conv_w[l], vrow(rg_conv_b), rg_wa[l], rg_ba[l].reshape(RG_BLOCKS, 1, RG_BW),
                   rg_wx[l], rg_bx[l].reshape(RG_BLOCKS, 1, RG_BW), vrow(rg_a_param))
        y_all, h_prompt = _rg_prompt(proj, *rg_args, batch, seq, m, rg_rows)
        y_all, h_sample = _rg_sample(proj, state_conv[l], state_rglru[l], *rg_args, y_all, m_prompt)

        dense = l % 2 == 0
        merged = _merge_out(o_all, y_all, proj, x, w_br_a[l].astype(BF16), w_br_b[l].astype(BF16),
                            w_out[l].astype(BF16), norm_ffn[l], with_h32=not dense)
        x, h2 = merged[0], merged[1]

        last = l == depth - 1
        g_next = norm_final if last else norm_mix[l + 1]
        out_dtype = F32 if last else BF16
        j = l // 2
        if dense:
            wg = _bf16_padded(ffn_w_gate[j], 1, 512)
            wu = _bf16_padded(ffn_w_up[j], 1, 512)
            wd = _bf16_padded(ffn_w_down[j], 0, 512)
            h, x = _ffn(h2, x, wg, wu, wd, g_next, out_dtype, 512)
        else:
            assert last
            y_prompt, y_sample = _moe(h2, merged[2], x, moe_router[j], moe_w_gate, moe_w_up,
                                      moe_w_down, j, g_next, out_dtype, m_prompt)

        xr_cols = slice(2 * HG_F + 2 * HG_I, 2 * HG_F + 2 * HG_I + RG_WIDTH)
        tail = jnp.stack([proj[(b + 1) * seq - (CONV_W - 1):(b + 1) * seq, xr_cols]
                          for b in range(batch)])
        hg_p.append(s_prompt)
        rg_p.append(h_prompt.reshape(batch, RG_WIDTH))
        cv_p.append(tail)
        rg_s.append(h_sample)
        cv_s.append(jnp.concatenate([state_conv[l][:, 1:], proj[m_prompt:, None, xr_cols]], axis=1))

    y_prompt = y_prompt.reshape(batch, seq, d)
    y_sample = y_sample.reshape(n_seq, 1, d)
    return (y_prompt, y_sample, jnp.stack(hg_p), jnp.stack(rg_p), jnp.stack(cv_p),
            hg_s, jnp.stack(rg_s), jnp.stack(cv_s))
```

```python
import functools

import jax
import jax.numpy as jnp
from jax import lax
from jax.experimental import pallas as pl
from jax.experimental.pallas import tpu as pltpu

D_MODEL = 2048
HG_HEADS = 8
HG_DK = 128
HG_DV = 128
HG_F = HG_HEADS * HG_DK
HG_I = HG_HEADS * HG_DV
RG_WIDTH = 1024
RG_BLOCKS = 8
RG_BW = RG_WIDTH // RG_BLOCKS
CONV_W = 4
RG_C = 8.0
IN_COLS = 2 * HG_F + 2 * HG_I + 2 * RG_WIDTH + 2 * D_MODEL
N_EXPERTS = 8
TOP_K = 2
EPS = 1e-6

LANES = 128
SUBLANES = 8
BF16_SUBLANES = 16
VMEM_LIMIT = 56 * 1024 * 1024
EXPERT_VMEM_LIMIT = 62 * 1024 * 1024
HGRN_HEADS_PER_STEP = 4
HGRN_SAMPLE_ROWS = 16
HGRN_SAMPLE_HEADS = 2
RG_BLOCKS_PER_STEP = 4
EXPERT_ROWS = 512
EXPERT_WINDOW_SUBS = 3

BF16 = jnp.bfloat16
F32 = jnp.float32
I32 = jnp.int32


def _params(sem, vmem=VMEM_LIMIT):
    return pltpu.CompilerParams(dimension_semantics=sem, vmem_limit_bytes=vmem)


def _row_tile(m, target):
    best = None
    for t in range(BF16_SUBLANES, min(m, target) + 1, BF16_SUBLANES):
        if m % t == 0:
            best = t
    assert best is not None, (m, target)
    return best


def _dot(a, b):
    return jnp.dot(a, b, preferred_element_type=F32)


def _dot_nt(a, b):
    return lax.dot_general(a, b, (((1,), (1,)), ((), ())), preferred_element_type=F32)


def _dot_tn(a, b):
    return lax.dot_general(a, b, (((0,), (0,)), ((), ())), preferred_element_type=F32)


def _sigmoid(x):
    return 1.0 / (1.0 + jnp.exp(-x))


def _log_sigmoid(x):
    return jnp.minimum(x, 0.0) - jnp.log1p(jnp.exp(-jnp.abs(x)))


def _rms(x, g):
    ms = jnp.mean(x * x, axis=-1, keepdims=True)
    return x * lax.rsqrt(ms + EPS) * g


def _rmsnorm_kernel(x_ref, g_ref, o_ref):
    o_ref[...] = _rms(x_ref[...], g_ref[...]).astype(o_ref.dtype)


def _rmsnorm(x, g, out_dtype):
    m, d = x.shape
    tm = _row_tile(m, 640)
    return pl.pallas_call(
        _rmsnorm_kernel,
        grid=(m // tm,),
        in_specs=[pl.BlockSpec((tm, d), lambda i: (i, 0)),
                  pl.BlockSpec((1, d), lambda i: (0, 0))],
        out_specs=pl.BlockSpec((tm, d), lambda i: (i, 0)),
        out_shape=jax.ShapeDtypeStruct((m, d), out_dtype),
        compiler_params=_params(("parallel",)),
        name="rmsnorm",
    )(x, g.reshape(1, d))


def _in_proj_kernel(x_ref, w_ref, o_ref, wbf_ref):
    @pl.when(pl.program_id(1) == 0)
    def _():
        wbf_ref[...] = w_ref[...].astype(BF16)

    o_ref[...] = _dot(x_ref[...], wbf_ref[...])


def _in_proj(h, w_in, layer):
    m, d = h.shape
    n = w_in.shape[-1]
    tm = _row_tile(m, 1664)
    tn = 1024
    return pl.pallas_call(
        _in_proj_kernel,
        grid=(n // tn, m // tm),
        in_specs=[pl.BlockSpec((tm, d), lambda j, i: (i, 0)),
                  pl.BlockSpec((None, d, tn), lambda j, i: (layer, 0, j))],
        out_specs=pl.BlockSpec((tm, tn), lambda j, i: (i, j)),
        out_shape=jax.ShapeDtypeStruct((m, n), F32),
        scratch_shapes=[pltpu.VMEM((d, tn), BF16)],
        compiler_params=_params(("arbitrary", "arbitrary")),
        name="in_proj",
    )(h, w_in)


def _hgrn_gates(fz, lb, one_m_lb):
    u = jnp.exp(-jnp.abs(fz))
    r = 1.0 / (1.0 + u)
    nonneg = fz >= 0.0
    sig = jnp.where(nonneg, r, u * r)
    sig_neg = jnp.where(nonneg, u * r, r)
    f = lb + one_m_lb * sig
    return jnp.log(f), one_m_lb * sig_neg, f


def _hgrn_out(o, gate, gnorm):
    return _rms(o, gnorm) * (gate * _sigmoid(gate))


def _hgrn_prompt_kernel(q_ref, f_ref, i_ref, g_ref, lb_ref, omlb_ref, gn_ref, o_init_ref,
                        o_ref, s_out_ref, s_ref, *, chunk):
    del o_init_ref
    C = chunk
    c = pl.program_id(2)

    @pl.when(c == 0)
    def _():
        s_ref[...] = jnp.zeros_like(s_ref)

    row = lax.broadcasted_iota(I32, (C, HG_DK), 0)
    rc_xor = lax.broadcasted_iota(I32, (C, C), 0) ^ lax.broadcasted_iota(I32, (C, C), 1)

    for hh in range(HGRN_HEADS_PER_STEP):
        cols = slice(hh * HG_DK, (hh + 1) * HG_DK)
        q = q_ref[:, cols]
        v = i_ref[:, cols].astype(BF16)
        g, k, _ = _hgrn_gates(f_ref[:, cols], lb_ref[:, cols], omlb_ref[:, cols])

        a_mat = jnp.where(rc_xor == 0, _dot_nt(q.astype(BF16), k.astype(BF16)), 0.0)

        pre, suf, tot = g, jnp.zeros_like(g), g
        w = 1
        while w < C:
            upper = (row & w) != 0
            e = jnp.exp(jnp.where(upper, pre, suf))
            qw = jnp.where(upper, q * e, 0.0).astype(BF16)
            kw = jnp.where(upper, 0.0, k * e).astype(BF16)
            level = _dot_nt(qw, kw)
            a_mat = a_mat + (level if 2 * w == C else jnp.where(rc_xor < 2 * w, level, 0.0))
            up = pltpu.roll(tot, w, 0)
            dn = pltpu.roll(tot, C - w, 0)
            pre = jnp.where(upper, pre + up, pre)
            suf = jnp.where(upper, suf, suf + dn)
            tot = pre + suf
            w *= 2

        s_old = s_ref[hh]
        o = (_dot(a_mat.astype(BF16), v)
             + _dot((q * jnp.exp(pre)).astype(BF16), s_old.astype(BF16)))
        decay_rows = jnp.exp(jnp.broadcast_to(tot[0:1, :], (HG_DK, HG_DK))).T
        s_new = decay_rows * s_old + _dot_tn((k * jnp.exp(suf)).astype(BF16), v)
        s_ref[hh] = s_new

        o_ref[:, cols] = _hgrn_out(o, g_ref[:, cols], gn_ref[:, cols]).astype(o_ref.dtype)

    @pl.when(c == pl.num_programs(2) - 1)
    def _():
        s_out_ref[...] = s_ref[...]


def _hgrn_prompt(proj, lb, one_m_lb, gnorm, batch, seq, m_total, chunk):
    n_chunks = seq // chunk
    hps = HGRN_HEADS_PER_STEP
    width = hps * LANES
    groups = HG_HEADS // hps
    tok = lambda off: pl.BlockSpec((chunk, width), lambda b, h, c: (b * n_chunks + c, off + h))
    vec = pl.BlockSpec((1, width), lambda b, h, c: (0, h))
    return pl.pallas_call(
        functools.partial(_hgrn_prompt_kernel, chunk=chunk),
        grid=(batch, groups, n_chunks),
        in_specs=[tok(0), tok(groups), tok(2 * groups), tok(3 * groups), vec, vec, vec,
                  pl.BlockSpec(memory_space=pl.ANY)],
        input_output_aliases={7: 0},
        out_specs=[pl.BlockSpec((chunk, width), lambda b, h, c: (b * n_chunks + c, h)),
                   pl.BlockSpec((None, hps, HG_DK, HG_DV), lambda b, h, c: (b, h, 0, 0))],
        out_shape=[jax.ShapeDtypeStruct((m_total, HG_I), BF16),
                   jax.ShapeDtypeStruct((batch, HG_HEADS, HG_DK, HG_DV), F32)],
        scratch_shapes=[pltpu.VMEM((hps, HG_DK, HG_DV), F32)],
        compiler_params=_params(("parallel", "parallel", "arbitrary")),
        name="hgrn_prompt",
    )(proj, proj, proj, proj, lb, one_m_lb, gnorm, jnp.zeros((m_total, HG_I), BF16))


def _hgrn_sample_kernel(q_ref, f_ref, i_ref, g_ref, lb_ref, omlb_ref, gn_ref, s_ref, o_prev_ref,
                        *rest, layer, first):
    del o_prev_ref
    o_ref, s_out_ref = rest[-2:]
    n = HGRN_SAMPLE_ROWS
    seq_id = lax.broadcasted_iota(I32, (n, HG_DV), 0)

    def columns(x):
        pad = jnp.zeros((LANES - n, HG_DK), F32)
        return jnp.concatenate([x, pad], axis=0).T

    for hh in range(HGRN_SAMPLE_HEADS):
        cols = slice(hh * HG_DK, (hh + 1) * HG_DK)
        vi = i_ref[:, cols]
        _, k, f = _hgrn_gates(f_ref[:, cols], lb_ref[:, cols], omlb_ref[:, cols])
        f_t, k_t = columns(f), columns(k)
        qb = q_ref[:, cols].astype(BF16)
        o = jnp.zeros((n, HG_DV), F32)
        for j in range(n):
            bc = lambda xt: jnp.broadcast_to(xt[:, j:j + 1], (HG_DK, HG_DV))
            s_new = bc(f_t) * s_ref[j, hh] + bc(k_t) * vi[j:j + 1, :]
            if first:
                for l2 in range(s_out_ref.shape[0]):
                    s_out_ref[l2, j, hh] = s_new if l2 == layer else jnp.zeros_like(s_new)
            else:
                s_out_ref[j, hh] = s_new
            o_j = _dot(qb[j:j + 1, :], s_new.astype(BF16))
            o = jnp.where(seq_id == j, o_j, o)
        o_ref[:, cols] = _hgrn_out(o, g_ref[:, cols], gn_ref[:, cols]).astype(o_ref.dtype)


def _hgrn_sample(proj, lb, one_m_lb, gnorm, states, layer, o_all, row0, new_states):
    depth, n_seq = states.shape[:2]
    n = HGRN_SAMPLE_ROWS
    assert n_seq % n == 0 and row0 % n == 0
    blk0 = row0 // n
    first = new_states is None
    hps = HGRN_SAMPLE_HEADS
    width = hps * LANES
    groups = HG_HEADS // hps
    tok = lambda off: pl.BlockSpec((n, width), lambda j, h: (blk0 + j, off + h))
    vec = pl.BlockSpec((1, width), lambda j, h: (0, h))
    one_layer = pl.BlockSpec((None, n, hps, HG_DK, HG_DV), lambda j, h: (layer, j, h, 0, 0))
    all_layers = pl.BlockSpec((depth, n, hps, HG_DK, HG_DV), lambda j, h: (0, j, h, 0, 0))
    hbm = pl.BlockSpec(memory_space=pl.ANY)
    args = [proj, proj, proj, proj, lb, one_m_lb, gnorm, states, o_all]
    in_specs = [tok(0), tok(groups), tok(2 * groups), tok(3 * groups), vec, vec, vec,
                one_layer, hbm]
    aliases = {8: 0}
    if not first:
        args.append(new_states)
        in_specs.append(hbm)
        aliases[9] = 1
    return pl.pallas_call(
        functools.partial(_hgrn_sample_kernel, layer=layer, first=first),
        grid=(n_seq // n, groups),
        in_specs=in_specs,
        out_specs=[pl.BlockSpec((n, width), lambda j, h: (blk0 + j, h)),
                   all_layers if first else one_layer],
        out_shape=[jax.ShapeDtypeStruct(o_all.shape, o_all.dtype),
                   jax.ShapeDtypeStruct(states.shape, F32)],
        input_output_aliases=aliases,
        compiler_params=_params(("parallel", "parallel")),
        name="hgrn_sample",
    )(*args)


def _gelu_tanh(x):
    sqrt_2_over_pi = 0.7978845608028654
    return 0.5 * x * (1.0 + jnp.tanh(sqrt_2_over_pi * (x + 0.044715 * (x * x * x))))


def _rg_gates(conv, wa, ba, wx, bx, a_param):
    cb = conv.astype(BF16)
    r = _sigmoid(_dot(cb, wa.astype(BF16)) + ba)
    ig = _sigmoid(_dot(cb, wx.astype(BF16)) + bx)
    log_a = RG_C * r * _log_sigmoid(a_param)
    a = jnp.exp(log_a)
    t = jnp.tanh(log_a)
    y = -2.0 * t
    root = y * lax.rsqrt(jnp.maximum(y, 1e-30)) * lax.rsqrt(1.0 - t)
    b = root * ig * conv
    return a, b


def _rg_prompt_kernel(x_ref, gate_ref, cw_ref, cb_ref, wa_ref, ba_ref, wx_ref, bx_ref, ap_ref,
                      y_init_ref, y_ref, h_out_ref, xp_ref, h_ref, *, rows):
    del y_init_ref
    T = rows
    t = pl.program_id(2)

    @pl.when(t == 0)
    def _():
        xp_ref[0:SUBLANES, :] = jnp.zeros((SUBLANES, xp_ref.shape[1]), F32)
        h_ref[...] = jnp.zeros_like(h_ref)

    xp_ref[SUBLANES:SUBLANES + T, :] = x_ref[...]
    row = lax.broadcasted_iota(I32, (T, RG_BW), 0)

    for nb in range(RG_BLOCKS_PER_STEP):
        cols = slice(nb * RG_BW, (nb + 1) * RG_BW)
        xp = xp_ref[:, cols]
        conv = cb_ref[:, cols]
        for j in range(CONV_W):
            lag = CONV_W - 1 - j
            tap = xp if lag == 0 else pltpu.roll(xp, lag, 0)
            conv = conv + tap[SUBLANES:SUBLANES + T, :] * cw_ref[j:j + 1, cols]

        a, b = _rg_gates(conv, wa_ref[nb], ba_ref[nb], wx_ref[nb], bx_ref[nb], ap_ref[:, cols])

        first = row == 0
        b = jnp.where(first, a * h_ref[0:1, cols] + b, b)
        a = jnp.where(first, 0.0, a)
        d = 1
        while d < T:
            b = a * pltpu.roll(b, d, 0) + b
            a = a * pltpu.roll(a, d, 0)
            d *= 2
        h = b
        h_last = h[T - 1:T, :]
        h_ref[:, cols] = jnp.broadcast_to(h_last, (h_ref.shape[0], RG_BW))
        y_ref[:, cols] = (h * _gelu_tanh(gate_ref[:, cols])).astype(y_ref.dtype)

    xp_ref[0:SUBLANES, :] = xp_ref[T:T + SUBLANES, :]

    @pl.when(t == pl.num_programs(2) - 1)
    def _():
        h_out_ref[...] = h_ref[0:1, :]


def _rg_prompt(proj, cw, cb, wa, ba, wx, bx, ap, batch, seq, m_total, rows):
    n_t = seq // rows
    bps = RG_BLOCKS_PER_STEP
    width = bps * RG_BW
    groups = RG_BLOCKS // bps
    xoff = (2 * HG_F + 2 * HG_I) // width
    goff = xoff + groups
    tok = lambda off: pl.BlockSpec((rows, width), lambda b, n, t: (b * n_t + t, off + n))
    vec = pl.BlockSpec((1, width), lambda b, n, t: (0, n))
    blkw = pl.BlockSpec((bps, RG_BW, RG_BW), lambda b, n, t: (n, 0, 0))
    blkb = pl.BlockSpec((bps, 1, RG_BW), lambda b, n, t: (n, 0, 0))
    return pl.pallas_call(
        functools.partial(_rg_prompt_kernel, rows=rows),
        grid=(batch, groups, n_t),
        in_specs=[tok(xoff), tok(goff),
                  pl.BlockSpec((CONV_W, width), lambda b, n, t: (0, n)), vec,
                  blkw, blkb, blkw, blkb, vec, pl.BlockSpec(memory_space=pl.ANY)],
        input_output_aliases={9: 0},
        out_specs=[pl.BlockSpec((rows, width), lambda b, n, t: (b * n_t + t, n)),
                   pl.BlockSpec((None, 1, width), lambda b, n, t: (b, 0, n))],
        out_shape=[jax.ShapeDtypeStruct((m_total, RG_WIDTH), BF16),
                   jax.ShapeDtypeStruct((batch, 1, RG_WIDTH), F32)],
        scratch_shapes=[pltpu.VMEM((rows + SUBLANES, width), F32),
                        pltpu.VMEM((SUBLANES, width), F32)],
        compiler_params=_params(("parallel", "parallel", "arbitrary")),
        name="rg_prompt",
    )(proj, proj, cw, cb, wa, ba, wx, bx, ap, jnp.zeros((m_total, RG_WIDTH), BF16))


def _rg_sample_kernel(x_ref, gate_ref, b0_ref, b1_ref, b2_ref, h0_ref, cw_ref, cb_ref,
                      wa_ref, ba_ref, wx_ref, bx_ref, ap_ref, y_prev_ref, y_ref, h_out_ref):
    del y_prev_ref
    conv = cb_ref[...]
    for j, r in enumerate((b0_ref, b1_ref, b2_ref, x_ref)):
        conv = conv + r[...] * cw_ref[j:j + 1, :]
    a, b = _rg_gates(conv, wa_ref[...], ba_ref[...], wx_ref[...], bx_ref[...], ap_ref[...])
    h = a * h0_ref[...] + b
    h_out_ref[...] = h
    y_ref[...] = (h * _gelu_tanh(gate_ref[...])).astype(y_ref.dtype)


def _rg_sample(proj, conv_state, h0, cw, cb, wa, ba, wx, bx, ap, y_all, row0):
    n_seq = h0.shape[0]
    assert row0 % n_seq == 0 and n_seq % BF16_SUBLANES == 0
    blk0 = row0 // n_seq
    xoff = (2 * HG_F + 2 * HG_I) // LANES
    goff = xoff + RG_BLOCKS
    tok = lambda off: pl.BlockSpec((n_seq, LANES), lambda n: (blk0, off + n))
    buf = lambda j: pl.BlockSpec((n_seq, LANES), lambda n: (0, j * RG_BLOCKS + n))
    vec = pl.BlockSpec((1, LANES), lambda n: (0, n))
    blkw = pl.BlockSpec((None, RG_BW, RG_BW), lambda n: (n, 0, 0))
    blkb = pl.BlockSpec((None, 1, RG_BW), lambda n: (n, 0, 0))
    flat_state = conv_state.reshape(n_seq, (CONV_W - 1) * RG_WIDTH)
    return pl.pallas_call(
        _rg_sample_kernel,
        grid=(RG_BLOCKS,),
        in_specs=[tok(xoff), tok(goff), buf(0), buf(1), buf(2),
                  pl.BlockSpec((n_seq, LANES), lambda n: (0, n)),
                  pl.BlockSpec((CONV_W, LANES), lambda n: (0, n)), vec,
                  blkw, blkb, blkw, blkb, vec,
                  pl.BlockSpec(memory_space=pl.ANY)],
        out_specs=[pl.BlockSpec((n_seq, LANES), lambda n: (blk0, n)),
                   pl.BlockSpec((n_seq, LANES), lambda n: (0, n))],
        out_shape=[jax.ShapeDtypeStruct(y_all.shape, y_all.dtype),
                   jax.ShapeDtypeStruct((n_seq, RG_WIDTH), F32)],
        input_output_aliases={13: 0},
        compiler_params=_params(("parallel",)),
        name="rg_sample",
    )(proj, proj, flat_state, flat_state, flat_state, h0, cw, cb, wa, ba, wx, bx, ap, y_all)


def _merge_out_kernel(o_ref, y_ref, ma_ref, mb_ref, x_ref, wa_ref, wb_ref, wo_ref, g_ref,
                      xo_ref, h_ref, *maybe_h32_ref):
    merged = (_sigmoid(ma_ref[...]) * _dot(o_ref[...], wa_ref[...])
              + _sigmoid(mb_ref[...]) * _dot(y_ref[...], wb_ref[...]))
    x_new = x_ref[...] + _dot(merged.astype(BF16), wo_ref[...])
    xo_ref[...] = x_new
    h = _rms(x_new, g_ref[...])
    h_ref[...] = h.astype(h_ref.dtype)
    for r in maybe_h32_ref:
        r[...] = h


def _merge_out(o, y, proj, x, w_a, w_b, w_o, g, with_h32):
    m, d = x.shape
    tm = _row_tile(m, 320)
    moff = (2 * HG_F + 2 * HG_I + 2 * RG_WIDTH) // d
    row = lambda width, col: pl.BlockSpec((tm, width), lambda i: (i, col))
    res = lambda shape: pl.BlockSpec(shape, lambda i: (0, 0), pipeline_mode=pl.Buffered(1))
    n_out = 3 if with_h32 else 2
    return pl.pallas_call(
        _merge_out_kernel,
        grid=(m // tm,),
        in_specs=[row(HG_I, 0), row(RG_WIDTH, 0), row(d, moff), row(d, moff + 1), row(d, 0),
                  res(w_a.shape), res(w_b.shape), res(w_o.shape), res((1, d))],
        out_specs=[row(d, 0)] * n_out,
        out_shape=[jax.ShapeDtypeStruct((m, d), F32), jax.ShapeDtypeStruct((m, d), BF16),
                   jax.ShapeDtypeStruct((m, d), F32)][:n_out],
        compiler_params=_params(("parallel",)),
        name="merge_out",
    )(o, y, proj, proj, x, w_a, w_b, w_o, g.reshape(1, d))


def _ffn_kernel(h_ref, x_ref, wg_ref, wu_ref, wd_ref, g_ref, o_ref, xo_ref):
    f = pl.program_id(1)

    @pl.when(f == 0)
    def _():
        xo_ref[...] = jnp.zeros_like(xo_ref)

    h = h_ref[...]
    gate = _dot(h, wg_ref[...])
    act = (gate * _sigmoid(gate)) * _dot(h, wu_ref[...])
    xo_ref[...] += _dot(act.astype(BF16), wd_ref[...])

    @pl.when(f == pl.num_programs(1) - 1)
    def _():
        x_new = x_ref[...] + xo_ref[...]
        xo_ref[...] = x_new
        o_ref[...] = _rms(x_new, g_ref[...]).astype(o_ref.dtype)


def _ffn(h, x, wg, wu, wd, g, out_dtype, tf):
    m, d = x.shape
    ff = wg.shape[1]
    tm = _row_tile(m, 640)
    return pl.pallas_call(
        _ffn_kernel,
        grid=(m // tm, ff // tf),
        in_specs=[pl.BlockSpec((tm, d), lambda i, f: (i, 0)),
                  pl.BlockSpec((tm, d), lambda i, f: (i, 0)),
                  pl.BlockSpec((d, tf), lambda i, f: (0, f)),
                  pl.BlockSpec((d, tf), lambda i, f: (0, f)),
                  pl.BlockSpec((tf, d), lambda i, f: (f, 0)),
                  pl.BlockSpec((1, d), lambda i, f: (0, 0))],
        out_specs=[pl.BlockSpec((tm, d), lambda i, f: (i, 0)),
                   pl.BlockSpec((tm, d), lambda i, f: (i, 0))],
        out_shape=[jax.ShapeDtypeStruct((m, d), out_dtype), jax.ShapeDtypeStruct((m, d), F32)],
        compiler_params=_params(("parallel", "arbitrary")),
        name="ffn_dense",
    )(h, x, wg, wu, wd, g.reshape(1, d))


def _router_kernel(h_ref, w_ref, info_ref, gate_ref, cnt_ref, carry_ref):
    @pl.when(pl.program_id(0) == 0)
    def _():
        carry_ref[...] = jnp.zeros_like(carry_ref)

    logits = _dot(h_ref[...], w_ref[...])
    tm = logits.shape[0]
    lane = lax.broadcasted_iota(I32, logits.shape, 1)
    neg = jnp.float32(-jnp.inf)
    logits = jnp.where(lane < N_EXPERTS, logits, neg)
    m1 = jnp.max(logits, axis=-1, keepdims=True)
    i1 = jnp.min(jnp.where(logits == m1, lane, LANES), axis=-1, keepdims=True)
    rest = jnp.where(lane == i1, neg, logits)
    m2 = jnp.max(rest, axis=-1, keepdims=True)
    i2 = jnp.min(jnp.where(rest == m2, lane, LANES), axis=-1, keepdims=True)
    e = jnp.exp(m2 - m1)
    g1 = 1.0 / (1.0 + e)
    g2 = e / (1.0 + e)
    gate_ref[...] = jnp.where(lane == 0, g1, jnp.where(lane == 1, g2, 0.0))

    chosen = jnp.where((lane == i1) | (lane == i2), 1.0, 0.0)
    lower = jnp.where(lax.broadcasted_iota(I32, (tm, tm), 0) > lax.broadcasted_iota(I32, (tm, tm), 1),
                      1.0, 0.0).astype(BF16)
    before = _dot(lower, chosen.astype(BF16)) + carry_ref[0:1, :]
    r1 = jnp.sum(jnp.where(lane == i1, before, 0.0), axis=-1, keepdims=True).astype(I32)
    r2 = jnp.sum(jnp.where(lane == i2, before, 0.0), axis=-1, keepdims=True).astype(I32)
    info_ref[...] = jnp.where(lane == 0, i1, jnp.where(lane == 1, i2,
                              jnp.where(lane == 2, r1, jnp.where(lane == 3, r2, 0))))
    total = carry_ref[0:1, :] + jnp.sum(chosen, axis=0, keepdims=True)
    carry_ref[...] = jnp.broadcast_to(total, carry_ref.shape)
    cnt_ref[...] = jnp.broadcast_to(total, cnt_ref.shape)


def _router(h, w_router):
    m, d = h.shape
    tm = _row_tile(m, 640)
    w = jnp.pad(w_router, ((0, 0), (0, LANES - N_EXPERTS))).astype(BF16)
    return pl.pallas_call(
        _router_kernel,
        grid=(m // tm,),
        in_specs=[pl.BlockSpec((tm, d), lambda i: (i, 0)),
                  pl.BlockSpec((d, LANES), lambda i: (0, 0))],
        out_specs=[pl.BlockSpec((tm, LANES), lambda i: (i, 0)),
                   pl.BlockSpec((tm, LANES), lambda i: (i, 0)),
                   pl.BlockSpec((SUBLANES, LANES), lambda i: (0, 0))],
        out_shape=[jax.ShapeDtypeStruct((m, LANES), I32),
                   jax.ShapeDtypeStruct((m, LANES), F32),
                   jax.ShapeDtypeStruct((SUBLANES, LANES), F32)],
        scratch_shapes=[pltpu.VMEM((SUBLANES, LANES), F32)],
        compiler_params=_params(("arbitrary",)),
        name="router",
    )(h, w)


def _row_copy(src_ref, src_row, dst_ref, dst_row, sem):
    return pltpu.make_async_copy(src_ref.at[pl.ds(src_row, 1), :], dst_ref.at[pl.ds(dst_row, 1), :], sem)


def _rows_wait(src_ref, dst_ref, n_rows, sem):
    pltpu.make_async_copy(src_ref.at[pl.ds(0, n_rows), :], dst_ref.at[pl.ds(0, n_rows), :], sem).wait()


def _slots_kernel(pos_ref, src_ref, *, pad_mask):
    tb = pos_ref.shape[-1] // TOP_K
    base = pl.program_id(0) * tb

    @pl.when(pl.program_id(0) == 0)
    def _():
        def clear(p, carry):
            src_ref[p] = p & pad_mask
            return carry
        lax.fori_loop(0, src_ref.shape[0], clear, 0, unroll=8)

    def body(j, carry):
        for k in range(TOP_K):
            src_ref[pos_ref[0, TOP_K * j + k]] = base + j
        return carry

    lax.fori_loop(0, tb, body, 0, unroll=4)


def _slots(pos, n_rows):
    m = pos.shape[0]
    tb = _row_tile(m, 512)
    pad_mask = (1 << (m.bit_length() - 1)) - 1
    return pl.pallas_call(
        functools.partial(_slots_kernel, pad_mask=pad_mask),
        grid=(m // tb,),
        in_specs=[pl.BlockSpec((None, 1, TOP_K * tb), lambda i: (i, 0, 0), memory_space=pltpu.SMEM)],
        out_specs=pl.BlockSpec(memory_space=pltpu.SMEM),
        out_shape=jax.ShapeDtypeStruct((n_rows,), I32),
        compiler_params=_params(("arbitrary",)),
        name="moe_slots",
    )(pos.reshape(m // tb, 1, TOP_K * tb))


def _gather_kernel(ns_ref, src_ref, h_ref, xs_ref, buf_ref, sem):
    n_sub = ns_ref[pl.program_id(0)]
    for s in range(EXPERT_WINDOW_SUBS):
        lo = s * EXPERT_ROWS

        @pl.when(s < n_sub)
        def _():
            def body(r2, carry):
                for p in range(2):
                    r = lo + 2 * r2 + p
                    _row_copy(h_ref, src_ref[0, r], buf_ref, r, sem.at[s]).start(priority=p)
                return carry

            lax.fori_loop(0, EXPERT_ROWS // 2, body, 0, unroll=4)

    for s in range(EXPERT_WINDOW_SUBS):
        lo = s * EXPERT_ROWS
        rows = slice(lo, lo + EXPERT_ROWS)

        @pl.when(s < n_sub)
        def _():
            _rows_wait(h_ref, buf_ref.at[rows, :], EXPERT_ROWS, sem.at[s])
            xs_ref[rows, :] = buf_ref[rows, :].astype(xs_ref.dtype)

        @pl.when(s >= n_sub)
        def _():
            xs_ref[rows, :] = jnp.zeros((EXPERT_ROWS, xs_ref.shape[1]), xs_ref.dtype)


def _gather(h32, src, win_subs):
    m, d = h32.shape
    win = EXPERT_WINDOW_SUBS * EXPERT_ROWS
    assert m >= EXPERT_ROWS
    n_win = src.shape[0] // win
    return pl.pallas_call(
        _gather_kernel,
        grid=(n_win,),
        in_specs=[pl.BlockSpec(memory_space=pltpu.SMEM),
                  pl.BlockSpec((None, 1, win), lambda b: (b, 0, 0), memory_space=pltpu.SMEM),
                  pl.BlockSpec(memory_space=pl.ANY)],
        out_specs=pl.BlockSpec((win, d), lambda b: (b, 0)),
        out_shape=jax.ShapeDtypeStruct((n_win * win, d), BF16),
        scratch_shapes=[pltpu.VMEM((win, d), F32),
                        pltpu.SemaphoreType.DMA((EXPERT_WINDOW_SUBS,))],
        compiler_params=_params(("arbitrary",)),
        name="moe_gather",
    )(win_subs, src.reshape(n_win, 1, win), h32)


def _experts_kernel(we_ref, ns_ref, nv_ref, x_ref, wg_ref, wu_ref, wd_ref, y_ref,
                    wg_bf, wu_bf, wd_bf):
    del we_ref
    b = pl.program_id(0)

    @pl.when(pl.program_id(1) == 0)
    def _():
        y_ref[...] = jnp.zeros_like(y_ref)

    def sub_block(s):
        rows = slice(s * EXPERT_ROWS, (s + 1) * EXPERT_ROWS)
        h = x_ref[rows, :]
        gate = _dot(h, wg_bf[...])
        act = (gate * _sigmoid(gate)) * _dot(h, wu_bf[...])
        y_ref[rows, :] += _dot(act.astype(BF16), wd_bf[...])

    @pl.when(b < nv_ref[0])
    def _():
        wg_bf[...] = wg_ref[...].astype(BF16)
        wu_bf[...] = wu_ref[...].astype(BF16)
        wd_bf[...] = wd_ref[...].astype(BF16)
        sub_block(0)
        for s in range(1, EXPERT_WINDOW_SUBS):
            @pl.when(s < ns_ref[b])
            def _():
                sub_block(s)


def _experts(xs, wg, wu, wd, layer, win_expert, win_subs, n_valid, tf):
    n_rows, d = xs.shape
    ff = wg.shape[-1]
    n_f = ff // tf
    win = EXPERT_WINDOW_SUBS * EXPERT_ROWS
    xmap = lambda b, f, we, ns, nv: (jnp.minimum(b, nv[0] - 1), 0)
    fidx = lambda b, f, nv: jnp.where(b < nv[0], f, n_f - 1)
    once = pl.Buffered(1)
    grid_spec = pltpu.PrefetchScalarGridSpec(
        num_scalar_prefetch=3,
        grid=(n_rows // win, n_f),
        in_specs=[pl.BlockSpec((win, d), xmap),
                  pl.BlockSpec((None, None, d, tf),
                               lambda b, f, we, ns, nv: (layer, we[b], 0, fidx(b, f, nv))),
                  pl.BlockSpec((None, None, d, tf),
                               lambda b, f, we, ns, nv: (layer, we[b], 0, fidx(b, f, nv))),
                  pl.BlockSpec((None, None, tf, d),
                               lambda b, f, we, ns, nv: (layer, we[b], fidx(b, f, nv), 0))],
        out_specs=pl.BlockSpec((win, d), lambda b, f, we, ns, nv: (b, 0), pipeline_mode=once),
        scratch_shapes=[pltpu.VMEM((d, tf), BF16), pltpu.VMEM((d, tf), BF16),
                        pltpu.VMEM((tf, d), BF16)],
    )
    return pl.pallas_call(
        _experts_kernel,
        grid_spec=grid_spec,
        out_shape=jax.ShapeDtypeStruct((n_rows, d), F32),
        compiler_params=_params(("arbitrary", "arbitrary"), vmem=EXPERT_VMEM_LIMIT),
        name="moe_experts",
    )(win_expert, win_subs, n_valid, xs, wg, wu, wd)


def _combine_kernel(pos_ref, x_ref, gate_ref, g_ref, y_ref, o_ref, ya_ref, yb_ref, sem):
    tb = x_ref.shape[0]

    def body(j, carry):
        _row_copy(y_ref, pos_ref[0, TOP_K * j], ya_ref, j, sem).start(priority=0)
        _row_copy(y_ref, pos_ref[0, TOP_K * j + 1], yb_ref, j, sem).start(priority=1)
        return carry

    lax.fori_loop(0, tb, body, 0, unroll=4)
    _rows_wait(y_ref, ya_ref, tb, sem)
    _rows_wait(y_ref, yb_ref, tb, sem)
    gate = gate_ref[...]
    x_new = x_ref[...] + (gate[:, 0:1] * ya_ref[...] + gate[:, 1:2] * yb_ref[...])
    o_ref[...] = _rms(x_new, g_ref[...]).astype(o_ref.dtype)


def _combine(x, y, pos, gates, g, out_dtype, row0, n_rows):
    d = x.shape[1]
    tb = _row_tile(n_rows, 512)
    assert row0 % tb == 0
    blk0 = row0 // tb
    pos = pos.reshape(-1)[TOP_K * row0:TOP_K * (row0 + n_rows)].reshape(n_rows // tb, 1, TOP_K * tb)
    return pl.pallas_call(
        _combine_kernel,
        grid=(n_rows // tb,),
        in_specs=[pl.BlockSpec((None, 1, TOP_K * tb), lambda i: (i, 0, 0), memory_space=pltpu.SMEM),
                  pl.BlockSpec((tb, d), lambda i: (blk0 + i, 0)),
                  pl.BlockSpec((tb, LANES), lambda i: (blk0 + i, 0)),
                  pl.BlockSpec((1, d), lambda i: (0, 0)),
                  pl.BlockSpec(memory_space=pl.ANY)],
        out_specs=pl.BlockSpec((tb, d), lambda i: (i, 0)),
        out_shape=jax.ShapeDtypeStruct((n_rows, d), out_dtype),
        scratch_shapes=[pltpu.VMEM((tb, d), F32), pltpu.VMEM((tb, d), F32),
                        pltpu.SemaphoreType.DMA(())],
        compiler_params=_params(("arbitrary",)),
        name="moe_combine",
    )(pos, x, gates, g.reshape(1, d), y)


def _moe(h, h32, x, w_router, wg, wu, wd, layer, g, out_dtype, m_prompt):
    m, d = x.shape
    sub, wsubs = EXPERT_ROWS, EXPERT_WINDOW_SUBS
    win = sub * wsubs
    info, gates, cnt = _router(h, w_router)

    counts = cnt[0, :N_EXPERTS].astype(I32)
    n_sub = (counts + (sub - 1)) // sub
    n_win = (n_sub + (wsubs - 1)) // wsubs
    per_win = (n_sub + jnp.maximum(n_win, 1) - 1) // jnp.maximum(n_win, 1)
    win_end = jnp.cumsum(n_win)
    win_first = win_end - n_win
    n_valid = win_end[-1]
    max_subs = (TOP_K * m + N_EXPERTS * (sub - 1)) // sub
    max_windows = (max_subs + N_EXPERTS * (wsubs - 1)) // wsubs
    wid = jnp.arange(max_windows, dtype=I32)
    owner = jnp.minimum(jnp.sum((wid[:, None] >= win_end[None, :]).astype(I32), axis=1),
                        N_EXPERTS - 1)
    subs_here = jnp.clip(n_sub[owner] - (wid - win_first[owner]) * per_win[owner], 0, per_win[owner])
    used = wid < n_valid
    win_expert = jnp.where(used, owner, owner[n_valid - 1])
    win_subs = jnp.where(used, subs_here, 0)

    e_tok, r_tok = info[:, :TOP_K], info[:, TOP_K:2 * TOP_K]
    rows_per_win = (per_win * sub)[e_tok]
    max_win_per_expert = -(-(-(-m // sub)) // wsubs)
    win_in_expert = sum((r_tok >= k * rows_per_win).astype(I32) for k in range(1, max_win_per_expert))
    pos = (win_first[e_tok] + win_in_expert) * win + (r_tok - win_in_expert * rows_per_win)

    src = _slots(pos, max_windows * win)
    xs = _gather(h32, src, win_subs)
    y = _experts(xs, wg, wu, wd, layer, win_expert, win_subs, n_valid.reshape(1), 512)
    return (_combine(x, y, pos, gates, g, out_dtype, 0, m_prompt),
            _combine(x, y, pos, gates, g, out_dtype, m_prompt, m - m_prompt))


def _cast_pad_cols_kernel(w_ref, o_ref):
    ff = w_ref.shape[1]
    o_ref[:, :ff] = w_ref[...].astype(o_ref.dtype)
    o_ref[:, ff:] = jnp.zeros((o_ref.shape[0], o_ref.shape[1] - ff), o_ref.dtype)


def _cast_pad_rows_kernel(w_ref, o_ref, *, real_blocks):
    @pl.when(pl.program_id(0) < real_blocks)
    def _():
        o_ref[...] = w_ref[...].astype(o_ref.dtype)

    @pl.when(pl.program_id(0) >= real_blocks)
    def _():
        o_ref[...] = jnp.zeros_like(o_ref)


def _bf16_padded(w, axis, mult):
    rows, cols = w.shape
    if axis == 1:
        assert cols % LANES == 0
        tr = _row_tile(rows, 256)
        padded = cols + (-cols) % mult
        return pl.pallas_call(
            _cast_pad_cols_kernel,
            grid=(rows // tr,),
            in_specs=[pl.BlockSpec((tr, cols), lambda i: (i, 0))],
            out_specs=pl.BlockSpec((tr, padded), lambda i: (i, 0)),
            out_shape=jax.ShapeDtypeStruct((rows, padded), BF16),
            compiler_params=_params(("parallel",)),
            name="cast_pad_cols",
        )(w)
    assert rows % LANES == 0
    padded = rows + (-rows) % mult
    real_blocks = rows // LANES
    return pl.pallas_call(
        functools.partial(_cast_pad_rows_kernel, real_blocks=real_blocks),
        grid=(padded // LANES,),
        in_specs=[pl.BlockSpec((LANES, cols), lambda i: (jnp.minimum(i, real_blocks - 1), 0))],
        out_specs=pl.BlockSpec((LANES, cols), lambda i: (i, 0)),
        out_shape=jax.ShapeDtypeStruct((padded, cols), BF16),
        compiler_params=_params(("parallel",)),
        name="cast_pad_rows",
    )(w)


def kernel(x_prompt, x_sample, state_hgrn, state_rglru, state_conv, hgrn_lb_logits, norm_mix, w_in,
           hgrn_gnorm, rg_conv_w, rg_conv_b, rg_wa, rg_ba, rg_wx, rg_bx, rg_a_param, w_br_a, w_br_b,
           w_out, norm_ffn, ffn_w_gate, ffn_w_up, ffn_w_down, moe_router, moe_w_gate, moe_w_up,
           moe_w_down, norm_final):
    depth = w_in.shape[0]
    batch, seq, d = x_prompt.shape
    n_seq = x_sample.shape[0]
    m_prompt = batch * seq
    m = m_prompt + n_seq

    sm = jax.nn.softmax(hgrn_lb_logits.astype(F32), axis=0)
    lbs = jnp.maximum(jnp.cumsum(sm, axis=0) - sm[0:1], 0.0)
    one_m_lbs = 1.0 - lbs

    x = jnp.concatenate([x_prompt.reshape(m_prompt, d), x_sample.reshape(n_seq, d)], axis=0)
    h = _rmsnorm(x, norm_mix[0], BF16)

    chunk = 256 if seq % 256 == 0 else seq
    rg_rows = 512 if seq % 512 == 0 else chunk
    hg_p, rg_p, cv_p, rg_s, cv_s = [], [], [], [], []
    hg_s = None
    for l in range(depth):
        proj = _in_proj(h, w_in, l)
        vrow = lambda a: a[l].reshape(1, -1)
        o_all, s_prompt = _hgrn_prompt(proj, vrow(lbs), vrow(one_m_lbs), vrow(hgrn_gnorm),
                                       batch, seq, m, chunk)
        o_all, hg_s = _hgrn_sample(proj, vrow(lbs), vrow(one_m_lbs), vrow(hgrn_gnorm),
                                   state_hgrn, l, o_all, m_prompt, hg_s)
        rg_args = (rg_conv_w[l], vrow(rg_conv_b), rg_wa[l], rg_ba[l].reshape(RG_BLOCKS, 1, RG_BW),
                   rg_wx[l], rg_bx[l].reshape(RG_BLOCKS, 1, RG_BW), vrow(rg_a_param))
        y_all, h_prompt = _rg_prompt(proj, *rg_args, batch, seq, m, rg_rows)
        y_all, h_sample = _rg_sample(proj, state_conv[l], state_rglru[l], *rg_args, y_all, m_prompt)

        dense = l % 2 == 0
        merged = _merge_out(o_all, y_all, proj, x, w_br_a[l].astype(BF16), w_br_b[l].astype(BF16),
                            w_out[l].astype(BF16), norm_ffn[l], with_h32=not dense)
        x, h2 = merged[0], merged[1]

        last = l == depth - 1
        g_next = norm_final if last else norm_mix[l + 1]
        out_dtype = F32 if last else BF16
        j = l // 2
        if dense:
            wg = _bf16_padded(ffn_w_gate[j], 1, 512)
            wu = _bf16_padded(ffn_w_up[j], 1, 512)
            wd = _bf16_padded(ffn_w_down[j], 0, 512)
            h, x = _ffn(h2, x, wg, wu, wd, g_next, out_dtype, 512)
        else:
            assert last
            y_prompt, y_sample = _moe(h2, merged[2], x, moe_router[j], moe_w_gate, moe_w_up,
                                      moe_w_down, j, g_next, out_dtype, m_prompt)

        xr_cols = slice(2 * HG_F + 2 * HG_I, 2 * HG_F + 2 * HG_I + RG_WIDTH)
        tail = jnp.stack([proj[(b + 1) * seq - (CONV_W - 1):(b + 1) * seq, xr_cols]
                          for b in range(batch)])
        hg_p.append(s_prompt)
        rg_p.append(h_prompt.reshape(batch, RG_WIDTH))
        cv_p.append(tail)
        rg_s.append(h_sample)
        cv_s.append(jnp.concatenate([state_conv[l][:, 1:], proj[m_prompt:, None, xr_cols]], axis=1))

    y_prompt = y_prompt.reshape(batch, seq, d)
    y_sample = y_sample.reshape(n_seq, 1, d)
    return (y_prompt, y_sample, jnp.stack(hg_p), jnp.stack(rg_p), jnp.stack(cv_p),
            hg_s, jnp.stack(rg_s), jnp.stack(cv_s))
```

```python
import functools

import jax
import jax.numpy as jnp
from jax import lax
from jax.experimental import pallas as pl
from jax.experimental.pallas import tpu as pltpu

D_MODEL = 2048
HG_HEADS = 8
HG_DK = 128
HG_DV = 128
HG_F = HG_HEADS * HG_DK
HG_I = HG_HEADS * HG_DV
RG_WIDTH = 1024
RG_BLOCKS = 8
RG_BW = RG_WIDTH // RG_BLOCKS
CONV_W = 4
RG_C = 8.0
IN_COLS = 2 * HG_F + 2 * HG_I + 2 * RG_WIDTH + 2 * D_MODEL
N_EXPERTS = 8
TOP_K = 2
EPS = 1e-6

LANES = 128
SUBLANES = 8
BF16_SUBLANES = 16
VMEM_LIMIT = 56 * 1024 * 1024
EXPERT_VMEM_LIMIT = 62 * 1024 * 1024
HGRN_HEADS_PER_STEP = 4
HGRN_SAMPLE_ROWS = 16
HGRN_SAMPLE_HEADS = 2
RG_BLOCKS_PER_STEP = 4
EXPERT_ROWS = 512
EXPERT_WINDOW_SUBS = 3

BF16 = jnp.bfloat16
F32 = jnp.float32
I32 = jnp.int32


def _params(sem, vmem=VMEM_LIMIT):
    return pltpu.CompilerParams(dimension_semantics=sem, vmem_limit_bytes=vmem)


def _row_tile(m, target):
    best = None
    for t in range(BF16_SUBLANES, min(m, target) + 1, BF16_SUBLANES):
        if m % t == 0:
            best = t
    assert best is not None, (m, target)
    return best


def _dot(a, b):
    return jnp.dot(a, b, preferred_element_type=F32)


def _dot_nt(a, b):
    return lax.dot_general(a, b, (((1,), (1,)), ((), ())), preferred_element_type=F32)


def _dot_tn(a, b):
    return lax.dot_general(a, b, (((0,), (0,)), ((), ())), preferred_element_type=F32)


def _sigmoid(x):
    return 1.0 / (1.0 + jnp.exp(-x))


def _log_sigmoid(x):
    return jnp.minimum(x, 0.0) - jnp.log1p(jnp.exp(-jnp.abs(x)))


def _rms(x, g):
    ms = jnp.mean(x * x, axis=-1, keepdims=True)
    return x * lax.rsqrt(ms + EPS) * g


def _rmsnorm_kernel(x_ref, g_ref, o_ref):
    o_ref[...] = _rms(x_ref[...], g_ref[...]).astype(o_ref.dtype)


def _rmsnorm(x, g, out_dtype):
    m, d = x.shape
    tm = _row_tile(m, 640)
    return pl.pallas_call(
        _rmsnorm_kernel,
        grid=(m // tm,),
        in_specs=[pl.BlockSpec((tm, d), lambda i: (i, 0)),
                  pl.BlockSpec((1, d), lambda i: (0, 0))],
        out_specs=pl.BlockSpec((tm, d), lambda i: (i, 0)),
        out_shape=jax.ShapeDtypeStruct((m, d), out_dtype),
        compiler_params=_params(("parallel",)),
        name="rmsnorm",
    )(x, g.reshape(1, d))


def _in_proj_kernel(x_ref, w_ref, o_ref, wbf_ref):
    @pl.when(pl.program_id(1) == 0)
    def _():
        wbf_ref[...] = w_ref[...].astype(BF16)

    o_ref[...] = _dot(x_ref[...], wbf_ref[...])


def _in_proj(h, w_in, layer):
    m, d = h.shape
    n = w_in.shape[-1]
    tm = _row_tile(m, 1664)
    tn = 1024
    return pl.pallas_call(
        _in_proj_kernel,
        grid=(n // tn, m // tm),
        in_specs=[pl.BlockSpec((tm, d), lambda j, i: (i, 0)),
                  pl.BlockSpec((None, d, tn), lambda j, i: (layer, 0, j))],
        out_specs=pl.BlockSpec((tm, tn), lambda j, i: (i, j)),
        out_shape=jax.ShapeDtypeStruct((m, n), F32),
        scratch_shapes=[pltpu.VMEM((d, tn), BF16)],
        compiler_params=_params(("arbitrary", "arbitrary")),
        name="in_proj",
    )(h, w_in)


def _hgrn_gates(fz, lb, one_m_lb):
    u = jnp.exp(-jnp.abs(fz))
    r = 1.0 / (1.0 + u)
    nonneg = fz >= 0.0
    sig = jnp.where(nonneg, r, u * r)
    sig_neg = jnp.where(nonneg, u * r, r)
    f = lb + one_m_lb * sig
    return jnp.log(f), one_m_lb * sig_neg, f


def _hgrn_out(o, gate, gnorm):
    return _rms(o, gnorm) * (gate * _sigmoid(gate))


def _hgrn_prompt_kernel(q_ref, f_ref, i_ref, g_ref, lb_ref, omlb_ref, gn_ref, o_init_ref,
                        o_ref, s_out_ref, s_ref, *, chunk):
    del o_init_ref
    C = chunk
    c = pl.program_id(2)

    @pl.when(c == 0)
    def _():
        s_ref[...] = jnp.zeros_like(s_ref)

    row = lax.broadcasted_iota(I32, (C, HG_DK), 0)
    rc_xor = lax.broadcasted_iota(I32, (C, C), 0) ^ lax.broadcasted_iota(I32, (C, C), 1)

    for hh in range(HGRN_HEADS_PER_STEP):
        cols = slice(hh * HG_DK, (hh + 1) * HG_DK)
        q = q_ref[:, cols]
        v = i_ref[:, cols].astype(BF16)
        g, k, _ = _hgrn_gates(f_ref[:, cols], lb_ref[:, cols], omlb_ref[:, cols])

        a_mat = jnp.where(rc_xor == 0, _dot_nt(q.astype(BF16), k.astype(BF16)), 0.0)

        pre, suf, tot = g, jnp.zeros_like(g), g
        w = 1
        while w < C:
            upper = (row & w) != 0
            e = jnp.exp(jnp.where(upper, pre, suf))
            qw = jnp.where(upper, q * e, 0.0).astype(BF16)
            kw = jnp.where(upper, 0.0, k * e).astype(BF16)
            level = _dot_nt(qw, kw)
            a_mat = a_mat + (level if 2 * w == C else jnp.where(rc_xor < 2 * w, level, 0.0))
            up = pltpu.roll(tot, w, 0)
            dn = pltpu.roll(tot, C - w, 0)
            pre = jnp.where(upper, pre + up, pre)
            suf = jnp.where(upper, suf, suf + dn)
            tot = pre + suf
            w *= 2

        s_old = s_ref[hh]
        o = (_dot(a_mat.astype(BF16), v)
             + _dot((q * jnp.exp(pre)).astype(BF16), s_old.astype(BF16)))
        decay_rows = jnp.exp(jnp.broadcast_to(tot[0:1, :], (HG_DK, HG_DK))).T
        s_new = decay_rows * s_old + _dot_tn((k * jnp.exp(suf)).astype(BF16), v)
        s_ref[hh] = s_new

        o_ref[:, cols] = _hgrn_out(o, g_ref[:, cols], gn_ref[:, cols]).astype(o_ref.dtype)

    @pl.when(c == pl.num_programs(2) - 1)
    def _():
        s_out_ref[...] = s_ref[...]


def _hgrn_prompt(proj, lb, one_m_lb, gnorm, batch, seq, m_total, chunk):
    n_chunks = seq // chunk
    hps = HGRN_HEADS_PER_STEP
    width = hps * LANES
    groups = HG_HEADS // hps
    tok = lambda off: pl.BlockSpec((chunk, width), lambda b, h, c: (b * n_chunks + c, off + h))
    vec = pl.BlockSpec((1, width), lambda b, h, c: (0, h))
    return pl.pallas_call(
        functools.partial(_hgrn_prompt_kernel, chunk=chunk),
        grid=(batch, groups, n_chunks),
        in_specs=[tok(0), tok(groups), tok(2 * groups), tok(3 * groups), vec, vec, vec,
                  pl.BlockSpec(memory_space=pl.ANY)],
        input_output_aliases={7: 0},
        out_specs=[pl.BlockSpec((chunk, width), lambda b, h, c: (b * n_chunks + c, h)),
                   pl.BlockSpec((None, hps, HG_DK, HG_DV), lambda b, h, c: (b, h, 0, 0))],
        out_shape=[jax.ShapeDtypeStruct((m_total, HG_I), BF16),
                   jax.ShapeDtypeStruct((batch, HG_HEADS, HG_DK, HG_DV), F32)],
        scratch_shapes=[pltpu.VMEM((hps, HG_DK, HG_DV), F32)],
        compiler_params=_params(("parallel", "parallel", "arbitrary")),
        name="hgrn_prompt",
    )(proj, proj, proj, proj, lb, one_m_lb, gnorm, jnp.zeros((m_total, HG_I), BF16))


def _hgrn_sample_kernel(q_ref, f_ref, i_ref, g_ref, lb_ref, omlb_ref, gn_ref, s_ref, o_prev_ref,
                        *rest, layer, first):
    del o_prev_ref
    o_ref, s_out_ref = rest[-2:]
    n = HGRN_SAMPLE_ROWS
    seq_id = lax.broadcasted_iota(I32, (n, HG_DV), 0)

    def columns(x):
        pad = jnp.zeros((LANES - n, HG_DK), F32)
        return jnp.concatenate([x, pad], axis=0).T

    for hh in range(HGRN_SAMPLE_HEADS):
        cols = slice(hh * HG_DK, (hh + 1) * HG_DK)
        vi = i_ref[:, cols]
        _, k, f = _hgrn_gates(f_ref[:, cols], lb_ref[:, cols], omlb_ref[:, cols])
        f_t, k_t = columns(f), columns(k)
        qb = q_ref[:, cols].astype(BF16)
        o = jnp.zeros((n, HG_DV), F32)
        for j in range(n):
            bc = lambda xt: jnp.broadcast_to(xt[:, j:j + 1], (HG_DK, HG_DV))
            s_new = bc(f_t) * s_ref[j, hh] + bc(k_t) * vi[j:j + 1, :]
            if first:
                for l2 in range(s_out_ref.shape[0]):
                    s_out_ref[l2, j, hh] = s_new if l2 == layer else jnp.zeros_like(s_new)
            else:
                s_out_ref[j, hh] = s_new
            o_j = _dot(qb[j:j + 1, :], s_new.astype(BF16))
            o = jnp.where(seq_id == j, o_j, o)
        o_ref[:, cols] = _hgrn_out(o, g_ref[:, cols], gn_ref[:, cols]).astype(o_ref.dtype)


def _hgrn_sample(proj, lb, one_m_lb, gnorm, states, layer, o_all, row0, new_states):
    depth, n_seq = states.shape[:2]
    n = HGRN_SAMPLE_ROWS
    assert n_seq % n == 0 and row0 % n == 0
    blk0 = row0 // n
    first = new_states is None
    hps = HGRN_SAMPLE_HEADS
    width = hps * LANES
    groups = HG_HEADS // hps
    tok = lambda off: pl.BlockSpec((n, width), lambda j, h: (blk0 + j, off + h))
    vec = pl.BlockSpec((1, width), lambda j, h: (0, h))
    one_layer = pl.BlockSpec((None, n, hps, HG_DK, HG_DV), lambda j, h: (layer, j, h, 0, 0))
    all_layers = pl.BlockSpec((depth, n, hps, HG_DK, HG_DV), lambda j, h: (0, j, h, 0, 0))
    hbm = pl.BlockSpec(memory_space=pl.ANY)
    args = [proj, proj, proj, proj, lb, one_m_lb, gnorm, states, o_all]
    in_specs = [tok(0), tok(groups), tok(2 * groups), tok(3 * groups), vec, vec, vec,
                one_layer, hbm]
    aliases = {8: 0}
    if not first:
        args.append(new_states)
        in_specs.append(hbm)
        aliases[9] = 1
    return pl.pallas_call(
        functools.partial(_hgrn_sample_kernel, layer=layer, first=first),
        grid=(n_seq // n, groups),
        in_specs=in_specs,
        out_specs=[pl.BlockSpec((n, width), lambda j, h: (blk0 + j, h)),
                   all_layers if first else one_layer],
        out_shape=[jax.ShapeDtypeStruct(o_all.shape, o_all.dtype),
                   jax.ShapeDtypeStruct(states.shape, F32)],
        input_output_aliases=aliases,
        compiler_params=_params(("parallel", "parallel")),
        name="hgrn_sample",
    )(*args)


def _gelu_tanh(x):
    sqrt_2_over_pi = 0.7978845608028654
    return 0.5 * x * (1.0 + jnp.tanh(sqrt_2_over_pi * (x + 0.044715 * (x * x * x))))


def _rg_gates(conv, wa, ba, wx, bx, a_param):
    cb = conv.astype(BF16)
    r = _sigmoid(_dot(cb, wa.astype(BF16)) + ba)
    ig = _sigmoid(_dot(cb, wx.astype(BF16)) + bx)
    log_a = RG_C * r * _log_sigmoid(a_param)
    a = jnp.exp(log_a)
    t = jnp.tanh(log_a)
    y = -2.0 * t
    root = y * lax.rsqrt(jnp.maximum(y, 1e-30)) * lax.rsqrt(1.0 - t)
    b = root * ig * conv
    return a, b


def _rg_prompt_kernel(x_ref, gate_ref, cw_ref, cb_ref, wa_ref, ba_ref, wx_ref, bx_ref, ap_ref,
                      y_init_ref, y_ref, h_out_ref, xp_ref, h_ref, *, rows):
    del y_init_ref
    T = rows
    t = pl.program_id(2)

    @pl.when(t == 0)
    def _():
        xp_ref[0:SUBLANES, :] = jnp.zeros((SUBLANES, xp_ref.shape[1]), F32)
        h_ref[...] = jnp.zeros_like(h_ref)

    xp_ref[SUBLANES:SUBLANES + T, :] = x_ref[...]
    row = lax.broadcasted_iota(I32, (T, RG_BW), 0)

    for nb in range(RG_BLOCKS_PER_STEP):
        cols = slice(nb * RG_BW, (nb + 1) * RG_BW)
        xp = xp_ref[:, cols]
        conv = cb_ref[:, cols]
        for j in range(CONV_W):
            lag = CONV_W - 1 - j
            tap = xp if lag == 0 else pltpu.roll(xp, lag, 0)
            conv = conv + tap[SUBLANES:SUBLANES + T, :] * cw_ref[j:j + 1, cols]

        a, b = _rg_gates(conv, wa_ref[nb], ba_ref[nb], wx_ref[nb], bx_ref[nb], ap_ref[:, cols])

        first = row == 0
        b = jnp.where(first, a * h_ref[0:1, cols] + b, b)
        a = jnp.where(first, 0.0, a)
        d = 1
        while d < T:
            b = a * pltpu.roll(b, d, 0) + b
            a = a * pltpu.roll(a, d, 0)
            d *= 2
        h = b
        h_last = h[T - 1:T, :]
        h_ref[:, cols] = jnp.broadcast_to(h_last, (h_ref.shape[0], RG_BW))
        y_ref[:, cols] = (h * _gelu_tanh(gate_ref[:, cols])).astype(y_ref.dtype)

    xp_ref[0:SUBLANES, :] = xp_ref[T:T + SUBLANES, :]

    @pl.when(t == pl.num_programs(2) - 1)
    def _():
        h_out_ref[...] = h_ref[0:1, :]


def _rg_prompt(proj, cw, cb, wa, ba, wx, bx, ap, batch, seq, m_total, rows):
    n_t = seq // rows
    bps = RG_BLOCKS_PER_STEP
    width = bps * RG_BW
    groups = RG_BLOCKS // bps
    xoff = (2 * HG_F + 2 * HG_I) // width
    goff = xoff + groups
    tok = lambda off: pl.BlockSpec((rows, width), lambda b, n, t: (b * n_t + t, off + n))
    vec = pl.BlockSpec((1, width), lambda b, n, t: (0, n))
    blkw = pl.BlockSpec((bps, RG_BW, RG_BW), lambda b, n, t: (n, 0, 0))
    blkb = pl.BlockSpec((bps, 1, RG_BW), lambda b, n, t: (n, 0, 0))
    return pl.pallas_call(
        functools.partial(_rg_prompt_kernel, rows=rows),
        grid=(batch, groups, n_t),
        in_specs=[tok(xoff), tok(goff),
                  pl.BlockSpec((CONV_W, width), lambda b, n, t: (0, n)), vec,
                  blkw, blkb, blkw, blkb, vec, pl.BlockSpec(memory_space=pl.ANY)],
        input_output_aliases={9: 0},
        out_specs=[pl.BlockSpec((rows, width), lambda b, n, t: (b * n_t + t, n)),
                   pl.BlockSpec((None, 1, width), lambda b, n, t: (b, 0, n))],
        out_shape=[jax.ShapeDtypeStruct((m_total, RG_WIDTH), BF16),
                   jax.ShapeDtypeStruct((batch, 1, RG_WIDTH), F32)],
        scratch_shapes=[pltpu.VMEM((rows + SUBLANES, width), F32),
                        pltpu.VMEM((SUBLANES, width), F32)],
        compiler_params=_params(("parallel", "parallel", "arbitrary")),
        name="rg_prompt",
    )(proj, proj, cw, cb, wa, ba, wx, bx, ap, jnp.zeros((m_total, RG_WIDTH), BF16))


def _rg_sample_kernel(x_ref, gate_ref, b0_ref, b1_ref, b2_ref, h0_ref, cw_ref, cb_ref,
                      wa_ref, ba_ref, wx_ref, bx_ref, ap_ref, y_prev_ref, y_ref, h_out_ref):
    del y_prev_ref
    conv = cb_ref[...]
    for j, r in enumerate((b0_ref, b1_ref, b2_ref, x_ref)):
        conv = conv + r[...] * cw_ref[j:j + 1, :]
    a, b = _rg_gates(conv, wa_ref[...], ba_ref[...], wx_ref[...], bx_ref[...], ap_ref[...])
    h = a * h0_ref[...] + b
    h_out_ref[...] = h
    y_ref[...] = (h * _gelu_tanh(gate_ref[...])).astype(y_ref.dtype)


def _rg_sample(proj, conv_state, h0, cw, cb, wa, ba, wx, bx, ap, y_all, row0):
    n_seq = h0.shape[0]
    assert row0 % n_seq == 0 and n_seq % BF16_SUBLANES == 0
    blk0 = row0 // n_seq
    xoff = (2 * HG_F + 2 * HG_I) // LANES
    goff = xoff + RG_BLOCKS
    tok = lambda off: pl.BlockSpec((n_seq, LANES), lambda n: (blk0, off + n))
    buf = lambda j: pl.BlockSpec((n_seq, LANES), lambda n: (0, j * RG_BLOCKS + n))
    vec = pl.BlockSpec((1, LANES), lambda n: (0, n))
    blkw = pl.BlockSpec((None, RG_BW, RG_BW), lambda n: (n, 0, 0))
    blkb = pl.BlockSpec((None, 1, RG_BW), lambda n: (n, 0, 0))
    flat_state = conv_state.reshape(n_seq, (CONV_W - 1) * RG_WIDTH)
    return pl.pallas_call(
        _rg_sample_kernel,
        grid=(RG_BLOCKS,),
        in_specs=[tok(xoff), tok(goff), buf(0), buf(1), buf(2),
                  pl.BlockSpec((n_seq, LANES), lambda n: (0, n)),
                  pl.BlockSpec((CONV_W, LANES), lambda n: (0, n)), vec,
                  blkw, blkb, blkw, blkb, vec,
                  pl.BlockSpec(memory_space=pl.ANY)],
        out_specs=[pl.BlockSpec((n_seq, LANES), lambda n: (blk0, n)),
                   pl.BlockSpec((n_seq, LANES), lambda n: (0, n))],
        out_shape=[jax.ShapeDtypeStruct(y_all.shape, y_all.dtype),
                   jax.ShapeDtypeStruct((n_seq, RG_WIDTH), F32)],
        input_output_aliases={13: 0},
        compiler_params=_params(("parallel",)),
        name="rg_sample",
    )(proj, proj, flat_state, flat_state, flat_state, h0, cw, cb, wa, ba, wx, bx, ap, y_all)


def _merge_out_kernel(o_ref, y_ref, ma_ref, mb_ref, x_ref, wa_ref, wb_ref, wo_ref, g_ref,
                      xo_ref, h_ref, *maybe_h32_ref):
    merged = (_sigmoid(ma_ref[...]) * _dot(o_ref[...], wa_ref[...])
              + _sigmoid(mb_ref[...]) * _dot(y_ref[...], wb_ref[...]))
    x_new = x_ref[...] + _dot(merged.astype(BF16), wo_ref[...])
    xo_ref[...] = x_new
    h = _rms(x_new, g_ref[...])
    h_ref[...] = h.astype(h_ref.dtype)
    for r in maybe_h32_ref:
        r[...] = h


def _merge_out(o, y, proj, x, w_a, w_b, w_o, g, with_h32):
    m, d = x.shape
    tm = _row_tile(m, 320)
    moff = (2 * HG_F + 2 * HG_I + 2 * RG_WIDTH) // d
    row = lambda width, col: pl.BlockSpec((tm, width), lambda i: (i, col))
    res = lambda shape: pl.BlockSpec(shape, lambda i: (0, 0), pipeline_mode=pl.Buffered(1))
    n_out = 3 if with_h32 else 2
    return pl.pallas_call(
        _merge_out_kernel,
        grid=(m // tm,),
        in_specs=[row(HG_I, 0), row(RG_WIDTH, 0), row(d, moff), row(d, moff + 1), row(d, 0),
                  res(w_a.shape), res(w_b.shape), res(w_o.shape), res((1, d))],
        out_specs=[row(d, 0)] * n_out,
        out_shape=[jax.ShapeDtypeStruct((m, d), F32), jax.ShapeDtypeStruct((m, d), BF16),
                   jax.ShapeDtypeStruct((m, d), F32)][:n_out],
        compiler_params=_params(("parallel",)),
        name="merge_out",
    )(o, y, proj, proj, x, w_a, w_b, w_o, g.reshape(1, d))


def _ffn_kernel(h_ref, x_ref, wg_ref, wu_ref, wd_ref, g_ref, o_ref, xo_ref):
    f = pl.program_id(1)

    @pl.when(f == 0)
    def _():
        xo_ref[...] = jnp.zeros_like(xo_ref)

    h = h_ref[...]
    gate = _dot(h, wg_ref[...])
    act = (gate * _sigmoid(gate)) * _dot(h, wu_ref[...])
    xo_ref[...] += _dot(act.astype(BF16), wd_ref[...])

    @pl.when(f == pl.num_programs(1) - 1)
    def _():
        x_new = x_ref[...] + xo_ref[...]
        xo_ref[...] = x_new
        o_ref[...] = _rms(x_new, g_ref[...]).astype(o_ref.dtype)


def _ffn(h, x, wg, wu, wd, g, out_dtype, tf):
    m, d = x.shape
    ff = wg.shape[1]
    tm = _row_tile(m, 640)
    return pl.pallas_call(
        _ffn_kernel,
        grid=(m // tm, ff // tf),
        in_specs=[pl.BlockSpec((tm, d), lambda i, f: (i, 0)),
                  pl.BlockSpec((tm, d), lambda i, f: (i, 0)),
                  pl.BlockSpec((d, tf), lambda i, f: (0, f)),
                  pl.BlockSpec((d, tf), lambda i, f: (0, f)),
                  pl.BlockSpec((tf, d), lambda i, f: (f, 0)),
                  pl.BlockSpec((1, d), lambda i, f: (0, 0))],
        out_specs=[pl.BlockSpec((tm, d), lambda i, f: (i, 0)),
                   pl.BlockSpec((tm, d), lambda i, f: (i, 0))],
        out_shape=[jax.ShapeDtypeStruct((m, d), out_dtype), jax.ShapeDtypeStruct((m, d), F32)],
        compiler_params=_params(("parallel", "arbitrary")),
        name="ffn_dense",
    )(h, x, wg, wu, wd, g.reshape(1, d))


def _router_kernel(h_ref, w_ref, info_ref, gate_ref, cnt_ref, carry_ref):
    @pl.when(pl.program_id(0) == 0)
    def _():
        carry_ref[...] = jnp.zeros_like(carry_ref)

    logits = _dot(h_ref[...], w_ref[...])
    tm = logits.shape[0]
    lane = lax.broadcasted_iota(I32, logits.shape, 1)
    neg = jnp.float32(-jnp.inf)
    logits = jnp.where(lane < N_EXPERTS, logits, neg)
    m1 = jnp.max(logits, axis=-1, keepdims=True)
    i1 = jnp.min(jnp.where(logits == m1, lane, LANES), axis=-1, keepdims=True)
    rest = jnp.where(lane == i1, neg, logits)
    m2 = jnp.max(rest, axis=-1, keepdims=True)
    i2 = jnp.min(jnp.where(rest == m2, lane, LANES), axis=-1, keepdims=True)
    e = jnp.exp(m2 - m1)
    g1 = 1.0 / (1.0 + e)
    g2 = e / (1.0 + e)
    gate_ref[...] = jnp.where(lane == 0, g1, jnp.where(lane == 1, g2, 0.0))

    chosen = jnp.where((lane == i1) | (lane == i2), 1.0, 0.0)
    lower = jnp.where(lax.broadcasted_iota(I32, (tm, tm), 0) > lax.broadcasted_iota(I32, (tm, tm), 1),
                      1.0, 0.0).astype(BF16)
    before = _dot(lower, chosen.astype(BF16)) + carry_ref[0:1, :]
    r1 = jnp.sum(jnp.where(lane == i1, before, 0.0), axis=-1, keepdims=True).astype(I32)
    r2 = jnp.sum(jnp.where(lane == i2, before, 0.0), axis=-1, keepdims=True).astype(I32)
    info_ref[...] = jnp.where(lane == 0, i1, jnp.where(lane == 1, i2,
                              jnp.where(lane == 2, r1, jnp.where(lane == 3, r2, 0))))
    total = carry_ref[0:1, :] + jnp.sum(chosen, axis=0, keepdims=True)
    carry_ref[...] = jnp.broadcast_to(total, carry_ref.shape)
    cnt_ref[...] = jnp.broadcast_to(total, cnt_ref.shape)


def _router(h, w_router):
    m, d = h.shape
    tm = _row_tile(m, 640)
    w = jnp.pad(w_router, ((0, 0), (0, LANES - N_EXPERTS))).astype(BF16)
    return pl.pallas_call(
        _router_kernel,
        grid=(m // tm,),
        in_specs=[pl.BlockSpec((tm, d), lambda i: (i, 0)),
                  pl.BlockSpec((d, LANES), lambda i: (0, 0))],
        out_specs=[pl.BlockSpec((tm, LANES), lambda i: (i, 0)),
                   pl.BlockSpec((tm, LANES), lambda i: (i, 0)),
                   pl.BlockSpec((SUBLANES, LANES), lambda i: (0, 0))],
        out_shape=[jax.ShapeDtypeStruct((m, LANES), I32),
                   jax.ShapeDtypeStruct((m, LANES), F32),
                   jax.ShapeDtypeStruct((SUBLANES, LANES), F32)],
        scratch_shapes=[pltpu.VMEM((SUBLANES, LANES), F32)],
        compiler_params=_params(("arbitrary",)),
        name="router",
    )(h, w)


def _row_copy(src_ref, src_row, dst_ref, dst_row, sem):
    return pltpu.make_async_copy(src_ref.at[pl.ds(src_row, 1), :], dst_ref.at[pl.ds(dst_row, 1), :], sem)


def _rows_wait(src_ref, dst_ref, n_rows, sem):
    pltpu.make_async_copy(src_ref.at[pl.ds(0, n_rows), :], dst_ref.at[pl.ds(0, n_rows), :], sem).wait()


def _slots_kernel(pos_ref, src_ref, *, pad_mask):
    tb = pos_ref.shape[-1] // TOP_K
    base = pl.program_id(0) * tb

    @pl.when(pl.program_id(0) == 0)
    def _():
        def clear(p, carry):
            src_ref[p] = p & pad_mask
            return carry
        lax.fori_loop(0, src_ref.shape[0], clear, 0, unroll=8)

    def body(j, carry):
        for k in range(TOP_K):
            src_ref[pos_ref[0, TOP_K * j + k]] = base + j
        return carry

    lax.fori_loop(0, tb, body, 0, unroll=4)


def _slots(pos, n_rows):
    m = pos.shape[0]
    tb = _row_tile(m, 512)
    pad_mask = (1 << (m.bit_length() - 1)) - 1
    return pl.pallas_call(
        functools.partial(_slots_kernel, pad_mask=pad_mask),
        grid=(m // tb,),
        in_specs=[pl.BlockSpec((None, 1, TOP_K * tb), lambda i: (i, 0, 0), memory_space=pltpu.SMEM)],
        out_specs=pl.BlockSpec(memory_space=pltpu.SMEM),
        out_shape=jax.ShapeDtypeStruct((n_rows,), I32),
        compiler_params=_params(("arbitrary",)),
        name="moe_slots",
    )(pos.reshape(m // tb, 1, TOP_K * tb))


def _gather_kernel(ns_ref, src_ref, src_next_ref, h_ref, xs_ref, buf_ref, sem):
    b = pl.program_id(0)
    slot = b % 2

    def start_window(n_sub, window_src_ref, into):
        for s in range(EXPERT_WINDOW_SUBS):
            lo = s * EXPERT_ROWS

            @pl.when(s < n_sub)
            def _():
                def body(r2, carry):
                    for p in range(2):
                        r = lo + 2 * r2 + p
                        _row_copy(h_ref, window_src_ref[0, r], buf_ref.at[into], r,
                                  sem.at[into, s]).start(priority=p)
                    return carry

                lax.fori_loop(0, EXPERT_ROWS // 2, body, 0, unroll=4)

    @pl.when(b == 0)
    def _():
        start_window(ns_ref[0], src_ref, 0)

    @pl.when(b + 1 < pl.num_programs(0))
    def _():
        start_window(ns_ref[b + 1], src_next_ref, 1 - slot)

    n_sub = ns_ref[b]
    for s in range(EXPERT_WINDOW_SUBS):
        lo = s * EXPERT_ROWS
        rows = slice(lo, lo + EXPERT_ROWS)

        @pl.when(s < n_sub)
        def _():
            _rows_wait(h_ref, buf_ref.at[slot, rows, :], EXPERT_ROWS, sem.at[slot, s])
            xs_ref[rows, :] = buf_ref[slot, rows, :].astype(xs_ref.dtype)

        @pl.when(s >= n_sub)
        def _():
            xs_ref[rows, :] = jnp.zeros((EXPERT_ROWS, xs_ref.shape[1]), xs_ref.dtype)


def _gather(h32, src, win_subs):
    m, d = h32.shape
    win = EXPERT_WINDOW_SUBS * EXPERT_ROWS
    assert m >= EXPERT_ROWS
    n_win = src.shape[0] // win
    src3 = src.reshape(n_win, 1, win)
    return pl.pallas_call(
        _gather_kernel,
        grid=(n_win,),
        in_specs=[pl.BlockSpec(memory_space=pltpu.SMEM),
                  pl.BlockSpec((None, 1, win), lambda b: (b, 0, 0), memory_space=pltpu.SMEM),
                  pl.BlockSpec((None, 1, win), lambda b: (jnp.minimum(b + 1, n_win - 1), 0, 0),
                               memory_space=pltpu.SMEM),
                  pl.BlockSpec(memory_space=pl.ANY)],
        out_specs=pl.BlockSpec((win, d), lambda b: (b, 0)),
        out_shape=jax.ShapeDtypeStruct((n_win * win, d), BF16),
        scratch_shapes=[pltpu.VMEM((2, win, d), F32),
                        pltpu.SemaphoreType.DMA((2, EXPERT_WINDOW_SUBS))],
        compiler_params=_params(("arbitrary",)),
        name="moe_gather",
    )(win_subs, src3, src3, h32)


def _experts_kernel(we_ref, ns_ref, nv_ref, x_ref, wg_ref, wu_ref, wd_ref, y_ref,
                    wg_bf, wu_bf, wd_bf):
    del we_ref
    b = pl.program_id(0)

    @pl.when(pl.program_id(1) == 0)
    def _():
        y_ref[...] = jnp.zeros_like(y_ref)

    def sub_block(s):
        rows = slice(s * EXPERT_ROWS, (s + 1) * EXPERT_ROWS)
        h = x_ref[rows, :]
        gate = _dot(h, wg_bf[...])
        act = (gate * _sigmoid(gate)) * _dot(h, wu_bf[...])
        y_ref[rows, :] += _dot(act.astype(BF16), wd_bf[...])

    @pl.when(b < nv_ref[0])
    def _():
        wg_bf[...] = wg_ref[...].astype(BF16)
        wu_bf[...] = wu_ref[...].astype(BF16)
        wd_bf[...] = wd_ref[...].astype(BF16)
        sub_block(0)
        for s in range(1, EXPERT_WINDOW_SUBS):
            @pl.when(s < ns_ref[b])
            def _():
                sub_block(s)


def _experts(xs, wg, wu, wd, layer, win_expert, win_subs, n_valid, tf):
    n_rows, d = xs.shape
    ff = wg.shape[-1]
    n_f = ff // tf
    win = EXPERT_WINDOW_SUBS * EXPERT_ROWS
    xmap = lambda b, f, we, ns, nv: (jnp.minimum(b, nv[0] - 1), 0)
    fidx = lambda b, f, nv: jnp.where(b < nv[0], f, n_f - 1)
    once = pl.Buffered(1)
    grid_spec = pltpu.PrefetchScalarGridSpec(
        num_scalar_prefetch=3,
        grid=(n_rows // win, n_f),
        in_specs=[pl.BlockSpec((win, d), xmap),
                  pl.BlockSpec((None, None, d, tf),
                               lambda b, f, we, ns, nv: (layer, we[b], 0, fidx(b, f, nv))),
                  pl.BlockSpec((None, None, d, tf),
                               lambda b, f, we, ns, nv: (layer, we[b], 0, fidx(b, f, nv))),
                  pl.BlockSpec((None, None, tf, d),
                               lambda b, f, we, ns, nv: (layer, we[b], fidx(b, f, nv), 0))],
        out_specs=pl.BlockSpec((win, d), lambda b, f, we, ns, nv: (b, 0), pipeline_mode=once),
        scratch_shapes=[pltpu.VMEM((d, tf), BF16), pltpu.VMEM((d, tf), BF16),
                        pltpu.VMEM((tf, d), BF16)],
    )
    return pl.pallas_call(
        _experts_kernel,
        grid_spec=grid_spec,
        out_shape=jax.ShapeDtypeStruct((n_rows, d), F32),
        compiler_params=_params(("arbitrary", "arbitrary"), vmem=EXPERT_VMEM_LIMIT),
        name="moe_experts",
    )(win_expert, win_subs, n_valid, xs, wg, wu, wd)


def _combine_kernel(pos_ref, x_ref, gate_ref, g_ref, y_ref, o_ref, ya_ref, yb_ref, sem):
    tb = x_ref.shape[0]

    def body(j, carry):
        _row_copy(y_ref, pos_ref[0, TOP_K * j], ya_ref, j, sem).start(priority=0)
        _row_copy(y_ref, pos_ref[0, TOP_K * j + 1], yb_ref, j, sem).start(priority=1)
        return carry

    lax.fori_loop(0, tb, body, 0, unroll=4)
    _rows_wait(y_ref, ya_ref, tb, sem)
    _rows_wait(y_ref, yb_ref, tb, sem)
    gate = gate_ref[...]
    x_new = x_ref[...] + (gate[:, 0:1] * ya_ref[...] + gate[:, 1:2] * yb_ref[...])
    o_ref[...] = _rms(x_new, g_ref[...]).astype(o_ref.dtype)


def _combine(x, y, pos, gates, g, out_dtype, row0, n_rows):
    d = x.shape[1]
    tb = _row_tile(n_rows, 512)
    assert row0 % tb == 0
    blk0 = row0 // tb
    pos = pos.reshape(-1)[TOP_K * row0:TOP_K * (row0 + n_rows)].reshape(n_rows // tb, 1, TOP_K * tb)
    return pl.pallas_call(
        _combine_kernel,
        grid=(n_rows // tb,),
        in_specs=[pl.BlockSpec((None, 1, TOP_K * tb), lambda i: (i, 0, 0), memory_space=pltpu.SMEM),
                  pl.BlockSpec((tb, d), lambda i: (blk0 + i, 0)),
                  pl.BlockSpec((tb, LANES), lambda i: (blk0 + i, 0)),
                  pl.BlockSpec((1, d), lambda i: (0, 0)),
                  pl.BlockSpec(memory_space=pl.ANY)],
        out_specs=pl.BlockSpec((tb, d), lambda i: (i, 0)),
        out_shape=jax.ShapeDtypeStruct((n_rows, d), out_dtype),
        scratch_shapes=[pltpu.VMEM((tb, d), F32), pltpu.VMEM((tb, d), F32),
                        pltpu.SemaphoreType.DMA(())],
        compiler_params=_params(("arbitrary",)),
        name="moe_combine",
    )(pos, x, gates, g.reshape(1, d), y)


def _moe(h, h32, x, w_router, wg, wu, wd, layer, g, out_dtype, m_prompt):
    m, d = x.shape
    sub, wsubs = EXPERT_ROWS, EXPERT_WINDOW_SUBS
    win = sub * wsubs
    info, gates, cnt = _router(h, w_router)

    counts = cnt[0, :N_EXPERTS].astype(I32)
    n_sub = (counts + (sub - 1)) // sub
    n_win = (n_sub + (wsubs - 1)) // wsubs
    per_win = (n_sub + jnp.maximum(n_win, 1) - 1) // jnp.maximum(n_win, 1)
    win_end = jnp.cumsum(n_win)
    win_first = win_end - n_win
    n_valid = win_end[-1]
    max_subs = (TOP_K * m + N_EXPERTS * (sub - 1)) // sub
    max_windows = (max_subs + N_EXPERTS * (wsubs - 1)) // wsubs
    wid = jnp.arange(max_windows, dtype=I32)
    owner = jnp.minimum(jnp.sum((wid[:, None] >= win_end[None, :]).astype(I32), axis=1),
                        N_EXPERTS - 1)
    subs_here = jnp.clip(n_sub[owner] - (wid - win_first[owner]) * per_win[owner], 0, per_win[owner])
    used = wid < n_valid
    win_expert = jnp.where(used, owner, owner[n_valid - 1])
    win_subs = jnp.where(used, subs_here, 0)

    e_tok, r_tok = info[:, :TOP_K], info[:, TOP_K:2 * TOP_K]
    rows_per_win = (per_win * sub)[e_tok]
    max_win_per_expert = -(-(-(-m // sub)) // wsubs)
    win_in_expert = sum((r_tok >= k * rows_per_win).astype(I32) for k in range(1, max_win_per_expert))
    pos = (win_first[e_tok] + win_in_expert) * win + (r_tok - win_in_expert * rows_per_win)

    src = _slots(pos, max_windows * win)
    xs = _gather(h32, src, win_subs)
    y = _experts(xs, wg, wu, wd, layer, win_expert, win_subs, n_valid.reshape(1), 512)
    return (_combine(x, y, pos, gates, g, out_dtype, 0, m_prompt),
            _combine(x, y, pos, gates, g, out_dtype, m_prompt, m - m_prompt))


def _cast_pad_cols_kernel(w_ref, o_ref):
    ff = w_ref.shape[1]
    o_ref[:, :ff] = w_ref[...].astype(o_ref.dtype)
    o_ref[:, ff:] = jnp.zeros((o_ref.shape[0], o_ref.shape[1] - ff), o_ref.dtype)


def _cast_pad_rows_kernel(w_ref, o_ref, *, real_blocks):
    @pl.when(pl.program_id(0) < real_blocks)
    def _():
        o_ref[...] = w_ref[...].astype(o_ref.dtype)

    @pl.when(pl.program_id(0) >= real_blocks)
    def _():
        o_ref[...] = jnp.zeros_like(o_ref)


def _bf16_padded(w, axis, mult):
    rows, cols = w.shape
    if axis == 1:
        assert cols % LANES == 0
        tr = _row_tile(rows, 256)
        padded = cols + (-cols) % mult
        return pl.pallas_call(
            _cast_pad_cols_kernel,
            grid=(rows // tr,),
            in_specs=[pl.BlockSpec((tr, cols), lambda i: (i, 0))],
            out_specs=pl.BlockSpec((tr, padded), lambda i: (i, 0)),
            out_shape=jax.ShapeDtypeStruct((rows, padded), BF16),
            compiler_params=_params(("parallel",)),
            name="cast_pad_cols",
        )(w)
    assert rows % LANES == 0
    padded = rows + (-rows) % mult
    real_blocks = rows // LANES
    return pl.pallas_call(
        functools.partial(_cast_pad_rows_kernel, real_blocks=real_blocks),
        grid=(padded // LANES,),
        in_specs=[pl.BlockSpec((LANES, cols), lambda i: (jnp.minimum(i, real_blocks - 1), 0))],
        out_specs=pl.BlockSpec((LANES, cols), lambda i: (i, 0)),
        out_shape=jax.ShapeDtypeStruct((padded, cols), BF16),
        compiler_params=_params(("parallel",)),
        name="cast_pad_rows",
    )(w)


def kernel(x_prompt, x_sample, state_hgrn, state_rglru, state_conv, hgrn_lb_logits, norm_mix, w_in,
           hgrn_gnorm, rg_conv_w, rg_conv_b, rg_wa, rg_ba, rg_wx, rg_bx, rg_a_param, w_br_a, w_br_b,
           w_out, norm_ffn, ffn_w_gate, ffn_w_up, ffn_w_down, moe_router, moe_w_gate, moe_w_up,
           moe_w_down, norm_final):
    depth = w_in.shape[0]
    batch, seq, d = x_prompt.shape
    n_seq = x_sample.shape[0]
    m_prompt = batch * seq
    m = m_prompt + n_seq

    sm = jax.nn.softmax(hgrn_lb_logits.astype(F32), axis=0)
    lbs = jnp.maximum(jnp.cumsum(sm, axis=0) - sm[0:1], 0.0)
    one_m_lbs = 1.0 - lbs

    x = jnp.concatenate([x_prompt.reshape(m_prompt, d), x_sample.reshape(n_seq, d)], axis=0)
    h = _rmsnorm(x, norm_mix[0], BF16)

    chunk = 256 if seq % 256 == 0 else seq
    rg_rows = 512 if seq % 512 == 0 else chunk
    hg_p, rg_p, cv_p, rg_s, cv_s = [], [], [], [], []
    hg_s = None
    for l in range(depth):
        proj = _in_proj(h, w_in, l)
        vrow = lambda a: a[l].reshape(1, -1)
        o_all, s_prompt = _hgrn_prompt(proj, vrow(lbs), vrow(one_m_lbs), vrow(hgrn_gnorm),
                                       batch, seq, m, chunk)
        o_all, hg_s = _hgrn_sample(proj, vrow(lbs), vrow(one_m_lbs), vrow(hgrn_gnorm),
                                   state_hgrn, l, o_all, m_prompt, hg_s)
        rg_args = (rg_conv_w[l], vrow(rg_conv_b), rg_wa[l], rg_ba[l].reshape(RG_BLOCKS, 1, RG_BW),
                   rg_wx[l], rg_bx[l].reshape(RG_BLOCKS, 1, RG_BW), vrow(rg_a_param))
        y_all, h_prompt = _rg_prompt(proj, *rg_args, batch, seq, m, rg_rows)
        y_all, h_sample = _rg_sample(proj, state_conv[l], state_rglru[l], *rg_args, y_all, m_prompt)

        dense = l % 2 == 0
        merged = _merge_out(o_all, y_all, proj, x, w_br_a[l].astype(BF16), w_br_b[l].astype(BF16),
                            w_out[l].astype(BF16), norm_ffn[l], with_h32=not dense)
        x, h2 = merged[0], merged[1]

        last = l == depth - 1
        g_next = norm_final if last else norm_mix[l + 1]
        out_dtype = F32 if last else BF16
        j = l // 2
        if dense:
            wg = _bf16_padded(ffn_w_gate[j], 1, 512)
            wu = _bf16_padded(ffn_w_up[j], 1, 512)
            wd = _bf16_padded(ffn_w_down[j], 0, 512)
            h, x = _ffn(h2, x, wg, wu, wd, g_next, out_dtype, 512)
        else:
            assert last
            y_prompt, y_sample = _moe(h2, merged[2], x, moe_router[j], moe_w_gate, moe_w_up,
                                      moe_w_down, j, g_next, out_dtype, m_prompt)

        xr_cols = slice(2 * HG_F + 2 * HG_I, 2 * HG_F + 2 * HG_I + RG_WIDTH)
        tail = jnp.stack([proj[(b + 1) * seq - (CONV_W - 1):(b + 1) * seq, xr_cols]
                          for b in range(batch)])
        hg_p.append(s_prompt)
        rg_p.append(h_prompt.reshape(batch, RG_WIDTH))
        cv_p.append(tail)
        rg_s.append(h_sample)
        cv_s.append(jnp.concatenate([state_conv[l][:, 1:], proj[m_prompt:, None, xr_cols]], axis=1))

    y_prompt = y_prompt.reshape(batch, seq, d)
    y_sample = y_sample.reshape(n_seq, 1, d)
    return (y_prompt, y_sample, jnp.stack(hg_p), jnp.stack(rg_p), jnp.stack(cv_p),
            hg_s, jnp.stack(rg_s), jnp.stack(cv_s))
```

```python
import functools

import jax
import jax.numpy as jnp
from jax import lax
from jax.experimental import pallas as pl
from jax.experimental.pallas import tpu as pltpu

D_MODEL = 2048
HG_HEADS = 8
HG_DK = 128
HG_DV = 128
HG_F = HG_HEADS * HG_DK
HG_I = HG_HEADS * HG_DV
RG_WIDTH = 1024
RG_BLOCKS = 8
RG_BW = RG_WIDTH // RG_BLOCKS
CONV_W = 4
RG_C = 8.0
IN_COLS = 2 * HG_F + 2 * HG_I + 2 * RG_WIDTH + 2 * D_MODEL
N_EXPERTS = 8
TOP_K = 2
EPS = 1e-6

LANES = 128
SUBLANES = 8
BF16_SUBLANES = 16
VMEM_LIMIT = 56 * 1024 * 1024
EXPERT_VMEM_LIMIT = 62 * 1024 * 1024
HGRN_HEADS_PER_STEP = 4
HGRN_SAMPLE_ROWS = 16
HGRN_SAMPLE_HEADS = 2
RG_BLOCKS_PER_STEP = 4
EXPERT_ROWS = 512
EXPERT_WINDOW_SUBS = 2

BF16 = jnp.bfloat16
F32 = jnp.float32
I32 = jnp.int32


def _params(sem, vmem=VMEM_LIMIT):
    return pltpu.CompilerParams(dimension_semantics=sem, vmem_limit_bytes=vmem)


def _row_tile(m, target):
    best = None
    for t in range(BF16_SUBLANES, min(m, target) + 1, BF16_SUBLANES):
        if m % t == 0:
            best = t
    assert best is not None, (m, target)
    return best


def _dot(a, b):
    return jnp.dot(a, b, preferred_element_type=F32)


def _dot_nt(a, b):
    return lax.dot_general(a, b, (((1,), (1,)), ((), ())), preferred_element_type=F32)


def _dot_tn(a, b):
    return lax.dot_general(a, b, (((0,), (0,)), ((), ())), preferred_element_type=F32)


def _sigmoid(x):
    return 1.0 / (1.0 + jnp.exp(-x))


def _log_sigmoid(x):
    return jnp.minimum(x, 0.0) - jnp.log1p(jnp.exp(-jnp.abs(x)))


def _rms(x, g):
    ms = jnp.mean(x * x, axis=-1, keepdims=True)
    return x * lax.rsqrt(ms + EPS) * g


def _rmsnorm_kernel(x_ref, g_ref, o_ref):
    o_ref[...] = _rms(x_ref[...], g_ref[...]).astype(o_ref.dtype)


def _rmsnorm(x, g, out_dtype):
    m, d = x.shape
    tm = _row_tile(m, 640)
    return pl.pallas_call(
        _rmsnorm_kernel,
        grid=(m // tm,),
        in_specs=[pl.BlockSpec((tm, d), lambda i: (i, 0)),
                  pl.BlockSpec((1, d), lambda i: (0, 0))],
        out_specs=pl.BlockSpec((tm, d), lambda i: (i, 0)),
        out_shape=jax.ShapeDtypeStruct((m, d), out_dtype),
        compiler_params=_params(("parallel",)),
        name="rmsnorm",
    )(x, g.reshape(1, d))


def _in_proj_kernel(x_ref, w_ref, o_ref, wbf_ref):
    @pl.when(pl.program_id(1) == 0)
    def _():
        wbf_ref[...] = w_ref[...].astype(BF16)

    o_ref[...] = _dot(x_ref[...], wbf_ref[...])


def _in_proj(h, w_in, layer):
    m, d = h.shape
    n = w_in.shape[-1]
    tm = _row_tile(m, 1664)
    tn = 1024
    return pl.pallas_call(
        _in_proj_kernel,
        grid=(n // tn, m // tm),
        in_specs=[pl.BlockSpec((tm, d), lambda j, i: (i, 0)),
                  pl.BlockSpec((None, d, tn), lambda j, i: (layer, 0, j))],
        out_specs=pl.BlockSpec((tm, tn), lambda j, i: (i, j)),
        out_shape=jax.ShapeDtypeStruct((m, n), F32),
        scratch_shapes=[pltpu.VMEM((d, tn), BF16)],
        compiler_params=_params(("arbitrary", "arbitrary")),
        name="in_proj",
    )(h, w_in)


def _hgrn_gates(fz, lb, one_m_lb):
    u = jnp.exp(-jnp.abs(fz))
    r = 1.0 / (1.0 + u)
    nonneg = fz >= 0.0
    sig = jnp.where(nonneg, r, u * r)
    sig_neg = jnp.where(nonneg, u * r, r)
    f = lb + one_m_lb * sig
    return jnp.log(f), one_m_lb * sig_neg, f


def _hgrn_out(o, gate, gnorm):
    return _rms(o, gnorm) * (gate * _sigmoid(gate))


def _hgrn_prompt_kernel(q_ref, f_ref, i_ref, g_ref, lb_ref, omlb_ref, gn_ref, o_init_ref,
                        o_ref, s_out_ref, s_ref, *, chunk):
    del o_init_ref
    C = chunk
    c = pl.program_id(2)

    @pl.when(c == 0)
    def _():
        s_ref[...] = jnp.zeros_like(s_ref)

    row = lax.broadcasted_iota(I32, (C, HG_DK), 0)
    rc_xor = lax.broadcasted_iota(I32, (C, C), 0) ^ lax.broadcasted_iota(I32, (C, C), 1)

    for hh in range(HGRN_HEADS_PER_STEP):
        cols = slice(hh * HG_DK, (hh + 1) * HG_DK)
        q = q_ref[:, cols]
        v = i_ref[:, cols].astype(BF16)
        g, k, _ = _hgrn_gates(f_ref[:, cols], lb_ref[:, cols], omlb_ref[:, cols])

        a_mat = jnp.where(rc_xor == 0, _dot_nt(q.astype(BF16), k.astype(BF16)), 0.0)

        pre, suf, tot = g, jnp.zeros_like(g), g
        w = 1
        while w < C:
            upper = (row & w) != 0
            e = jnp.exp(jnp.where(upper, pre, suf))
            qw = jnp.where(upper, q * e, 0.0).astype(BF16)
            kw = jnp.where(upper, 0.0, k * e).astype(BF16)
            level = _dot_nt(qw, kw)
            a_mat = a_mat + (level if 2 * w == C else jnp.where(rc_xor < 2 * w, level, 0.0))
            up = pltpu.roll(tot, w, 0)
            dn = pltpu.roll(tot, C - w, 0)
            pre = jnp.where(upper, pre + up, pre)
            suf = jnp.where(upper, suf, suf + dn)
            tot = pre + suf
            w *= 2

        s_old = s_ref[hh]
        o = (_dot(a_mat.astype(BF16), v)
             + _dot((q * jnp.exp(pre)).astype(BF16), s_old.astype(BF16)))
        decay_rows = jnp.exp(jnp.broadcast_to(tot[0:1, :], (HG_DK, HG_DK))).T
        s_new = decay_rows * s_old + _dot_tn((k * jnp.exp(suf)).astype(BF16), v)
        s_ref[hh] = s_new

        o_ref[:, cols] = _hgrn_out(o, g_ref[:, cols], gn_ref[:, cols]).astype(o_ref.dtype)

    @pl.when(c == pl.num_programs(2) - 1)
    def _():
        s_out_ref[...] = s_ref[...]


def _hgrn_prompt(proj, lb, one_m_lb, gnorm, batch, seq, m_total, chunk):
    n_chunks = seq // chunk
    hps = HGRN_HEADS_PER_STEP
    width = hps * LANES
    groups = HG_HEADS // hps
    tok = lambda off: pl.BlockSpec((chunk, width), lambda b, h, c: (b * n_chunks + c, off + h))
    vec = pl.BlockSpec((1, width), lambda b, h, c: (0, h))
    return pl.pallas_call(
        functools.partial(_hgrn_prompt_kernel, chunk=chunk),
        grid=(batch, groups, n_chunks),
        in_specs=[tok(0), tok(groups), tok(2 * groups), tok(3 * groups), vec, vec, vec,
                  pl.BlockSpec(memory_space=pl.ANY)],
        input_output_aliases={7: 0},
        out_specs=[pl.BlockSpec((chunk, width), lambda b, h, c: (b * n_chunks + c, h)),
                   pl.BlockSpec((None, hps, HG_DK, HG_DV), lambda b, h, c: (b, h, 0, 0))],
        out_shape=[jax.ShapeDtypeStruct((m_total, HG_I), BF16),
                   jax.ShapeDtypeStruct((batch, HG_HEADS, HG_DK, HG_DV), F32)],
        scratch_shapes=[pltpu.VMEM((hps, HG_DK, HG_DV), F32)],
        compiler_params=_params(("parallel", "parallel", "arbitrary")),
        name="hgrn_prompt",
    )(proj, proj, proj, proj, lb, one_m_lb, gnorm, jnp.zeros((m_total, HG_I), BF16))


def _hgrn_sample_kernel(q_ref, f_ref, i_ref, g_ref, lb_ref, omlb_ref, gn_ref, s_ref, o_prev_ref,
                        *rest, layer, first):
    del o_prev_ref
    o_ref, s_out_ref = rest[-2:]
    n = HGRN_SAMPLE_ROWS
    seq_id = lax.broadcasted_iota(I32, (n, HG_DV), 0)

    def columns(x):
        pad = jnp.zeros((LANES - n, HG_DK), F32)
        return jnp.concatenate([x, pad], axis=0).T

    for hh in range(HGRN_SAMPLE_HEADS):
        cols = slice(hh * HG_DK, (hh + 1) * HG_DK)
        vi = i_ref[:, cols]
        _, k, f = _hgrn_gates(f_ref[:, cols], lb_ref[:, cols], omlb_ref[:, cols])
        f_t, k_t = columns(f), columns(k)
        qb = q_ref[:, cols].astype(BF16)
        o = jnp.zeros((n, HG_DV), F32)
        for j in range(n):
            bc = lambda xt: jnp.broadcast_to(xt[:, j:j + 1], (HG_DK, HG_DV))
            s_new = bc(f_t) * s_ref[j, hh] + bc(k_t) * vi[j:j + 1, :]
            if first:
                for l2 in range(s_out_ref.shape[0]):
                    s_out_ref[l2, j, hh] = s_new if l2 == layer else jnp.zeros_like(s_new)
            else:
                s_out_ref[j, hh] = s_new
            o_j = _dot(qb[j:j + 1, :], s_new.astype(BF16))
            o = jnp.where(seq_id == j, o_j, o)
        o_ref[:, cols] = _hgrn_out(o, g_ref[:, cols], gn_ref[:, cols]).astype(o_ref.dtype)


def _hgrn_sample(proj, lb, one_m_lb, gnorm, states, layer, o_all, row0, new_states):
    depth, n_seq = states.shape[:2]
    n = HGRN_SAMPLE_ROWS
    assert n_seq % n == 0 and row0 % n == 0
    blk0 = row0 // n
    first = new_states is None
    hps = HGRN_SAMPLE_HEADS
    width = hps * LANES
    groups = HG_HEADS // hps
    tok = lambda off: pl.BlockSpec((n, width), lambda j, h: (blk0 + j, off + h))
    vec = pl.BlockSpec((1, width), lambda j, h: (0, h))
    one_layer = pl.BlockSpec((None, n, hps, HG_DK, HG_DV), lambda j, h: (layer, j, h, 0, 0))
    all_layers = pl.BlockSpec((depth, n, hps, HG_DK, HG_DV), lambda j, h: (0, j, h, 0, 0))
    hbm = pl.BlockSpec(memory_space=pl.ANY)
    args = [proj, proj, proj, proj, lb, one_m_lb, gnorm, states, o_all]
    in_specs = [tok(0), tok(groups), tok(2 * groups), tok(3 * groups), vec, vec, vec,
                one_layer, hbm]
    aliases = {8: 0}
    if not first:
        args.append(new_states)
        in_specs.append(hbm)
        aliases[9] = 1
    return pl.pallas_call(
        functools.partial(_hgrn_sample_kernel, layer=layer, first=first),
        grid=(n_seq // n, groups),
        in_specs=in_specs,
        out_specs=[pl.BlockSpec((n, width), lambda j, h: (blk0 + j, h)),
                   all_layers if first else one_layer],
        out_shape=[jax.ShapeDtypeStruct(o_all.shape, o_all.dtype),
                   jax.ShapeDtypeStruct(states.shape, F32)],
        input_output_aliases=aliases,
        compiler_params=_params(("parallel", "parallel")),
        name="hgrn_sample",
    )(*args)


def _gelu_tanh(x):
    sqrt_2_over_pi = 0.7978845608028654
    return 0.5 * x * (1.0 + jnp.tanh(sqrt_2_over_pi * (x + 0.044715 * (x * x * x))))


def _rg_gates(conv, wa, ba, wx, bx, a_param):
    cb = conv.astype(BF16)
    r = _sigmoid(_dot(cb, wa.astype(BF16)) + ba)
    ig = _sigmoid(_dot(cb, wx.astype(BF16)) + bx)
    log_a = RG_C * r * _log_sigmoid(a_param)
    a = jnp.exp(log_a)
    t = jnp.tanh(log_a)
    y = -2.0 * t
    root = y * lax.rsqrt(jnp.maximum(y, 1e-30)) * lax.rsqrt(1.0 - t)
    b = root * ig * conv
    return a, b


def _rg_prompt_kernel(x_ref, gate_ref, cw_ref, cb_ref, wa_ref, ba_ref, wx_ref, bx_ref, ap_ref,
                      y_init_ref, y_ref, h_out_ref, xp_ref, h_ref, *, rows):
    del y_init_ref
    T = rows
    t = pl.program_id(2)

    @pl.when(t == 0)
    def _():
        xp_ref[0:SUBLANES, :] = jnp.zeros((SUBLANES, xp_ref.shape[1]), F32)
        h_ref[...] = jnp.zeros_like(h_ref)

    xp_ref[SUBLANES:SUBLANES + T, :] = x_ref[...]
    row = lax.broadcasted_iota(I32, (T, RG_BW), 0)

    for nb in range(RG_BLOCKS_PER_STEP):
        cols = slice(nb * RG_BW, (nb + 1) * RG_BW)
        xp = xp_ref[:, cols]
        conv = cb_ref[:, cols]
        for j in range(CONV_W):
            lag = CONV_W - 1 - j
            tap = xp if lag == 0 else pltpu.roll(xp, lag, 0)
            conv = conv + tap[SUBLANES:SUBLANES + T, :] * cw_ref[j:j + 1, cols]

        a, b = _rg_gates(conv, wa_ref[nb], ba_ref[nb], wx_ref[nb], bx_ref[nb], ap_ref[:, cols])

        first = row == 0
        b = jnp.where(first, a * h_ref[0:1, cols] + b, b)
        a = jnp.where(first, 0.0, a)
        d = 1
        while d < T:
            b = a * pltpu.roll(b, d, 0) + b
            a = a * pltpu.roll(a, d, 0)
            d *= 2
        h = b
        h_last = h[T - 1:T, :]
        h_ref[:, cols] = jnp.broadcast_to(h_last, (h_ref.shape[0], RG_BW))
        y_ref[:, cols] = (h * _gelu_tanh(gate_ref[:, cols])).astype(y_ref.dtype)

    xp_ref[0:SUBLANES, :] = xp_ref[T:T + SUBLANES, :]

    @pl.when(t == pl.num_programs(2) - 1)
    def _():
        h_out_ref[...] = h_ref[0:1, :]


def _rg_prompt(proj, cw, cb, wa, ba, wx, bx, ap, batch, seq, m_total, rows):
    n_t = seq // rows
    bps = RG_BLOCKS_PER_STEP
    width = bps * RG_BW
    groups = RG_BLOCKS // bps
    xoff = (2 * HG_F + 2 * HG_I) // width
    goff = xoff + groups
    tok = lambda off: pl.BlockSpec((rows, width), lambda b, n, t: (b * n_t + t, off + n))
    vec = pl.BlockSpec((1, width), lambda b, n, t: (0, n))
    blkw = pl.BlockSpec((bps, RG_BW, RG_BW), lambda b, n, t: (n, 0, 0))
    blkb = pl.BlockSpec((bps, 1, RG_BW), lambda b, n, t: (n, 0, 0))
    return pl.pallas_call(
        functools.partial(_rg_prompt_kernel, rows=rows),
        grid=(batch, groups, n_t),
        in_specs=[tok(xoff), tok(goff),
                  pl.BlockSpec((CONV_W, width), lambda b, n, t: (0, n)), vec,
                  blkw, blkb, blkw, blkb, vec, pl.BlockSpec(memory_space=pl.ANY)],
        input_output_aliases={9: 0},
        out_specs=[pl.BlockSpec((rows, width), lambda b, n, t: (b * n_t + t, n)),
                   pl.BlockSpec((None, 1, width), lambda b, n, t: (b, 0, n))],
        out_shape=[jax.ShapeDtypeStruct((m_total, RG_WIDTH), BF16),
                   jax.ShapeDtypeStruct((batch, 1, RG_WIDTH), F32)],
        scratch_shapes=[pltpu.VMEM((rows + SUBLANES, width), F32),
                        pltpu.VMEM((SUBLANES, width), F32)],
        compiler_params=_params(("parallel", "parallel", "arbitrary")),
        name="rg_prompt",
    )(proj, proj, cw, cb, wa, ba, wx, bx, ap, jnp.zeros((m_total, RG_WIDTH), BF16))


def _rg_sample_kernel(x_ref, gate_ref, b0_ref, b1_ref, b2_ref, h0_ref, cw_ref, cb_ref,
                      wa_ref, ba_ref, wx_ref, bx_ref, ap_ref, y_prev_ref, y_ref, h_out_ref):
    del y_prev_ref
    conv = cb_ref[...]
    for j, r in enumerate((b0_ref, b1_ref, b2_ref, x_ref)):
        conv = conv + r[...] * cw_ref[j:j + 1, :]
    a, b = _rg_gates(conv, wa_ref[...], ba_ref[...], wx_ref[...], bx_ref[...], ap_ref[...])
    h = a * h0_ref[...] + b
    h_out_ref[...] = h
    y_ref[...] = (h * _gelu_tanh(gate_ref[...])).astype(y_ref.dtype)


def _rg_sample(proj, conv_state, h0, cw, cb, wa, ba, wx, bx, ap, y_all, row0):
    n_seq = h0.shape[0]
    assert row0 % n_seq == 0 and n_seq % BF16_SUBLANES == 0
    blk0 = row0 // n_seq
    xoff = (2 * HG_F + 2 * HG_I) // LANES
    goff = xoff + RG_BLOCKS
    tok = lambda off: pl.BlockSpec((n_seq, LANES), lambda n: (blk0, off + n))
    buf = lambda j: pl.BlockSpec((n_seq, LANES), lambda n: (0, j * RG_BLOCKS + n))
    vec = pl.BlockSpec((1, LANES), lambda n: (0, n))
    blkw = pl.BlockSpec((None, RG_BW, RG_BW), lambda n: (n, 0, 0))
    blkb = pl.BlockSpec((None, 1, RG_BW), lambda n: (n, 0, 0))
    flat_state = conv_state.reshape(n_seq, (CONV_W - 1) * RG_WIDTH)
    return pl.pallas_call(
        _rg_sample_kernel,
        grid=(RG_BLOCKS,),
        in_specs=[tok(xoff), tok(goff), buf(0), buf(1), buf(2),
                  pl.BlockSpec((n_seq, LANES), lambda n: (0, n)),
                  pl.BlockSpec((CONV_W, LANES), lambda n: (0, n)), vec,
                  blkw, blkb, blkw, blkb, vec,
                  pl.BlockSpec(memory_space=pl.ANY)],
        out_specs=[pl.BlockSpec((n_seq, LANES), lambda n: (blk0, n)),
                   pl.BlockSpec((n_seq, LANES), lambda n: (0, n))],
        out_shape=[jax.ShapeDtypeStruct(y_all.shape, y_all.dtype),
                   jax.ShapeDtypeStruct((n_seq, RG_WIDTH), F32)],
        input_output_aliases={13: 0},
        compiler_params=_params(("parallel",)),
        name="rg_sample",
    )(proj, proj, flat_state, flat_state, flat_state, h0, cw, cb, wa, ba, wx, bx, ap, y_all)


def _merge_out_kernel(o_ref, y_ref, ma_ref, mb_ref, x_ref, wa_ref, wb_ref, wo_ref, g_ref,
                      xo_ref, h_ref, *maybe_h32_ref):
    merged = (_sigmoid(ma_ref[...]) * _dot(o_ref[...], wa_ref[...])
              + _sigmoid(mb_ref[...]) * _dot(y_ref[...], wb_ref[...]))
    x_new = x_ref[...] + _dot(merged.astype(BF16), wo_ref[...])
    xo_ref[...] = x_new
    h = _rms(x_new, g_ref[...])
    h_ref[...] = h.astype(h_ref.dtype)
    for r in maybe_h32_ref:
        r[...] = h


def _merge_out(o, y, proj, x, w_a, w_b, w_o, g, with_h32):
    m, d = x.shape
    tm = _row_tile(m, 320)
    moff = (2 * HG_F + 2 * HG_I + 2 * RG_WIDTH) // d
    row = lambda width, col: pl.BlockSpec((tm, width), lambda i: (i, col))
    res = lambda shape: pl.BlockSpec(shape, lambda i: (0, 0), pipeline_mode=pl.Buffered(1))
    n_out = 3 if with_h32 else 2
    return pl.pallas_call(
        _merge_out_kernel,
        grid=(m // tm,),
        in_specs=[row(HG_I, 0), row(RG_WIDTH, 0), row(d, moff), row(d, moff + 1), row(d, 0),
                  res(w_a.shape), res(w_b.shape), res(w_o.shape), res((1, d))],
        out_specs=[row(d, 0)] * n_out,
        out_shape=[jax.ShapeDtypeStruct((m, d), F32), jax.ShapeDtypeStruct((m, d), BF16),
                   jax.ShapeDtypeStruct((m, d), F32)][:n_out],
        compiler_params=_params(("parallel",)),
        name="merge_out",
    )(o, y, proj, proj, x, w_a, w_b, w_o, g.reshape(1, d))


def _ffn_kernel(h_ref, x_ref, wg_ref, wu_ref, wd_ref, g_ref, o_ref, xo_ref):
    f = pl.program_id(1)

    @pl.when(f == 0)
    def _():
        xo_ref[...] = jnp.zeros_like(xo_ref)

    h = h_ref[...]
    gate = _dot(h, wg_ref[...])
    act = (gate * _sigmoid(gate)) * _dot(h, wu_ref[...])
    xo_ref[...] += _dot(act.astype(BF16), wd_ref[...])

    @pl.when(f == pl.num_programs(1) - 1)
    def _():
        x_new = x_ref[...] + xo_ref[...]
        xo_ref[...] = x_new
        o_ref[...] = _rms(x_new, g_ref[...]).astype(o_ref.dtype)


def _ffn(h, x, wg, wu, wd, g, out_dtype, tf):
    m, d = x.shape
    ff = wg.shape[1]
    tm = _row_tile(m, 640)
    return pl.pallas_call(
        _ffn_kernel,
        grid=(m // tm, ff // tf),
        in_specs=[pl.BlockSpec((tm, d), lambda i, f: (i, 0)),
                  pl.BlockSpec((tm, d), lambda i, f: (i, 0)),
                  pl.BlockSpec((d, tf), lambda i, f: (0, f)),
                  pl.BlockSpec((d, tf), lambda i, f: (0, f)),
                  pl.BlockSpec((tf, d), lambda i, f: (f, 0)),
                  pl.BlockSpec((1, d), lambda i, f: (0, 0))],
        out_specs=[pl.BlockSpec((tm, d), lambda i, f: (i, 0)),
                   pl.BlockSpec((tm, d), lambda i, f: (i, 0))],
        out_shape=[jax.ShapeDtypeStruct((m, d), out_dtype), jax.ShapeDtypeStruct((m, d), F32)],
        compiler_params=_params(("parallel", "arbitrary")),
        name="ffn_dense",
    )(h, x, wg, wu, wd, g.reshape(1, d))


def _router_kernel(h_ref, w_ref, info_ref, gate_ref, cnt_ref, carry_ref):
    @pl.when(pl.program_id(0) == 0)
    def _():
        carry_ref[...] = jnp.zeros_like(carry_ref)

    logits = _dot(h_ref[...], w_ref[...])
    tm = logits.shape[0]
    lane = lax.broadcasted_iota(I32, logits.shape, 1)
    neg = jnp.float32(-jnp.inf)
    logits = jnp.where(lane < N_EXPERTS, logits, neg)
    m1 = jnp.max(logits, axis=-1, keepdims=True)
    i1 = jnp.min(jnp.where(logits == m1, lane, LANES), axis=-1, keepdims=True)
    rest = jnp.where(lane == i1, neg, logits)
    m2 = jnp.max(rest, axis=-1, keepdims=True)
    i2 = jnp.min(jnp.where(rest == m2, lane, LANES), axis=-1, keepdims=True)
    e = jnp.exp(m2 - m1)
    g1 = 1.0 / (1.0 + e)
    g2 = e / (1.0 + e)
    gate_ref[...] = jnp.where(lane == 0, g1, jnp.where(lane == 1, g2, 0.0))

    chosen = jnp.where((lane == i1) | (lane == i2), 1.0, 0.0)
    lower = jnp.where(lax.broadcasted_iota(I32, (tm, tm), 0) > lax.broadcasted_iota(I32, (tm, tm), 1),
                      1.0, 0.0).astype(BF16)
    before = _dot(lower, chosen.astype(BF16)) + carry_ref[0:1, :]
    r1 = jnp.sum(jnp.where(lane == i1, before, 0.0), axis=-1, keepdims=True).astype(I32)
    r2 = jnp.sum(jnp.where(lane == i2, before, 0.0), axis=-1, keepdims=True).astype(I32)
    info_ref[...] = jnp.where(lane == 0, i1, jnp.where(lane == 1, i2,
                              jnp.where(lane == 2, r1, jnp.where(lane == 3, r2, 0))))
    total = carry_ref[0:1, :] + jnp.sum(chosen, axis=0, keepdims=True)
    carry_ref[...] = jnp.broadcast_to(total, carry_ref.shape)
    cnt_ref[...] = jnp.broadcast_to(total, cnt_ref.shape)


def _router(h, w_router):
    m, d = h.shape
    tm = _row_tile(m, 640)
    w = jnp.pad(w_router, ((0, 0), (0, LANES - N_EXPERTS))).astype(BF16)
    return pl.pallas_call(
        _router_kernel,
        grid=(m // tm,),
        in_specs=[pl.BlockSpec((tm, d), lambda i: (i, 0)),
                  pl.BlockSpec((d, LANES), lambda i: (0, 0))],
        out_specs=[pl.BlockSpec((tm, LANES), lambda i: (i, 0)),
                   pl.BlockSpec((tm, LANES), lambda i: (i, 0)),
                   pl.BlockSpec((SUBLANES, LANES), lambda i: (0, 0))],
        out_shape=[jax.ShapeDtypeStruct((m, LANES), I32),
                   jax.ShapeDtypeStruct((m, LANES), F32),
                   jax.ShapeDtypeStruct((SUBLANES, LANES), F32)],
        scratch_shapes=[pltpu.VMEM((SUBLANES, LANES), F32)],
        compiler_params=_params(("arbitrary",)),
        name="router",
    )(h, w)


def _row_copy(src_ref, src_row, dst_ref, dst_row, sem):
    return pltpu.make_async_copy(src_ref.at[pl.ds(src_row, 1), :], dst_ref.at[pl.ds(dst_row, 1), :], sem)


def _rows_wait(src_ref, dst_ref, n_rows, sem):
    pltpu.make_async_copy(src_ref.at[pl.ds(0, n_rows), :], dst_ref.at[pl.ds(0, n_rows), :], sem).wait()


def _slots_kernel(pos_ref, src_ref, *, pad_mask):
    tb = pos_ref.shape[-1] // TOP_K
    base = pl.program_id(0) * tb

    @pl.when(pl.program_id(0) == 0)
    def _():
        def clear(p, carry):
            src_ref[p] = p & pad_mask
            return carry
        lax.fori_loop(0, src_ref.shape[0], clear, 0, unroll=8)

    def body(j, carry):
        for k in range(TOP_K):
            src_ref[pos_ref[0, TOP_K * j + k]] = base + j
        return carry

    lax.fori_loop(0, tb, body, 0, unroll=4)


def _slots(pos, n_rows):
    m = pos.shape[0]
    tb = _row_tile(m, 512)
    pad_mask = (1 << (m.bit_length() - 1)) - 1
    return pl.pallas_call(
        functools.partial(_slots_kernel, pad_mask=pad_mask),
        grid=(m // tb,),
        in_specs=[pl.BlockSpec((None, 1, TOP_K * tb), lambda i: (i, 0, 0), memory_space=pltpu.SMEM)],
        out_specs=pl.BlockSpec(memory_space=pltpu.SMEM),
        out_shape=jax.ShapeDtypeStruct((n_rows,), I32),
        compiler_params=_params(("arbitrary",)),
        name="moe_slots",
    )(pos.reshape(m // tb, 1, TOP_K * tb))


def _gather_kernel(ns_ref, src_ref, h_ref, xs_ref, buf_ref, sem):
    n_sub = ns_ref[pl.program_id(0)]
    for s in range(EXPERT_WINDOW_SUBS):
        lo = s * EXPERT_ROWS

        @pl.when(s < n_sub)
        def _():
            def body(r2, carry):
                for p in range(2):
                    r = lo + 2 * r2 + p
                    _row_copy(h_ref, src_ref[0, r], buf_ref, r, sem.at[s]).start(priority=p)
                return carry

            lax.fori_loop(0, EXPERT_ROWS // 2, body, 0, unroll=4)

    for s in range(EXPERT_WINDOW_SUBS):
        lo = s * EXPERT_ROWS
        rows = slice(lo, lo + EXPERT_ROWS)

        @pl.when(s < n_sub)
        def _():
            _rows_wait(h_ref, buf_ref.at[rows, :], EXPERT_ROWS, sem.at[s])
            xs_ref[rows, :] = buf_ref[rows, :].astype(xs_ref.dtype)

        @pl.when(s >= n_sub)
        def _():
            xs_ref[rows, :] = jnp.zeros((EXPERT_ROWS, xs_ref.shape[1]), xs_ref.dtype)


def _gather(h32, src, win_subs):
    m, d = h32.shape
    win = EXPERT_WINDOW_SUBS * EXPERT_ROWS
    assert m >= EXPERT_ROWS
    n_win = src.shape[0] // win
    return pl.pallas_call(
        _gather_kernel,
        grid=(n_win,),
        in_specs=[pl.BlockSpec(memory_space=pltpu.SMEM),
                  pl.BlockSpec((None, 1, win), lambda b: (b, 0, 0), memory_space=pltpu.SMEM),
                  pl.BlockSpec(memory_space=pl.ANY)],
        out_specs=pl.BlockSpec((win, d), lambda b: (b, 0)),
        out_shape=jax.ShapeDtypeStruct((n_win * win, d), BF16),
        scratch_shapes=[pltpu.VMEM((win, d), F32),
                        pltpu.SemaphoreType.DMA((EXPERT_WINDOW_SUBS,))],
        compiler_params=_params(("arbitrary",)),
        name="moe_gather",
    )(win_subs, src.reshape(n_win, 1, win), h32)


def _experts_kernel(we_ref, ns_ref, nv_ref, x_ref, wg_ref, wu_ref, wd_ref, y_ref,
                    wg_bf, wu_bf, wd_bf):
    del we_ref
    b = pl.program_id(0)

    @pl.when(pl.program_id(1) == 0)
    def _():
        y_ref[...] = jnp.zeros_like(y_ref)

    def sub_block(s):
        rows = slice(s * EXPERT_ROWS, (s + 1) * EXPERT_ROWS)
        h = x_ref[rows, :]
        gate = _dot(h, wg_bf[...])
        act = (gate * _sigmoid(gate)) * _dot(h, wu_bf[...])
        y_ref[rows, :] += _dot(act.astype(BF16), wd_bf[...])

    @pl.when(b < nv_ref[0])
    def _():
        wg_bf[...] = wg_ref[...].astype(BF16)
        wu_bf[...] = wu_ref[...].astype(BF16)
        wd_bf[...] = wd_ref[...].astype(BF16)
        sub_block(0)
        for s in range(1, EXPERT_WINDOW_SUBS):
            @pl.when(s < ns_ref[b])
            def _():
                sub_block(s)


def _experts(xs, wg, wu, wd, layer, win_expert, win_subs, n_valid, tf):
    n_rows, d = xs.shape
    ff = wg.shape[-1]
    n_f = ff // tf
    win = EXPERT_WINDOW_SUBS * EXPERT_ROWS
    xmap = lambda b, f, we, ns, nv: (jnp.minimum(b, nv[0] - 1), 0)
    fidx = lambda b, f, nv: jnp.where(b < nv[0], f, n_f - 1)
    once = pl.Buffered(1)
    grid_spec = pltpu.PrefetchScalarGridSpec(
        num_scalar_prefetch=3,
        grid=(n_rows // win, n_f),
        in_specs=[pl.BlockSpec((win, d), xmap),
                  pl.BlockSpec((None, None, d, tf),
                               lambda b, f, we, ns, nv: (layer, we[b], 0, fidx(b, f, nv))),
                  pl.BlockSpec((None, None, d, tf),
                               lambda b, f, we, ns, nv: (layer, we[b], 0, fidx(b, f, nv))),
                  pl.BlockSpec((None, None, tf, d),
                               lambda b, f, we, ns, nv: (layer, we[b], fidx(b, f, nv), 0))],
        out_specs=pl.BlockSpec((win, d), lambda b, f, we, ns, nv: (b, 0)),
        scratch_shapes=[pltpu.VMEM((d, tf), BF16), pltpu.VMEM((d, tf), BF16),
                        pltpu.VMEM((tf, d), BF16)],
    )
    return pl.pallas_call(
        _experts_kernel,
        grid_spec=grid_spec,
        out_shape=jax.ShapeDtypeStruct((n_rows, d), F32),
        compiler_params=_params(("arbitrary", "arbitrary"), vmem=EXPERT_VMEM_LIMIT),
        name="moe_experts",
    )(win_expert, win_subs, n_valid, xs, wg, wu, wd)


def _combine_kernel(pos_ref, x_ref, gate_ref, g_ref, y_ref, o_ref, ya_ref, yb_ref, sem):
    tb = x_ref.shape[0]

    def body(j, carry):
        _row_copy(y_ref, pos_ref[0, TOP_K * j], ya_ref, j, sem).start(priority=0)
        _row_copy(y_ref, pos_ref[0, TOP_K * j + 1], yb_ref, j, sem).start(priority=1)
        return carry

    lax.fori_loop(0, tb, body, 0, unroll=4)
    _rows_wait(y_ref, ya_ref, tb, sem)
    _rows_wait(y_ref, yb_ref, tb, sem)
    gate = gate_ref[...]
    x_new = x_ref[...] + (gate[:, 0:1] * ya_ref[...] + gate[:, 1:2] * yb_ref[...])
    o_ref[...] = _rms(x_new, g_ref[...]).astype(o_ref.dtype)


def _combine(x, y, pos, gates, g, out_dtype, row0, n_rows):
    d = x.shape[1]
    tb = _row_tile(n_rows, 512)
    assert row0 % tb == 0
    blk0 = row0 // tb
    pos = pos.reshape(-1)[TOP_K * row0:TOP_K * (row0 + n_rows)].reshape(n_rows // tb, 1, TOP_K * tb)
    return pl.pallas_call(
        _combine_kernel,
        grid=(n_rows // tb,),
        in_specs=[pl.BlockSpec((None, 1, TOP_K * tb), lambda i: (i, 0, 0), memory_space=pltpu.SMEM),
                  pl.BlockSpec((tb, d), lambda i: (blk0 + i, 0)),
                  pl.BlockSpec((tb, LANES), lambda i: (blk0 + i, 0)),
                  pl.BlockSpec((1, d), lambda i: (0, 0)),
                  pl.BlockSpec(memory_space=pl.ANY)],
        out_specs=pl.BlockSpec((tb, d), lambda i: (i, 0)),
        out_shape=jax.ShapeDtypeStruct((n_rows, d), out_dtype),
        scratch_shapes=[pltpu.VMEM((tb, d), F32), pltpu.VMEM((tb, d), F32),
                        pltpu.SemaphoreType.DMA(())],
        compiler_params=_params(("arbitrary",)),
        name="moe_combine",
    )(pos, x, gates, g.reshape(1, d), y)


def _moe(h, h32, x, w_router, wg, wu, wd, layer, g, out_dtype, m_prompt):
    m, d = x.shape
    sub, wsubs = EXPERT_ROWS, EXPERT_WINDOW_SUBS
    win = sub * wsubs
    info, gates, cnt = _router(h, w_router)

    counts = cnt[0, :N_EXPERTS].astype(I32)
    n_sub = (counts + (sub - 1)) // sub
    n_win = (n_sub + (wsubs - 1)) // wsubs
    per_win = (n_sub + jnp.maximum(n_win, 1) - 1) // jnp.maximum(n_win, 1)
    win_end = jnp.cumsum(n_win)
    win_first = win_end - n_win
    n_valid = win_end[-1]
    max_subs = (TOP_K * m + N_EXPERTS * (sub - 1)) // sub
    max_windows = (max_subs + N_EXPERTS * (wsubs - 1)) // wsubs
    wid = jnp.arange(max_windows, dtype=I32)
    owner = jnp.minimum(jnp.sum((wid[:, None] >= win_end[None, :]).astype(I32), axis=1),
                        N_EXPERTS - 1)
    subs_here = jnp.clip(n_sub[owner] - (wid - win_first[owner]) * per_win[owner], 0, per_win[owner])
    used = wid < n_valid
    win_expert = jnp.where(used, owner, owner[n_valid - 1])
    win_subs = jnp.where(used, subs_here, 0)

    e_tok, r_tok = info[:, :TOP_K], info[:, TOP_K:2 * TOP_K]
    rows_per_win = (per_win * sub)[e_tok]
    max_win_per_expert = -(-(-(-m // sub)) // wsubs)
    win_in_expert = sum((r_tok >= k * rows_per_win).astype(I32) for k in range(1, max_win_per_expert))
    pos = (win_first[e_tok] + win_in_expert) * win + (r_tok - win_in_expert * rows_per_win)

    src = _slots(pos, max_windows * win)
    xs = _gather(h32, src, win_subs)
    y = _experts(xs, wg, wu, wd, layer, win_expert, win_subs, n_valid.reshape(1), 512)
    return (_combine(x, y, pos, gates, g, out_dtype, 0, m_prompt),
            _combine(x, y, pos, gates, g, out_dtype, m_prompt, m - m_prompt))


def _cast_pad_cols_kernel(w_ref, o_ref):
    ff = w_ref.shape[1]
    o_ref[:, :ff] = w_ref[...].astype(o_ref.dtype)
    o_ref[:, ff:] = jnp.zeros((o_ref.shape[0], o_ref.shape[1] - ff), o_ref.dtype)


def _cast_pad_rows_kernel(w_ref, o_ref, *, real_blocks):
    @pl.when(pl.program_id(0) < real_blocks)
    def _():
        o_ref[...] = w_ref[...].astype(o_ref.dtype)

    @pl.when(pl.program_id(0) >= real_blocks)
    def _():
        o_ref[...] = jnp.zeros_like(o_ref)


def _bf16_padded(w, axis, mult):
    rows, cols = w.shape
    if axis == 1:
        assert cols % LANES == 0
        tr = _row_tile(rows, 256)
        padded = cols + (-cols) % mult
        return pl.pallas_call(
            _cast_pad_cols_kernel,
            grid=(rows // tr,),
            in_specs=[pl.BlockSpec((tr, cols), lambda i: (i, 0))],
            out_specs=pl.BlockSpec((tr, padded), lambda i: (i, 0)),
            out_shape=jax.ShapeDtypeStruct((rows, padded), BF16),
            compiler_params=_params(("parallel",)),
            name="cast_pad_cols",
        )(w)
    assert rows % LANES == 0
    padded = rows + (-rows) % mult
    real_blocks = rows // LANES
    return pl.pallas_call(
        functools.partial(_cast_pad_rows_kernel, real_blocks=real_blocks),
        grid=(padded // LANES,),
        in_specs=[pl.BlockSpec((LANES, cols), lambda i: (jnp.minimum(i, real_blocks - 1), 0))],
        out_specs=pl.BlockSpec((LANES, cols), lambda i: (i, 0)),
        out_shape=jax.ShapeDtypeStruct((padded, cols), BF16),
        compiler_params=_params(("parallel",)),
        name="cast_pad_rows",
    )(w)


def kernel(x_prompt, x_sample, state_hgrn, state_rglru, state_conv, hgrn_lb_logits, norm_mix, w_in,
           hgrn_gnorm, rg_conv_w, rg_conv_b, rg_wa, rg_ba, rg_wx, rg_bx, rg_a_param, w_br_a, w_br_b,
           w_out, norm_ffn, ffn_w_gate, ffn_w_up, ffn_w_down, moe_router, moe_w_gate, moe_w_up,
           moe_w_down, norm_final):
    depth = w_in.shape[0]
    batch, seq, d = x_prompt.shape
    n_seq = x_sample.shape[0]
    m_prompt = batch * seq
    m = m_prompt + n_seq

    sm = jax.nn.softmax(hgrn_lb_logits.astype(F32), axis=0)
    lbs = jnp.maximum(jnp.cumsum(sm, axis=0) - sm[0:1], 0.0)
    one_m_lbs = 1.0 - lbs

    x = jnp.concatenate([x_prompt.reshape(m_prompt, d), x_sample.reshape(n_seq, d)], axis=0)
    h = _rmsnorm(x, norm_mix[0], BF16)

    chunk = 256 if seq % 256 == 0 else seq
    rg_rows = 512 if seq % 512 == 0 else chunk
    hg_p, rg_p, cv_p, rg_s, cv_s = [], [], [], [], []
    hg_s = None
    for l in range(depth):
        proj = _in_proj(h, w_in, l)
        vrow = lambda a: a[l].reshape(1, -1)
        o_all, s_prompt = _hgrn_prompt(proj, vrow(lbs), vrow(one_m_lbs), vrow(hgrn_gnorm),
                                       batch, seq, m, chunk)
        o_all, hg_s = _hgrn_sample(proj, vrow(lbs), vrow(one_m_lbs), vrow(hgrn_gnorm),
                                   state_hgrn, l, o_all, m_prompt, hg_s)
        rg_args = (rg_conv_w[l], vrow(rg_conv_b), rg_wa[l], rg_ba[l].reshape(RG_BLOCKS, 1, RG_BW),
                   rg_wx[l], rg_bx[l].reshape(RG_BLOCKS, 1, RG_BW), vrow(rg_a_param))
        y_all, h_prompt = _rg_prompt(proj, *rg_args, batch, seq, m, rg_rows)
        y_all, h_sample = _rg_sample(proj, state_conv[l], state_rglru[l], *rg_args, y_all, m_prompt)

        dense = l % 2 == 0
        merged = _merge_out(o_all, y_all, proj, x, w_br_a[l].astype(BF16), w_br_b[l].astype(BF16),
                            w_out[l].astype(BF16), norm_ffn[l], with_h32=not dense)
        x, h2 = merged[0], merged[1]

        last = l == depth - 1
        g_next = norm_final if last else norm_mix[l + 1]
        out_dtype = F32 if last else BF16
        j = l // 2
        if dense:
            wg = _bf16_padded(ffn_w_gate[j], 1, 512)
            wu = _bf16_padded(ffn_w_up[j], 1, 512)
            wd = _bf16_padded(ffn_w_down[j], 0, 512)
            h, x = _ffn(h2, x, wg, wu, wd, g_next, out_dtype, 512)
        else:
            assert last
            y_prompt, y_sample = _moe(h2, merged[2], x, moe_router[j], moe_w_gate, moe_w_up,
                                      moe_w_down, j, g_next, out_dtype, m_prompt)

        xr_cols = slice(2 * HG_F + 2 * HG_I, 2 * HG_F + 2 * HG_I + RG_WIDTH)
        tail = jnp.stack([proj[(b + 1) * seq - (CONV_W - 1):(b + 1) * seq, xr_cols]
                          for b in range(batch)])
        hg_p.append(s_prompt)
        rg_p.append(h_prompt.reshape(batch, RG_WIDTH))
        cv_p.append(tail)
        rg_s.append(h_sample)
        cv_s.append(jnp.concatenate([state_conv[l][:, 1:], proj[m_prompt:, None, xr_cols]], axis=1))

    y_prompt = y_prompt.reshape(batch, seq, d)
    y_sample = y_sample.reshape(n_seq, 1, d)
    return (y_prompt, y_sample, jnp.stack(hg_p), jnp.stack(rg_p), jnp.stack(cv_p),
            hg_s, jnp.stack(rg_s), jnp.stack(cv_s))
```
